```python
import jax, jax.numpy as jnp
from jax import lax
import numpy as np

D_MODEL = 1024
BATCH = 8
SEQ = 2048
DEPTH = 4
DEC_BATCH = 32
DEC_SEQ = 8
PAST_LEN = 8192
PAGE_SIZE = 128

N_A = DEPTH // 2
N_B = DEPTH - N_A
POOL_WINDOWS = (2, 4, 8, 16)
N_POOL_GROUPS = len(POOL_WINDOWS)
POOL_GROUP_DIM = D_MODEL // N_POOL_GROUPS
POOL_BUF = max(POOL_WINDOWS) - 1
HEAD_DIM = 64
N_HEADS = D_MODEL // HEAD_DIM
DIL_GROUPS = ((128, 1), (512, 4), (2048, 16))
N_DIL = len(DIL_GROUPS)
MAX_WINDOW = max(w for w, _ in DIL_GROUPS)
BAND = max(w // d for w, d in DIL_GROUPS)
D_FF = 2816
RMS_EPS = 1e-6
NEG_INF = -1e30
ATTN_SCALE = HEAD_DIM ** -0.5

kernel_name = 'yoco_pool_dilated_attn_step'


def alibi_slopes(n):
    return 2.0 ** (-8.0 * jnp.arange(1, n + 1, dtype=jnp.float32) / n)


def rmsnorm(x, g):
    xf = x.astype(jnp.float32)
    y = xf * lax.rsqrt(jnp.mean(xf * xf, axis=-1, keepdims=True) + RMS_EPS)
    return (y * g.astype(jnp.float32)).astype(x.dtype)


def swiglu(x, w_gate, w_up, w_down):
    return (jax.nn.silu(x @ w_gate) * (x @ w_up)) @ w_down


def causal_multiscale_pool(u, prev, pos0):
    b, s, _ = u.shape
    n_prev = prev.shape[1]
    ext = jnp.concatenate([prev, u], axis=1).astype(jnp.float32)
    csum = jnp.pad(jnp.cumsum(ext, axis=1), ((0, 0), (1, 0), (0, 0)))
    end = n_prev + jnp.arange(s) + 1
    pos = pos0 + jnp.arange(s)
    groups = []
    for gi, win in enumerate(POOL_WINDOWS):
        ch = slice(gi * POOL_GROUP_DIM, (gi + 1) * POOL_GROUP_DIM)
        tot = csum[:, end, ch] - csum[:, jnp.maximum(end - win, 0), ch]
        cnt = jnp.minimum(pos + 1, win).astype(jnp.float32)
        groups.append(tot / cnt[None, :, None])
    pooled = jnp.stack(groups, axis=2)
    return pooled - u.astype(jnp.float32).reshape(b, s, N_POOL_GROUPS, POOL_GROUP_DIM)


def pool_mixer(hn, prev, pos0, w_in, w_grp, scale, w_out):
    b, s, d = hn.shape
    u = hn @ w_in
    z = causal_multiscale_pool(u, prev, pos0).astype(hn.dtype)
    z = jnp.einsum('bsgc,gcd->bsgd', z, w_grp).reshape(b, s, d)
    y = (z * scale) @ w_out
    rows = jnp.concatenate([prev, u], axis=1)[:, -POOL_BUF:]
    return y, rows


def softmax_stats(s):
    m = jnp.max(s, axis=-1, keepdims=True)
    p = jnp.exp(s - m)
    l = jnp.sum(p, axis=-1, keepdims=True)
    return p / l, (m + jnp.log(l))[..., 0]


def dilated_band_attn(q, k, v, dil, n_steps, slopes):
    b, s, h, dh = q.shape
    blk = dil * BAND
    sp = -(-s // blk) * blk
    n = sp // dil
    nb = n // BAND

    def split(t):
        t = jnp.pad(t, ((0, 0), (0, sp - s), (0, 0), (0, 0)))
        t = t.reshape(b, n, dil, h, dh).transpose(0, 2, 1, 3, 4)
        return t.reshape(b, dil, nb, BAND, h, dh)

    def with_prev(t):
        prev = jnp.pad(t, ((0, 0), (0, 0), (1, 0), (0, 0), (0, 0), (0, 0)))[:, :, :-1]
        return jnp.concatenate([prev, t], axis=3)

    qb = split(q)
    kk = with_prev(split(k))
    vv = with_prev(split(v))
    sc = jnp.einsum('brnqhe,brnkhe->brnhqk', qb, kk).astype(jnp.float32) * ATTN_SCALE
    qi = jnp.arange(BAND)[:, None]
    ki = jnp.arange(2 * BAND)[None, :]
    step = qi + BAND - ki
    kstep = jnp.arange(nb)[:, None, None] * BAND + ki[None] - BAND
    valid = (step >= 0)[None] & (step <= n_steps)[None] & (kstep >= 0)
    bias = -slopes[:, None, None] * (step * dil).astype(jnp.float32)[None]
    sc = jnp.where(valid[:, None], sc + bias, NEG_INF)
    p, lse = softmax_stats(sc)
    o = jnp.einsum('brnhqk,brnkhe->brnqhe', p.astype(v.dtype), vv)
    o = o.reshape(b, dil, n, h, dh).transpose(0, 2, 1, 3, 4).reshape(b, sp, h, dh)[:, :s]
    lse = lse.transpose(0, 1, 2, 4, 3).reshape(b, dil, n, h).transpose(0, 2, 1, 3).reshape(b, sp, h)[:, :s]
    return o, lse


def dilated_gather_attn(q, k_all, v_all, n_past, dil, n_steps, slopes):
    t = q.shape[1]
    dist = jnp.arange(n_steps + 1) * dil
    idx = n_past + jnp.arange(t)[:, None] - dist[None, :]
    valid = idx >= 0
    idx = jnp.maximum(idx, 0)
    kg = k_all[:, idx]
    vg = v_all[:, idx]
    sc = jnp.einsum('bthe,btkhe->bhtk', q, kg).astype(jnp.float32) * ATTN_SCALE
    sc = sc - slopes[:, None, None] * dist.astype(jnp.float32)[None, None, :]
    sc = jnp.where(valid[None, None], sc, NEG_INF)
    p, lse = softmax_stats(sc)
    o = jnp.einsum('bhtk,btkhe->bthe', p.astype(v_all.dtype), vg)
    return o, lse.transpose(0, 2, 1)


def dilated_mixer(hn, w_q, w_o, k_all, v_all, n_past, banded, slopes):
    b, s, d = hn.shape
    q = (hn @ w_q).reshape(b, s, N_DIL, N_HEADS, HEAD_DIM)
    outs, lses = [], []
    for gi, (win, dil) in enumerate(DIL_GROUPS):
        if banded:
            o, lse = dilated_band_attn(q[:, :, gi], k_all, v_all, dil, win // dil, slopes)
        else:
            o, lse = dilated_gather_attn(q[:, :, gi], k_all, v_all, n_past, dil, win // dil, slopes)
        outs.append(o)
        lses.append(lse)
    wts = jax.nn.softmax(jnp.stack(lses), axis=0)
    o = jnp.sum(wts[..., None] * jnp.stack(outs).astype(jnp.float32), axis=0)
    return o.reshape(b, s, d).astype(hn.dtype) @ w_o


def trunk(x, pool_prev, k_past, v_past, pos0, norm_g, ffn_w_gate, ffn_w_up, ffn_w_down,
          pool_w_in, pool_w_grp, pool_scale, pool_w_out, kv_norm, w_k, w_v, attn_w_q, attn_w_o):
    b, s, _ = x.shape
    slopes = alibi_slopes(N_HEADS)
    banded = k_past is None
    h = x
    pool_rows = []
    k_all = v_all = None
    n_past = 0
    buf = 0
    for layer in range(DEPTH):
        if layer == N_A:
            kvn = rmsnorm(h, kv_norm)
            k_new = (kvn @ w_k).reshape(b, s, N_HEADS, HEAD_DIM)
            v_new = (kvn @ w_v).reshape(b, s, N_HEADS, HEAD_DIM)
            if banded:
                k_all, v_all = k_new, v_new
                buf = min(MAX_WINDOW, s)
            else:
                n_past = k_past.shape[1]
                k_all = jnp.concatenate([k_past, k_new], axis=1)
                v_all = jnp.concatenate([v_past, v_new], axis=1)
                buf = n_past
        g = norm_g[layer]
        f = swiglu(rmsnorm(h, g[0]), ffn_w_gate[layer, 0], ffn_w_up[layer, 0], ffn_w_down[layer, 0])
        h = h + 0.5 * rmsnorm(f, g[1])
        hn = rmsnorm(h, g[2])
        if layer < N_A:
            y, rows = pool_mixer(hn, pool_prev[layer], pos0, pool_w_in[layer], pool_w_grp[layer],
                                 pool_scale[layer], pool_w_out[layer])
            pool_rows.append(rows)
        else:
            j = layer - N_A
            y = dilated_mixer(hn, attn_w_q[j], attn_w_o[j], k_all, v_all, n_past, banded, slopes)
        h = h + rmsnorm(y, g[3])
        f = swiglu(rmsnorm(h, g[4]), ffn_w_gate[layer, 1], ffn_w_up[layer, 1], ffn_w_down[layer, 1])
        h = h + 0.5 * rmsnorm(f, g[5])
    return h, jnp.stack(pool_rows), k_all[:, -buf:], v_all[:, -buf:]


def setup_inputs(seed: int = 0) -> dict:
    key = jax.random.key(seed)
    ks = jax.random.split(key, 20)
    nrm = jax.random.normal
    kv_buf = min(MAX_WINDOW, PAST_LEN)
    d, f, c = D_MODEL, D_FF, POOL_GROUP_DIM
    return {
        'x_prompt': nrm(ks[0], (BATCH, SEQ, d), jnp.float32),
        'x_sample': nrm(ks[1], (DEC_BATCH, DEC_SEQ, d), jnp.float32),
        'state_pool': nrm(ks[2], (N_A, DEC_BATCH, POOL_BUF, d), jnp.float32),
        'cache_k': nrm(ks[3], (DEC_BATCH, kv_buf, N_HEADS, HEAD_DIM), jnp.float32),
        'cache_v': nrm(ks[4], (DEC_BATCH, kv_buf, N_HEADS, HEAD_DIM), jnp.float32),
        'norm_g': 1.0 + 0.05 * nrm(ks[5], (DEPTH, 6, d), jnp.float32),
        'ffn_w_gate': nrm(ks[6], (DEPTH, 2, d, f), jnp.float32) * d ** -0.5,
        'ffn_w_up': nrm(ks[7], (DEPTH, 2, d, f), jnp.float32) * d ** -0.5,
        'ffn_w_down': nrm(ks[8], (DEPTH, 2, f, d), jnp.float32) * f ** -0.5,
        'pool_w_in': nrm(ks[9], (N_A, d, d), jnp.float32) * d ** -0.5,
        'pool_w_grp': nrm(ks[10], (N_A, N_POOL_GROUPS, c, c), jnp.float32) * c ** -0.5,
        'pool_scale': 1.0 + 0.1 * nrm(ks[11], (N_A, d), jnp.float32),
        'pool_w_out': nrm(ks[12], (N_A, d, d), jnp.float32) * d ** -0.5,
        'kv_norm': 1.0 + 0.05 * nrm(ks[13], (d,), jnp.float32),
        'w_k': nrm(ks[14], (d, d), jnp.float32) * d ** -0.5,
        'w_v': nrm(ks[15], (d, d), jnp.float32) * d ** -0.5,
        'attn_w_q': nrm(ks[16], (N_B, d, N_DIL * d), jnp.float32) * d ** -0.5,
        'attn_w_o': nrm(ks[17], (N_B, d, d), jnp.float32) * d ** -0.5,
    }


def reference(x_prompt, x_sample, state_pool, cache_k, cache_v, norm_g, ffn_w_gate, ffn_w_up,
              ffn_w_down, pool_w_in, pool_w_grp, pool_scale, pool_w_out, kv_norm, w_k, w_v,
              attn_w_q, attn_w_o):
    empty_pool = jnp.zeros((N_A, x_prompt.shape[0], 0, D_MODEL), x_prompt.dtype)
    y_prompt, pool_p, k_p, v_p = trunk(
        x_prompt, empty_pool, None, None, 0, norm_g, ffn_w_gate, ffn_w_up, ffn_w_down,
        pool_w_in, pool_w_grp, pool_scale, pool_w_out, kv_norm, w_k, w_v, attn_w_q, attn_w_o)
    y_sample, pool_s, k_s, v_s = trunk(
        x_sample, state_pool, cache_k, cache_v, PAST_LEN, norm_g, ffn_w_gate, ffn_w_up, ffn_w_down,
        pool_w_in, pool_w_grp, pool_scale, pool_w_out, kv_norm, w_k, w_v, attn_w_q, attn_w_o)
    return (y_prompt, y_sample, pool_p, k_p, v_p, pool_s, k_s, v_s)
```

```python
import functools

import jax
import jax.numpy as jnp
from jax import lax
from jax.experimental import pallas as pl
from jax.experimental.pallas import tpu as pltpu

D_MODEL = 1024
D_FF = 2816
HEAD_DIM = 64
N_HEADS = D_MODEL // HEAD_DIM
POOL_WINDOWS = (2, 4, 8, 16)
POOL_GROUP_DIM = D_MODEL // len(POOL_WINDOWS)
POOL_BUF = max(POOL_WINDOWS) - 1
POOL_HALO = 16
DIL_GROUPS = ((128, 1), (512, 4), (2048, 16))
N_DIL = len(DIL_GROUPS)
BAND = 128
PAST_LEN = 8192
RMS_EPS = 1e-6
NEG_INF = -1e30
ATTN_SCALE = HEAD_DIM ** -0.5
LANES = 128
VMEM_LIMIT_BYTES = 56 * 1024 * 1024

F32 = jnp.float32
BF16 = jnp.bfloat16


def _params(n_axes):
    return pltpu.CompilerParams(dimension_semantics=("arbitrary",) * n_axes,
                                vmem_limit_bytes=VMEM_LIMIT_BYTES)


def _resident(shape):
    zeros = (0,) * len(shape)
    return pl.BlockSpec(shape, lambda *_: zeros, pipeline_mode=pl.Buffered(1))


def _rms(x, g):
    return x * lax.rsqrt(jnp.mean(x * x, axis=-1, keepdims=True) + RMS_EPS) * g


def _dot(a, b):
    return jnp.dot(a, b, preferred_element_type=F32)


def _token_tile(t):
    for tm in (512, 256, 128, 64, 32, 16, 8):
        if t % tm == 0:
            return tm
    raise ValueError(f"token count {t} is not a multiple of 8")


def _ffn_body(x_ref, g_ref, wg_ref, wu_ref, wd_ref, o_ref):
    x = x_ref[...]
    xn = _rms(x, g_ref[0:1, :]).astype(BF16)
    hg = _dot(xn, wg_ref[...])
    hu = _dot(xn, wu_ref[...])
    a = (hg / (1.0 + jnp.exp(-hg)) * hu).astype(BF16)
    f = _dot(a, wd_ref[...])
    o_ref[...] = x + 0.5 * _rms(f, g_ref[1:2, :])


def _ffn(x, g2, wg, wu, wd):
    t, d = x.shape
    tm = _token_tile(t)
    tile = pl.BlockSpec((tm, d), lambda i: (i, 0))
    return pl.pallas_call(
        _ffn_body,
        grid=(t // tm,),
        in_specs=[tile, _resident(g2.shape), _resident(wg.shape), _resident(wu.shape),
                  _resident(wd.shape)],
        out_specs=tile,
        out_shape=jax.ShapeDtypeStruct((t, d), F32),
        compiler_params=_params(1),
        name="ffn",
    )(x, g2, wg, wu, wd)


def _proj_body(x_ref, g_ref, *refs, out_cfg):
    n_w = max(wi for wi, _, _ in out_cfg) + 1
    w_refs, o_refs = refs[:n_w], refs[n_w:]
    xn = _rms(x_ref[...], g_ref[...]).astype(BF16)
    ys = [_dot(xn, w[...]) for w in w_refs]
    for o_ref, (wi, _, scale) in zip(o_refs, out_cfg):
        y = ys[wi] if scale == 1.0 else ys[wi] * scale
        o_ref[...] = y.astype(o_ref.dtype)


def _norm_proj(x, g, ws, out_cfg):
    t, d = x.shape
    tm = _token_tile(t)
    tile = pl.BlockSpec((tm, d), lambda i: (i, 0))
    return pl.pallas_call(
        functools.partial(_proj_body, out_cfg=out_cfg),
        grid=(t // tm,),
        in_specs=[tile, _resident(g.shape)] + [_resident(w.shape) for w in ws],
        out_specs=[pl.BlockSpec((tm, ws[wi].shape[1]), lambda i: (i, 0)) for wi, _, _ in out_cfg],
        out_shape=[jax.ShapeDtypeStruct((t, ws[wi].shape[1]), dt) for wi, dt, _ in out_cfg],
        compiler_params=_params(1),
        name="norm_proj",
    )(x, g, *ws)


def _oproj_body(o_ref, h_ref, g_ref, w_ref, out_ref):
    y = _dot(o_ref[...], w_ref[...])
    out_ref[...] = h_ref[...] + _rms(y, g_ref[...])


def _out_proj(o, h, g, w):
    t, d = h.shape
    tm = _token_tile(t)
    tile = pl.BlockSpec((tm, d), lambda i: (i, 0))
    return pl.pallas_call(
        _oproj_body,
        grid=(t // tm,),
        in_specs=[tile, tile, _resident(g.shape), _resident(w.shape)],
        out_specs=tile,
        out_shape=jax.ShapeDtypeStruct((t, d), F32),
        compiler_params=_params(1),
        name="out_proj",
    )(o, h, g, w)


def _pool_body(h_ref, prev_ref, g_ref, win_ref, wgrp_ref, scale_ref, wout_ref,
               out_ref, rows_ref, ext_ref, *, pos0, bb, ts):
    d = D_MODEL
    j = pl.program_id(1)

    @pl.when(j == 0)
    def _():
        ext_ref[:, 0:POOL_HALO, :] = prev_ref[...]

    @pl.when(j > 0)
    def _():
        ext_ref[:, 0:POOL_HALO, :] = ext_ref[:, ts:ts + POOL_HALO, :]

    h = h_ref[...].reshape(bb * ts, d)
    hn = _rms(h, g_ref[0:1, :]).astype(BF16)
    u = _dot(hn, win_ref[...])
    ext_ref[:, POOL_HALO:POOL_HALO + ts, :] = u.reshape(bb, ts, d)
    rows_ref[...] = ext_ref[:, ts:ts + POOL_HALO, :]

    pos = pos0 + j * ts + lax.broadcasted_iota(jnp.int32, (1, ts, 1), 1)
    zs = []
    for gi, win in enumerate(POOL_WINDOWS):
        cs = slice(gi * POOL_GROUP_DIM, (gi + 1) * POOL_GROUP_DIM)
        cur = ext_ref[:, POOL_HALO:POOL_HALO + ts, cs]
        tot = cur
        for k in range(1, win):
            tot = tot + ext_ref[:, POOL_HALO - k:POOL_HALO - k + ts, cs]
        cnt = jnp.minimum(pos + 1, win).astype(F32)
        z = (tot / cnt - cur).reshape(bb * ts, POOL_GROUP_DIM).astype(BF16)
        zs.append(_dot(z, wgrp_ref[gi]))
    zc = (jnp.concatenate(zs, axis=-1) * scale_ref[...]).astype(BF16)
    y = _dot(zc, wout_ref[...])
    out_ref[...] = (h + _rms(y, g_ref[1:2, :])).reshape(bb, ts, d)


def _pool_mixer(h, prev, g2, w_in, w_grp, scale, w_out, pos0):
    b, s, d = h.shape
    if s >= 512:
        bb, ts = 1, 512
    else:
        bb, ts = b, s
    n_t = s // ts
    assert s % ts == 0 and ts % 8 == 0 and (n_t == 1 or ts >= POOL_HALO)
    tile = pl.BlockSpec((bb, ts, d), lambda i, j: (i, j, 0))
    halo = pl.BlockSpec((bb, POOL_HALO, d), lambda i, j: (i, 0, 0))
    return pl.pallas_call(
        functools.partial(_pool_body, pos0=pos0, bb=bb, ts=ts),
        grid=(b // bb, n_t),
        in_specs=[tile, halo, _resident(g2.shape), _resident(w_in.shape), _resident(w_grp.shape),
                  _resident(scale.shape), _resident(w_out.shape)],
        out_specs=[tile, halo],
        out_shape=[jax.ShapeDtypeStruct((b, s, d), F32),
                   jax.ShapeDtypeStruct((b, POOL_HALO, d), F32)],
        scratch_shapes=[pltpu.VMEM((bb, POOL_HALO + ts, d), F32)],
        compiler_params=_params(2),
        name="pool_mixer",
    )(h, prev, g2, w_in, w_grp, scale, w_out)


def _alibi_slopes(n):
    return 2.0 ** (-8.0 * jnp.arange(1, n + 1, dtype=F32) / n)


def _band_table(dil, n_steps):
    qi = jnp.arange(BAND)[:, None]
    ki = jnp.arange(2 * BAND)[None, :]
    step = qi + BAND - ki
    valid = (step >= 0) & (step <= n_steps)
    bias = -_alibi_slopes(N_HEADS)[:, None, None] * (step * dil).astype(F32)[None]
    return jnp.where(valid[None], bias, NEG_INF)


def _step_table(n_past, t_new, n_keys):
    dist = n_past + jnp.arange(t_new)[:, None] - jnp.arange(n_keys)[None, :]
    in_buf = (jnp.arange(n_keys) < n_past + t_new)[None, :]
    bias = -_alibi_slopes(N_HEADS)[:, None, None] * dist.astype(F32)[None]
    tabs = []
    for win, dil in DIL_GROUPS:
        valid = (dist >= 0) & (dist % dil == 0) & (dist <= win) & in_buf
        tabs.append(jnp.where(valid[None], bias, NEG_INF))
    return jnp.stack(tabs, axis=1).reshape(N_HEADS * N_DIL * t_new, n_keys)


def _band_attn_body(*refs, nb, has_prev, final):
    refs = list(refs)
    q_ref, kc_ref, vc_ref = refs[:3]
    del refs[:3]
    if nb > 1:
        kp_ref, vp_ref = refs[:2]
        del refs[:2]
    tab_ref = refs.pop(0)
    if has_prev:
        op_ref, lp_ref = refs[:2]
        del refs[:2]
    o_ref = refs.pop(0)
    l_ref = None if final else refs.pop(0)
    lane = lax.broadcasted_iota(jnp.int32, (BAND, LANES), 1)

    def run(with_prev_block):
        lse_tile = jnp.zeros((BAND, LANES), F32)
        lp_tile = lp_ref[0] if has_prev else None
        for h in range(N_HEADS):
            hs = slice(h * HEAD_DIM, (h + 1) * HEAD_DIM)
            qh = q_ref[0, :, hs]
            if with_prev_block:
                kh = jnp.concatenate([kp_ref[0, :, hs], kc_ref[0, :, hs]], axis=0)
                vh = jnp.concatenate([vp_ref[0, :, hs], vc_ref[0, :, hs]], axis=0)
                tab = tab_ref[h]
            else:
                kh = kc_ref[0, :, hs]
                vh = vc_ref[0, :, hs]
                tab = tab_ref[h, :, BAND:2 * BAND]
            s = lax.dot_general(qh, kh, (((1,), (1,)), ((), ())), preferred_element_type=F32) + tab
            m = jnp.max(s, axis=-1, keepdims=True)
            p = jnp.exp(s - m)
            l = jnp.sum(p, axis=-1, keepdims=True)
            o = _dot(p.astype(BF16), vh) / l
            lse = m + jnp.log(l)
            if has_prev:
                lp = lp_tile[:, h:h + 1]
                mm = jnp.maximum(lp, lse)
                wp = jnp.exp(lp - mm)
                wc = jnp.exp(lse - mm)
                den = wp + wc
                o = (wp * op_ref[0, :, hs] + wc * o) / den
                lse = mm + jnp.log(den)
            o_ref[0, :, hs] = o.astype(o_ref.dtype)
            if not final:
                lse_tile = jnp.where(lane == h, lse, lse_tile)
        if not final:
            l_ref[0] = lse_tile

    if nb == 1:
        run(False)
    else:
        n = pl.program_id(2)
        pl.when(n == 0)(lambda: run(False))
        pl.when(n > 0)(lambda: run(True))


def _band_attn_group(q, kb, vb, gi, prev_state, final):
    win, dil = DIL_GROUPS[gi]
    b, s, d = kb.shape
    n = s // dil
    nb = n // BAND
    assert s % (dil * BAND) == 0
    d = D_MODEL
    has_prev = prev_state is not None

    qv = q.reshape(b, n, dil * N_DIL * d)
    kv = kb.reshape(b, n, dil * d)
    vv = vb.reshape(b, n, dil * d)
    blk = (1, BAND, d)
    cur = pl.BlockSpec(blk, lambda i, r, j: (i, j, r))
    in_specs = [pl.BlockSpec(blk, lambda i, r, j: (i, j, r * N_DIL + gi)), cur, cur]
    args = [qv, kv, vv]
    if nb > 1:
        prv = pl.BlockSpec(blk, lambda i, r, j: (i, jnp.maximum(j - 1, 0), r))
        in_specs += [prv, prv]
        args += [kv, vv]
    tab = _band_table(dil, win // dil)
    in_specs.append(_resident(tab.shape))
    args.append(tab)
    stat = pl.BlockSpec((1, BAND, LANES), lambda i, r, j: (i, j, r))
    if has_prev:
        o_prev, l_prev = prev_state
        in_specs += [cur, stat]
        args += [o_prev.reshape(b, n, dil * d), l_prev.reshape(b, n, dil * LANES)]
    out_specs = [cur]
    out_shape = [jax.ShapeDtypeStruct((b, n, dil * d), BF16 if final else F32)]
    if not final:
        out_specs.append(stat)
        out_shape.append(jax.ShapeDtypeStruct((b, n, dil * LANES), F32))
    outs = pl.pallas_call(
        functools.partial(_band_attn_body, nb=nb, has_prev=has_prev, final=final),
        grid=(b, dil, nb),
        in_specs=in_specs,
        out_specs=out_specs,
        out_shape=out_shape,
        compiler_params=_params(3),
        name=f"band_attn_g{gi}",
    )(*args)
    if final:
        return outs[0].reshape(b, s, d)
    return outs[0].reshape(b, s, d), outs[1].reshape(b, s, LANES)


def _band_attention(q, kb, vb):
    state = None
    for gi in range(N_DIL):
        state = _band_attn_group(q, kb, vb, gi, state, final=(gi == N_DIL - 1))
    return state


STEP_HEADS = 4
STEP_COLS = STEP_HEADS * HEAD_DIM


def _step_attn_body(*refs, n_past, t_new, n_keys, write_cache):
    q_refs = refs[:N_DIL]
    ck_ref, cv_ref, kn_ref, vn_ref, tab_ref = refs[N_DIL:N_DIL + 5]
    refs = refs[N_DIL + 5:]
    o_ref = refs[0]
    if write_cache:
        ok_ref, ov_ref = refs[1:3]
        refs = refs[3:]
    else:
        refs = refs[1:]
    kb_ref, vb_ref = refs
    rows_g = N_DIL * t_new
    rows = STEP_HEADS * rows_g

    for c_ref, n_ref, b_ref in ((ck_ref, kn_ref, kb_ref), (cv_ref, vn_ref, vb_ref)):
        b_ref[0:n_past, :] = c_ref[0].astype(BF16)
        b_ref[n_past:n_past + t_new, :] = n_ref[0].astype(BF16)
        b_ref[n_past + t_new:n_keys, :] = jnp.zeros((n_keys - n_past - t_new, STEP_COLS), BF16)
    if write_cache:
        for c_ref, n_ref, out_ref in ((ck_ref, kn_ref, ok_ref), (cv_ref, vn_ref, ov_ref)):
            out_ref[0, 0:n_past - t_new, :] = c_ref[0, t_new:n_past, :]
            out_ref[0, n_past - t_new:n_past, :] = n_ref[0]

    q_rows = jnp.concatenate([r[0] for r in q_refs], axis=0)
    q_all = jnp.concatenate([q_rows] * STEP_HEADS, axis=0)
    row_head = lax.broadcasted_iota(jnp.int32, (rows, STEP_COLS), 0) // rows_g
    col_head = lax.broadcasted_iota(jnp.int32, (rows, STEP_COLS), 1) // HEAD_DIM
    own = row_head == col_head
    q_bd = jnp.where(own, q_all, jnp.zeros_like(q_all))
    s = lax.dot_general(q_bd, kb_ref[...], (((1,), (1,)), ((), ())),
                        preferred_element_type=F32) + tab_ref[...]
    m = jnp.max(s, axis=-1, keepdims=True)
    p = jnp.exp(s - m)
    l = jnp.sum(p, axis=-1, keepdims=True)
    o_full = _dot(p.astype(BF16), vb_ref[...]) / l
    lse = m + jnp.log(l)

    out = jnp.zeros((t_new, STEP_COLS), F32)
    col = lax.broadcasted_iota(jnp.int32, (t_new, STEP_COLS), 1) // HEAD_DIM
    for hh in range(STEP_HEADS):
        base = hh * rows_g
        lses = [lse[base + g * t_new:base + (g + 1) * t_new] for g in range(N_DIL)]
        mm = functools.reduce(jnp.maximum, lses)
        ws = [jnp.exp(x - mm) for x in lses]
        den = functools.reduce(lambda a, b: a + b, ws)
        acc = functools.reduce(
            lambda a, b: a + b,
            [ws[g] * o_full[base + g * t_new:base + (g + 1) * t_new] for g in range(N_DIL)])
        out = jnp.where(col == hh, acc / den, out)
    o_ref[0] = out.astype(o_ref.dtype)


def _step_attention(q, cache_k, cache_v, k_new, v_new, write_cache):
    b, t_new, _ = q.shape
    n_past = cache_k.shape[1]
    d = D_MODEL
    assert t_new % 8 == 0 and n_past % 8 == 0
    n_keys = -(-(n_past + t_new) // LANES) * LANES
    rows = STEP_HEADS * N_DIL * t_new
    tab = _step_table(n_past, t_new, n_keys)
    n_hg = N_HEADS // STEP_HEADS

    def q_spec(g):
        return pl.BlockSpec((1, t_new, STEP_COLS), lambda i, j: (i, 0, g * n_hg + j))

    cache_spec = pl.BlockSpec((1, n_past, STEP_COLS), lambda i, j: (i, 0, j))
    new_spec = pl.BlockSpec((1, t_new, STEP_COLS), lambda i, j: (i, 0, j))
    out_specs = [new_spec]
    out_shape = [jax.ShapeDtypeStruct((b, t_new, d), BF16)]
    if write_cache:
        out_specs += [cache_spec, cache_spec]
        out_shape += [jax.ShapeDtypeStruct((b, n_past, d), F32)] * 2
    return pl.pallas_call(
        functools.partial(_step_attn_body, n_past=n_past, t_new=t_new, n_keys=n_keys,
                          write_cache=write_cache),
        grid=(b, n_hg),
        in_specs=[q_spec(g) for g in range(N_DIL)]
        + [cache_spec, cache_spec, new_spec, new_spec,
           pl.BlockSpec((rows, n_keys), lambda i, j: (j, 0))],
        out_specs=out_specs,
        out_shape=out_shape,
        scratch_shapes=[pltpu.VMEM((n_keys, STEP_COLS), BF16)] * 2,
        compiler_params=_params(2),
        name="step_attn",
    )(q, q, q, cache_k, cache_v, k_new, v_new, tab)


def _trunk(x, pool_prev, cache, pos0, w):
    b, s, d = x.shape
    t = b * s
    depth = w["norm_g"].shape[0]
    n_a = depth // 2
    h = x.reshape(t, d)
    pool_rows = []
    for layer in range(depth):
        g = w["norm_g"][layer]
        if layer == n_a:
            k32, v32, kb, vb = _norm_proj(
                h, w["kv_norm"], [w["w_k"], w["w_v"]],
                out_cfg=((0, F32, 1.0), (1, F32, 1.0), (0, BF16, 1.0), (1, BF16, 1.0)))
        h = _ffn(h, g[0:2], w["ffn_w_gate"][layer, 0], w["ffn_w_up"][layer, 0], w["ffn_w_down"][layer, 0])
        if layer < n_a:
            h3, rows = _pool_mixer(h.reshape(b, s, d), pool_prev[layer], g[2:4], w["pool_w_in"][layer],
                                   w["pool_w_grp"][layer], w["pool_scale"][layer], w["pool_w_out"][layer],
                                   pos0)
            h = h3.reshape(t, d)
            pool_rows.append(rows[:, POOL_HALO - POOL_BUF:])
        else:
            j = layer - n_a
            (q,) = _norm_proj(h, g[2:3], [w["attn_w_q"][j]], out_cfg=((0, BF16, ATTN_SCALE),))
            q = q.reshape(b, s, N_DIL * d)
            if cache is None:
                o = _band_attention(q, kb.reshape(b, s, d), vb.reshape(b, s, d))
            else:
                first = j == 0
                res = _step_attention(q, cache[0], cache[1], k32.reshape(b, s, d), v32.reshape(b, s, d),
                                      write_cache=first)
                o = res[0]
                if first:
                    new_k, new_v = res[1], res[2]
            h = _out_proj(o.reshape(t, d), h, g[3:4], w["attn_w_o"][j])
        h = _ffn(h, g[4:6], w["ffn_w_gate"][layer, 1], w["ffn_w_up"][layer, 1], w["ffn_w_down"][layer, 1])
    if cache is None:
        new_k, new_v = k32.reshape(b, s, d), v32.reshape(b, s, d)
    return h.reshape(b, s, d), jnp.stack(pool_rows), new_k, new_v


def kernel(x_prompt, x_sample, state_pool, cache_k, cache_v, norm_g, ffn_w_gate, ffn_w_up, ffn_w_down,
           pool_w_in, pool_w_grp, pool_scale, pool_w_out, kv_norm, w_k, w_v, attn_w_q, attn_w_o):
    d = D_MODEL
    w = dict(
        norm_g=norm_g,
        ffn_w_gate=ffn_w_gate.astype(BF16), ffn_w_up=ffn_w_up.astype(BF16), ffn_w_down=ffn_w_down.astype(BF16),
        pool_w_in=pool_w_in.astype(BF16), pool_w_grp=pool_w_grp.astype(BF16),
        pool_scale=pool_scale.reshape(-1, 1, d), pool_w_out=pool_w_out.astype(BF16),
        kv_norm=kv_norm.reshape(1, d), w_k=w_k.astype(BF16), w_v=w_v.astype(BF16),
        attn_w_q=attn_w_q.astype(BF16), attn_w_o=attn_w_o.astype(BF16),
    )
    n_a = state_pool.shape[0]
    bp = x_prompt.shape[0]
    bs, n_past = cache_k.shape[:2]
    assert x_prompt.shape[1] <= max(win for win, _ in DIL_GROUPS)

    empty = jnp.zeros((n_a, bp, POOL_HALO, d), F32)
    y_p, pool_p, k_p, v_p = _trunk(x_prompt, empty, None, 0, w)

    prev_s = jnp.pad(state_pool, ((0, 0), (0, 0), (POOL_HALO - POOL_BUF, 0), (0, 0)))
    y_s, pool_s, k_s, v_s = _trunk(x_sample, prev_s, (cache_k.reshape(bs, n_past, d), cache_v.reshape(bs, n_past, d)),
                                   PAST_LEN, w)
    hd = (N_HEADS, HEAD_DIM)
    return (y_p, y_s, pool_p,
            k_p.reshape(k_p.shape[:2] + hd), v_p.reshape(v_p.shape[:2] + hd),
            pool_s,
            k_s.reshape(k_s.shape[:2] + hd), v_s.reshape(v_s.shape[:2] + hd))
```

```python
import functools

import jax
import jax.numpy as jnp
from jax import lax
from jax.experimental import pallas as pl
from jax.experimental.pallas import tpu as pltpu

D_MODEL = 1024
D_FF = 2816
HEAD_DIM = 64
N_HEADS = D_MODEL // HEAD_DIM
POOL_WINDOWS = (2, 4, 8, 16)
POOL_GROUP_DIM = D_MODEL // len(POOL_WINDOWS)
POOL_BUF = max(POOL_WINDOWS) - 1
POOL_HALO = 16
DIL_GROUPS = ((128, 1), (512, 4), (2048, 16))
N_DIL = len(DIL_GROUPS)
BAND = 128
PAST_LEN = 8192
RMS_EPS = 1e-6
NEG_INF = -1e30
ATTN_SCALE = HEAD_DIM ** -0.5
LANES = 128
N_SLABS = D_MODEL // LANES
TOKEN_TILE = 512
R16 = 16
ROWS16 = TOKEN_TILE // R16
VMEM_LIMIT_BYTES = 56 * 1024 * 1024

F32 = jnp.float32
BF16 = jnp.bfloat16


def _params(n_axes):
    return pltpu.CompilerParams(dimension_semantics=("arbitrary",) * n_axes,
                                vmem_limit_bytes=VMEM_LIMIT_BYTES)


def _resident(shape):
    zeros = (0,) * len(shape)
    return pl.BlockSpec(shape, lambda *_: zeros, pipeline_mode=pl.Buffered(1))


def _rms(x, g):
    return x * lax.rsqrt(jnp.mean(x * x, axis=-1, keepdims=True) + RMS_EPS) * g


def _dot(a, b):
    return jnp.dot(a, b, preferred_element_type=F32)


def _dot_nt(a, b):
    return lax.dot_general(a, b, (((1,), (1,)), ((), ())), preferred_element_type=F32)


def _token_tile(t):
    for tm in (TOKEN_TILE, 256, 128, 64, 32, 16, 8):
        if t % tm == 0:
            return tm
    raise ValueError(f"token count {t} is not a multiple of 8")


SLAB_SCRATCH = pltpu.VMEM((N_SLABS, TOKEN_TILE, LANES), F32)


def _slabs_put(scr, val):
    for j in range(N_SLABS):
        scr[j] = val[:, j * LANES:(j + 1) * LANES]


def _slabs_get(scr):
    return jnp.concatenate([scr[j] for j in range(N_SLABS)], axis=-1)


def _slabs_gather(scr, start, n, stride):
    return jnp.concatenate([scr[j, pl.ds(start, n, stride=stride), :] for j in range(N_SLABS)], axis=-1)


def _slabs_scatter(scr, start, n, stride, val):
    for j in range(N_SLABS):
        scr[j, pl.ds(start, n, stride=stride), :] = val[:, j * LANES:(j + 1) * LANES]


def _natural_to_by16(scr, val):
    _slabs_put(scr, val)
    return [_slabs_gather(scr, r, ROWS16, R16) for r in range(R16)]


def _by16_to_natural(scr, val):
    for r in range(R16):
        _slabs_scatter(scr, r, ROWS16, R16, val[r * ROWS16:(r + 1) * ROWS16])
    return _slabs_get(scr)


def _by4_to_by16(scr, val):
    _slabs_put(scr, val)
    per_r4 = TOKEN_TILE // 4
    parts = [None] * R16
    for r4 in range(4):
        for c in range(4):
            parts[4 * c + r4] = _slabs_gather(scr, r4 * per_r4 + c, ROWS16, 4)
    return jnp.concatenate(parts, axis=0)


def _by16_to_by4(scr, val):
    per_r4 = TOKEN_TILE // 4
    for r4 in range(4):
        for c in range(4):
            r16 = 4 * c + r4
            _slabs_scatter(scr, r4 * per_r4 + c, ROWS16, 4, val[r16 * ROWS16:(r16 + 1) * ROWS16])
    return _slabs_get(scr)


def _tile16_spec(b_tiles, c):
    return pl.BlockSpec((1, R16, ROWS16, c), lambda i: (i // b_tiles, 0, i % b_tiles, 0))


def _tile4_spec(b_tiles, c):
    return pl.BlockSpec((1, 4, TOKEN_TILE // 4, c), lambda i: (i // b_tiles, 0, i % b_tiles, 0))


def _ffn_body(x_ref, g_ref, wg_ref, wu_ref, wd_ref, o_ref, *scratch, order_in, order_out):
    d = D_MODEL
    x = x_ref[...].reshape(-1, d)
    xn = _rms(x, g_ref[0:1, :]).astype(BF16)
    hg = _dot(xn, wg_ref[...])
    hu = _dot(xn, wu_ref[...])
    a = (hg / (1.0 + jnp.exp(-hg)) * hu).astype(BF16)
    f = _dot(a, wd_ref[...])
    y = x + 0.5 * _rms(f, g_ref[1:2, :])
    if order_in == order_out:
        o_ref[...] = y.reshape(o_ref.shape)
    elif order_out == "by16":
        for r, rows in enumerate(_natural_to_by16(scratch[0], y)):
            o_ref[0, r] = rows
    else:
        o_ref[...] = _by16_to_natural(scratch[0], y)


def _ffn(x, g2, wg, wu, wd, order_in="natural", order_out="natural", seq=None):
    d = D_MODEL
    regroup = order_in != order_out
    if regroup:
        b_tiles = seq // TOKEN_TILE
        t = x.size // d
        tm = TOKEN_TILE
        nat = pl.BlockSpec((tm, d), lambda i: (i, 0))
        p16 = _tile16_spec(b_tiles, d)
        in_spec, out_spec = (nat, p16) if order_out == "by16" else (p16, nat)
        out_shape = (t // seq, R16, seq // R16, d) if order_out == "by16" else (t, d)
        scratch = [SLAB_SCRATCH]
    else:
        t = x.shape[0]
        tm = _token_tile(t)
        in_spec = out_spec = pl.BlockSpec((tm, d), lambda i: (i, 0))
        out_shape = (t, d)
        scratch = []
    return pl.pallas_call(
        functools.partial(_ffn_body, order_in=order_in, order_out=order_out),
        grid=(t // tm,),
        in_specs=[in_spec, _resident(g2.shape), _resident(wg.shape), _resident(wu.shape),
                  _resident(wd.shape)],
        out_specs=out_spec,
        out_shape=jax.ShapeDtypeStruct(out_shape, F32),
        scratch_shapes=scratch,
        compiler_params=_params(1),
        name="ffn",
    )(x, g2, wg, wu, wd)


def _proj_body(x_ref, g_ref, *refs, out_cfg):
    n_w = max(wi for wi, _, _ in out_cfg) + 1
    w_refs, o_refs = refs[:n_w], refs[n_w:]
    xn = _rms(x_ref[...], g_ref[...]).astype(BF16)
    ys = [_dot(xn, w[...]) for w in w_refs]
    for o_ref, (wi, _, scale) in zip(o_refs, out_cfg):
        y = ys[wi] if scale == 1.0 else ys[wi] * scale
        o_ref[...] = y.astype(o_ref.dtype)


def _norm_proj(x, g, ws, out_cfg):
    t, d = x.shape
    tm = _token_tile(t)
    tile = pl.BlockSpec((tm, d), lambda i: (i, 0))
    return pl.pallas_call(
        functools.partial(_proj_body, out_cfg=out_cfg),
        grid=(t // tm,),
        in_specs=[tile, _resident(g.shape)] + [_resident(w.shape) for w in ws],
        out_specs=[pl.BlockSpec((tm, ws[wi].shape[1]), lambda i: (i, 0)) for wi, _, _ in out_cfg],
        out_shape=[jax.ShapeDtypeStruct((t, ws[wi].shape[1]), dt) for wi, dt, _ in out_cfg],
        compiler_params=_params(1),
        name="norm_proj",
    )(x, g, *ws)


def _kv_streams_body(x_ref, g_ref, wk_ref, wv_ref, k32_ref, v32_ref, kn_ref, vn_ref, k4_ref, v4_ref,
                     k16_ref, v16_ref, scr):
    xn = _rms(x_ref[...], g_ref[...]).astype(BF16)
    per_r4 = TOKEN_TILE // 4
    for w_ref, o32, on, o4, o16 in ((wk_ref, k32_ref, kn_ref, k4_ref, k16_ref),
                                    (wv_ref, v32_ref, vn_ref, v4_ref, v16_ref)):
        y = _dot(xn, w_ref[...])
        o32[...] = y
        on[...] = y.astype(BF16)
        _slabs_put(scr, y)
        for r4 in range(4):
            o4[0, r4] = _slabs_gather(scr, r4, per_r4, 4).astype(BF16)
        for r in range(R16):
            o16[0, r] = _slabs_gather(scr, r, ROWS16, R16).astype(BF16)


def _kv_streams(x, g, wk, wv, b, s):
    t, d = x.shape
    b_tiles = s // TOKEN_TILE
    nat = pl.BlockSpec((TOKEN_TILE, d), lambda i: (i, 0))
    f32n = jax.ShapeDtypeStruct((t, d), F32)
    bfn = jax.ShapeDtypeStruct((t, d), BF16)
    bf4 = jax.ShapeDtypeStruct((b, 4, s // 4, d), BF16)
    bf16_ = jax.ShapeDtypeStruct((b, R16, s // R16, d), BF16)
    return pl.pallas_call(
        _kv_streams_body,
        grid=(t // TOKEN_TILE,),
        in_specs=[nat, _resident(g.shape), _resident(wk.shape), _resident(wv.shape)],
        out_specs=[nat, nat, nat, nat, _tile4_spec(b_tiles, d), _tile4_spec(b_tiles, d),
                   _tile16_spec(b_tiles, d), _tile16_spec(b_tiles, d)],
        out_shape=[f32n, f32n, bfn, bfn, bf4, bf4, bf16_, bf16_],
        scratch_shapes=[SLAB_SCRATCH],
        compiler_params=_params(1),
        name="kv_streams",
    )(x, g, wk, wv)


def _q_streams_body(x_ref, g_ref, w_ref, q0_ref, q1_ref, q2_ref, scr):
    d = D_MODEL
    xn = _rms(x_ref[0].reshape(TOKEN_TILE, d), g_ref[...]).astype(BF16)
    y0 = _dot(xn, w_ref[:, 0:d]) * ATTN_SCALE
    q0_ref[0] = _by16_to_natural(scr, y0).astype(BF16)
    y1 = _dot(xn, w_ref[:, d:2 * d]) * ATTN_SCALE
    q1_ref[0] = _by16_to_by4(scr, y1).astype(BF16).reshape(4, TOKEN_TILE // 4, d)
    y2 = _dot(xn, w_ref[:, 2 * d:3 * d]) * ATTN_SCALE
    q2_ref[0] = y2.astype(BF16).reshape(R16, ROWS16, d)


def _q_streams(h16, g, wq):
    b, _, n16, d = h16.shape
    s = n16 * R16
    b_tiles = s // TOKEN_TILE
    return pl.pallas_call(
        _q_streams_body,
        grid=(b * b_tiles,),
        in_specs=[_tile16_spec(b_tiles, d), _resident(g.shape), _resident(wq.shape)],
        out_specs=[pl.BlockSpec((1, TOKEN_TILE, d), lambda i: (i // b_tiles, i % b_tiles, 0)),
                   _tile4_spec(b_tiles, d), _tile16_spec(b_tiles, d)],
        out_shape=[jax.ShapeDtypeStruct((b, s, d), BF16), jax.ShapeDtypeStruct((b, 4, s // 4, d), BF16),
                   jax.ShapeDtypeStruct((b, R16, n16, d), BF16)],
        scratch_shapes=[SLAB_SCRATCH],
        compiler_params=_params(1),
        name="q_streams",
    )(h16, g, wq)


def _oproj_body(o_ref, h_ref, g_ref, w_ref, out_ref):
    y = _dot(o_ref[...], w_ref[...])
    out_ref[...] = h_ref[...] + _rms(y, g_ref[...])


def _out_proj(o, h, g, w):
    t, d = h.shape
    tm = _token_tile(t)
    tile = pl.BlockSpec((tm, d), lambda i: (i, 0))
    return pl.pallas_call(
        _oproj_body,
        grid=(t // tm,),
        in_specs=[tile, tile, _resident(g.shape), _resident(w.shape)],
        out_specs=tile,
        out_shape=jax.ShapeDtypeStruct((t, d), F32),
        compiler_params=_params(1),
        name="out_proj",
    )(o, h, g, w)


def _merge_oproj_body(o0_ref, l0_ref, o1_ref, l1_ref, o2_ref, l2_ref, h_ref, g_ref, w_ref, out_ref, scr):
    d = D_MODEL
    o0 = jnp.concatenate(_natural_to_by16(scr, o0_ref[0]), axis=0)
    l0 = jnp.concatenate(_natural_to_by16(scr, l0_ref[0]), axis=0)
    o1 = _by4_to_by16(scr, o1_ref[0].reshape(TOKEN_TILE, d))
    l1 = _by4_to_by16(scr, l1_ref[0].reshape(TOKEN_TILE, d))
    o2 = o2_ref[0].reshape(TOKEN_TILE, d)
    l2 = l2_ref[0].reshape(TOKEN_TILE, d)
    mm = jnp.maximum(jnp.maximum(l0, l1), l2)
    w0 = jnp.exp(l0 - mm)
    w1 = jnp.exp(l1 - mm)
    w2 = jnp.exp(l2 - mm)
    o = (w0 * o0 + w1 * o1 + w2 * o2) / (w0 + w1 + w2)
    y = _dot(o.astype(BF16), w_ref[...])
    out_ref[0] = (h_ref[0].reshape(TOKEN_TILE, d) + _rms(y, g_ref[...])).reshape(R16, ROWS16, d)


def _merge_out_proj(groups, h16, g, w):
    b, _, n16, d = h16.shape
    s = n16 * R16
    b_tiles = s // TOKEN_TILE
    nat = pl.BlockSpec((1, TOKEN_TILE, d), lambda i: (i // b_tiles, i % b_tiles, 0))
    p4 = _tile4_spec(b_tiles, d)
    p16 = _tile16_spec(b_tiles, d)
    (o0, l0), (o1, l1), (o2, l2) = groups
    return pl.pallas_call(
        _merge_oproj_body,
        grid=(b * b_tiles,),
        in_specs=[nat, nat, p4, p4, p16, p16, p16, _resident(g.shape), _resident(w.shape)],
        out_specs=p16,
        out_shape=jax.ShapeDtypeStruct(h16.shape, F32),
        scratch_shapes=[SLAB_SCRATCH],
        compiler_params=_params(1),
        name="merge_out_proj",
    )(o0, l0, o1, l1, o2, l2, h16, g, w)


def _pool_body(h_ref, prev_ref, g_ref, win_ref, wgrp_ref, scale_ref, wout_ref,
               out_ref, rows_ref, ext_ref, *, pos0, bb, ts):
    d = D_MODEL
    j = pl.program_id(1)

    @pl.when(j == 0)
    def _():
        ext_ref[:, 0:POOL_HALO, :] = prev_ref[...]

    @pl.when(j > 0)
    def _():
        ext_ref[:, 0:POOL_HALO, :] = ext_ref[:, ts:ts + POOL_HALO, :]

    h = h_ref[...].reshape(bb * ts, d)
    hn = _rms(h, g_ref[0:1, :]).astype(BF16)
    u = _dot(hn, win_ref[...])
    ext_ref[:, POOL_HALO:POOL_HALO + ts, :] = u.reshape(bb, ts, d)
    rows_ref[...] = ext_ref[:, ts:ts + POOL_HALO, :]

    pos = pos0 + j * ts + lax.broadcasted_iota(jnp.int32, (1, ts, 1), 1)
    zs = []
    for gi, win in enumerate(POOL_WINDOWS):
        cs = slice(gi * POOL_GROUP_DIM, (gi + 1) * POOL_GROUP_DIM)
        cur = ext_ref[:, POOL_HALO:POOL_HALO + ts, cs]
        tot = cur
        for k in range(1, win):
            tot = tot + ext_ref[:, POOL_HALO - k:POOL_HALO - k + ts, cs]
        cnt = jnp.minimum(pos + 1, win).astype(F32)
        z = (tot / cnt - cur).reshape(bb * ts, POOL_GROUP_DIM).astype(BF16)
        zs.append(_dot(z, wgrp_ref[gi]))
    zc = (jnp.concatenate(zs, axis=-1) * scale_ref[...]).astype(BF16)
    y = _dot(zc, wout_ref[...])
    out_ref[...] = (h + _rms(y, g_ref[1:2, :])).reshape(bb, ts, d)


def _pool_mixer(h, prev, g2, w_in, w_grp, scale, w_out, pos0):
    b, s, d = h.shape
    if s >= TOKEN_TILE:
        bb, ts = 1, TOKEN_TILE
    else:
        bb, ts = b, s
    n_t = s // ts
    assert s % ts == 0 and ts % 8 == 0 and (n_t == 1 or ts >= POOL_HALO)
    tile = pl.BlockSpec((bb, ts, d), lambda i, j: (i, j, 0))
    halo = pl.BlockSpec((bb, POOL_HALO, d), lambda i, j: (i, 0, 0))
    return pl.pallas_call(
        functools.partial(_pool_body, pos0=pos0, bb=bb, ts=ts),
        grid=(b // bb, n_t),
        in_specs=[tile, halo, _resident(g2.shape), _resident(w_in.shape), _resident(w_grp.shape),
                  _resident(scale.shape), _resident(w_out.shape)],
        out_specs=[tile, halo],
        out_shape=[jax.ShapeDtypeStruct((b, s, d), F32),
                   jax.ShapeDtypeStruct((b, POOL_HALO, d), F32)],
        scratch_shapes=[pltpu.VMEM((bb, POOL_HALO + ts, d), F32)],
        compiler_params=_params(2),
        name="pool_mixer",
    )(h, prev, g2, w_in, w_grp, scale, w_out)


def _alibi_slopes(n):
    return 2.0 ** (-8.0 * jnp.arange(1, n + 1, dtype=F32) / n)


def _band_table(dil, n_steps):
    qi = jnp.arange(BAND)[:, None]
    ki = jnp.arange(2 * BAND)[None, :]
    step = qi + BAND - ki
    valid = (step >= 0) & (step <= n_steps)
    bias = -_alibi_slopes(N_HEADS)[:, None, None] * (step * dil).astype(F32)[None]
    return jnp.where(valid[None], bias, NEG_INF)


def _pair_tables(dil, n_steps):
    tab = _band_table(dil, n_steps)
    both = tab.reshape(N_HEADS // 2, 2, BAND, 2 * BAND)
    with_prev = jnp.concatenate([both[:, 0], both[:, 1]], axis=-1)
    cur_only = jnp.concatenate([both[:, 0, :, BAND:], both[:, 1, :, BAND:]], axis=-1)
    return with_prev, cur_only


def _step_table(n_past, t_new, n_keys):
    dist = n_past + jnp.arange(t_new)[:, None] - jnp.arange(n_keys)[None, :]
    in_buf = (jnp.arange(n_keys) < n_past + t_new)[None, :]
    bias = -_alibi_slopes(N_HEADS)[:, None, None] * dist.astype(F32)[None]
    tabs = []
    for win, dil in DIL_GROUPS:
        valid = (dist >= 0) & (dist % dil == 0) & (dist <= win) & in_buf
        tabs.append(jnp.where(valid[None], bias, NEG_INF))
    return jnp.stack(tabs, axis=1).reshape(N_HEADS * N_DIL * t_new, n_keys)


def _band_attn_body(*refs, nb, lead):
    refs = list(refs)
    q_ref, kc_ref, vc_ref = [r.at[(0,) * lead] for r in refs[:3]]
    del refs[:3]
    if nb > 1:
        kp_ref, vp_ref = [r.at[(0,) * lead] for r in refs[:2]]
        del refs[:2]
        tab2_ref = refs.pop(0)
    tab1_ref = refs.pop(0)
    o_ref, l_ref = [r.at[(0,) * lead] for r in refs[:2]]
    s_ref, p_ref = refs[2:]
    n_pairs = N_HEADS // 2

    def run(with_prev):
        nk = 2 * BAND if with_prev else BAND
        low = lax.broadcasted_iota(jnp.int32, (nk, LANES), 1) < HEAD_DIM
        row_low = lax.broadcasted_iota(jnp.int32, (2 * nk, LANES), 0) < nk
        lane_low = lax.broadcasted_iota(jnp.int32, (2 * nk, LANES), 1) < HEAD_DIM
        ones_st = jnp.where(row_low == lane_low, 1.0, 0.0).astype(BF16)
        q_low = lax.broadcasted_iota(jnp.int32, (BAND, LANES), 1) < HEAD_DIM
        tab_ref = tab2_ref if with_prev else tab1_ref

        def stacked(c_ref, p_ref_, cs):
            x = c_ref[:, cs]
            if with_prev:
                x = jnp.concatenate([p_ref_[:, cs], x], axis=0)
            z = jnp.zeros_like(x)
            return jnp.concatenate([jnp.where(low, x, z), jnp.where(low, z, x)], axis=0)

        for j in range(n_pairs):
            cs = slice(j * LANES, (j + 1) * LANES)
            kst = stacked(kc_ref, kp_ref if with_prev else None, cs)
            s_ref[:, 2 * nk * j:2 * nk * (j + 1)] = _dot_nt(q_ref[:, cs], kst) + tab_ref[j]
        ms = []
        for h in range(N_HEADS):
            sh = s_ref[:, nk * h:nk * (h + 1)]
            m = jnp.max(sh, axis=-1, keepdims=True)
            p_ref[:, nk * h:nk * (h + 1)] = jnp.exp(sh - m).astype(BF16)
            ms.append(m)
        for j in range(n_pairs):
            cs = slice(j * LANES, (j + 1) * LANES)
            vst = stacked(vc_ref, vp_ref if with_prev else None, cs)
            r = _dot(p_ref[:, 2 * nk * j:2 * nk * (j + 1)], jnp.concatenate([vst, ones_st], axis=1))
            l = r[:, LANES:]
            o_ref[:, cs] = r[:, :LANES] / l
            m2 = jnp.where(q_low, jnp.broadcast_to(ms[2 * j], (BAND, LANES)),
                           jnp.broadcast_to(ms[2 * j + 1], (BAND, LANES)))
            l_ref[:, cs] = m2 + jnp.log(l)

    if nb == 1:
        run(False)
    else:
        n = pl.program_id(2)
        pl.when(n == 0)(lambda: run(False))
        pl.when(n > 0)(lambda: run(True))


def _band_attn_group(q, kb, vb, gi):
    win, dil = DIL_GROUPS[gi]
    d = D_MODEL
    if dil == 1:
        b, n, _ = q.shape
        lead = 1
        blk = (1, BAND, d)
        cur = pl.BlockSpec(blk, lambda i, r, j: (i, j, 0))
        prv = pl.BlockSpec(blk, lambda i, r, j: (i, jnp.maximum(j - 1, 0), 0))
    else:
        b, _, n, _ = q.shape
        lead = 2
        blk = (1, 1, BAND, d)
        cur = pl.BlockSpec(blk, lambda i, r, j: (i, r, j, 0))
        prv = pl.BlockSpec(blk, lambda i, r, j: (i, r, jnp.maximum(j - 1, 0), 0))
    nb = n // BAND
    assert n % BAND == 0
    tab2, tab1 = _pair_tables(dil, win // dil)
    in_specs = [cur, cur, cur]
    args = [q, kb, vb]
    if nb > 1:
        in_specs += [prv, prv, _resident(tab2.shape)]
        args += [kb, vb, tab2]
    in_specs.append(_resident(tab1.shape))
    args.append(tab1)
    out = jax.ShapeDtypeStruct(q.shape, F32)
    return pl.pallas_call(
        functools.partial(_band_attn_body, nb=nb, lead=lead),
        grid=(b, dil, nb),
        in_specs=in_specs,
        out_specs=[cur, cur],
        out_shape=[out, out],
        scratch_shapes=[pltpu.VMEM((BAND, N_HEADS * 2 * BAND), F32), pltpu.VMEM((BAND, N_HEADS * 2 * BAND), BF16)],
        compiler_params=_params(3),
        name=f"band_attn_g{gi}",
    )(*args)


STEP_HEADS = 4
STEP_COLS = STEP_HEADS * HEAD_DIM


def _step_attn_body(*refs, n_past, t_new, n_keys, write_cache):
    q_refs = refs[:N_DIL]
    ck_ref, cv_ref, kn_ref, vn_ref, tab_ref = refs[N_DIL:N_DIL + 5]
    refs = refs[N_DIL + 5:]
    o_ref = refs[0]
    if write_cache:
        ok_ref, ov_ref = refs[1:3]
        refs = refs[3:]
    else:
        refs = refs[1:]
    kb_ref, vb_ref = refs
    rows_g = N_DIL * t_new
    rows = STEP_HEADS * rows_g

    for c_ref, n_ref, b_ref in ((ck_ref, kn_ref, kb_ref), (cv_ref, vn_ref, vb_ref)):
        b_ref[0:n_past, :] = c_ref[0].astype(BF16)
        b_ref[n_past:n_past + t_new, :] = n_ref[0].astype(BF16)
        b_ref[n_past + t_new:n_keys, :] = jnp.zeros((n_keys - n_past - t_new, STEP_COLS), BF16)
    if write_cache:
        for c_ref, n_ref, out_ref in ((ck_ref, kn_ref, ok_ref), (cv_ref, vn_ref, ov_ref)):
            out_ref[0, 0:n_past - t_new, :] = c_ref[0, t_new:n_past, :]
            out_ref[0, n_past - t_new:n_past, :] = n_ref[0]

    q_rows = jnp.concatenate([r[0] for r in q_refs], axis=0)
    q_all = jnp.concatenate([q_rows] * STEP_HEADS, axis=0)
    row_head = lax.broadcasted_iota(jnp.int32, (rows, STEP_COLS), 0) // rows_g
    col_head = lax.broadcasted_iota(jnp.int32, (rows, STEP_COLS), 1) // HEAD_DIM
    own = row_head == col_head
    q_bd = jnp.where(own, q_all, jnp.zeros_like(q_all))
    s = _dot_nt(q_bd, kb_ref[...]) + tab_ref[...]
    m = jnp.max(s, axis=-1, keepdims=True)
    p = jnp.exp(s - m)
    l = jnp.sum(p, axis=-1, keepdims=True)
    o_full = _dot(p.astype(BF16), vb_ref[...]) / l
    lse = m + jnp.log(l)

    out = jnp.zeros((t_new, STEP_COLS), F32)
    col = lax.broadcasted_iota(jnp.int32, (t_new, STEP_COLS), 1) // HEAD_DIM
    for hh in range(STEP_HEADS):
        base = hh * rows_g
        lses = [lse[base + g * t_new:base + (g + 1) * t_new] for g in range(N_DIL)]
        mm = functools.reduce(jnp.maximum, lses)
        ws = [jnp.exp(x - mm) for x in lses]
        den = functools.reduce(lambda a, b: a + b, ws)
        acc = functools.reduce(
            lambda a, b: a + b,
            [ws[g] * o_full[base + g * t_new:base + (g + 1) * t_new] for g in range(N_DIL)])
        out = jnp.where(col == hh, acc / den, out)
    o_ref[0] = out.astype(o_ref.dtype)


def _step_attention(q, cache_k, cache_v, k_new, v_new, write_cache):
    b, t_new, _ = q.shape
    n_past = cache_k.shape[1]
    d = D_MODEL
    assert t_new % 8 == 0 and n_past % 8 == 0
    n_keys = -(-(n_past + t_new) // LANES) * LANES
    rows = STEP_HEADS * N_DIL * t_new
    tab = _step_table(n_past, t_new, n_keys)
    n_hg = N_HEADS // STEP_HEADS

    def q_spec(g):
        return pl.BlockSpec((1, t_new, STEP_COLS), lambda i, j: (i, 0, g * n_hg + j))

    cache_spec = pl.BlockSpec((1, n_past, STEP_COLS), lambda i, j: (i, 0, j))
    new_spec = pl.BlockSpec((1, t_new, STEP_COLS), lambda i, j: (i, 0, j))
    out_specs = [new_spec]
    out_shape = [jax.ShapeDtypeStruct((b, t_new, d), BF16)]
    if write_cache:
        out_specs += [cache_spec, cache_spec]
        out_shape += [jax.ShapeDtypeStruct((b, n_past, d), F32)] * 2
    return pl.pallas_call(
        functools.partial(_step_attn_body, n_past=n_past, t_new=t_new, n_keys=n_keys,
                          write_cache=write_cache),
        grid=(b, n_hg),
        in_specs=[q_spec(g) for g in range(N_DIL)]
        + [cache_spec, cache_spec, new_spec, new_spec,
           pl.BlockSpec((rows, n_keys), lambda i, j: (j, 0))],
        out_specs=out_specs,
        out_shape=out_shape,
        scratch_shapes=[pltpu.VMEM((n_keys, STEP_COLS), BF16)] * 2,
        compiler_params=_params(2),
        name="step_attn",
    )(q, q, q, cache_k, cache_v, k_new, v_new, tab)


def _ffn_w(w, layer, half):
    return w["ffn_w_gate"][layer, half], w["ffn_w_up"][layer, half], w["ffn_w_down"][layer, half]


def _pool_layers(h, b, s, pool_prev, pos0, w, n_a):
    d = D_MODEL
    pool_rows = []
    for layer in range(n_a):
        g = w["norm_g"][layer]
        h = _ffn(h, g[0:2], *_ffn_w(w, layer, 0))
        h3, rows = _pool_mixer(h.reshape(b, s, d), pool_prev[layer], g[2:4], w["pool_w_in"][layer],
                               w["pool_w_grp"][layer], w["pool_scale"][layer], w["pool_w_out"][layer], pos0)
        h = h3.reshape(b * s, d)
        pool_rows.append(rows[:, POOL_HALO - POOL_BUF:])
        h = _ffn(h, g[4:6], *_ffn_w(w, layer, 1))
    return h, jnp.stack(pool_rows)


def _prompt_trunk(x, w):
    b, s, d = x.shape
    t = b * s
    depth = w["norm_g"].shape[0]
    n_a = depth // 2
    assert s % (R16 * BAND) == 0 and s % TOKEN_TILE == 0
    empty = jnp.zeros((n_a, b, POOL_HALO, d), F32)
    h, pool_rows = _pool_layers(x.reshape(t, d), b, s, empty, 0, w, n_a)
    k32, v32, kn, vn, k4, v4, k16, v16 = _kv_streams(h, w["kv_norm"], w["w_k"], w["w_v"], b, s)
    kn, vn = kn.reshape(b, s, d), vn.reshape(b, s, d)
    for layer in range(n_a, depth):
        g = w["norm_g"][layer]
        j = layer - n_a
        if layer == n_a:
            h = _ffn(h, g[0:2], *_ffn_w(w, layer, 0), order_in="natural", order_out="by16", seq=s)
        else:
            h = _ffn(h, g[0:2], *_ffn_w(w, layer, 0))
            h = h.reshape(b, R16, s // R16, d)
        q0, q1, q2 = _q_streams(h, g[2:3], w["attn_w_q"][j])
        groups = [_band_attn_group(q0, kn, vn, 0), _band_attn_group(q1, k4, v4, 1),
                  _band_attn_group(q2, k16, v16, 2)]
        h = _merge_out_proj(groups, h, g[3:4], w["attn_w_o"][j])
        if layer == depth - 1:
            h = _ffn(h, g[4:6], *_ffn_w(w, layer, 1), order_in="by16", order_out="natural", seq=s)
        else:
            h = _ffn(h.reshape(t, d), g[4:6], *_ffn_w(w, layer, 1))
    buf = min(max(win for win, _ in DIL_GROUPS), s)
    k32, v32 = k32.reshape(b, s, d)[:, s - buf:], v32.reshape(b, s, d)[:, s - buf:]
    return h.reshape(b, s, d), pool_rows, k32, v32


def _sample_trunk(x, pool_prev, cache_k, cache_v, pos0, w):
    b, s, d = x.shape
    t = b * s
    depth = w["norm_g"].shape[0]
    n_a = depth // 2
    h, pool_rows = _pool_layers(x.reshape(t, d), b, s, pool_prev, pos0, w, n_a)
    k_new, v_new = _norm_proj(h, w["kv_norm"], [w["w_k"], w["w_v"]], out_cfg=((0, F32, 1.0), (1, F32, 1.0)))
    k_new, v_new = k_new.reshape(b, s, d), v_new.reshape(b, s, d)
    for layer in range(n_a, depth):
        g = w["norm_g"][layer]
        j = layer - n_a
        h = _ffn(h, g[0:2], *_ffn_w(w, layer, 0))
        (q,) = _norm_proj(h, g[2:3], [w["attn_w_q"][j]], out_cfg=((0, BF16, ATTN_SCALE),))
        res = _step_attention(q.reshape(b, s, N_DIL * d), cache_k, cache_v, k_new, v_new, write_cache=(j == 0))
        if j == 0:
            new_k, new_v = res[1], res[2]
        h = _out_proj(res[0].reshape(t, d), h, g[3:4], w["attn_w_o"][j])
        h = _ffn(h, g[4:6], *_ffn_w(w, layer, 1))
    return h.reshape(b, s, d), pool_rows, new_k, new_v


def kernel(x_prompt, x_sample, state_pool, cache_k, cache_v, norm_g, ffn_w_gate, ffn_w_up, ffn_w_down,
           pool_w_in, pool_w_grp, pool_scale, pool_w_out, kv_norm, w_k, w_v, attn_w_q, attn_w_o):
    d = D_MODEL
    w = dict(
        norm_g=norm_g,
        ffn_w_gate=ffn_w_gate.astype(BF16), ffn_w_up=ffn_w_up.astype(BF16), ffn_w_down=ffn_w_down.astype(BF16),
        pool_w_in=pool_w_in.astype(BF16), pool_w_grp=pool_w_grp.astype(BF16),
        pool_scale=pool_scale.reshape(-1, 1, d), pool_w_out=pool_w_out.astype(BF16),
        kv_norm=kv_norm.reshape(1, d), w_k=w_k.astype(BF16), w_v=w_v.astype(BF16),
        attn_w_q=attn_w_q.astype(BF16), attn_w_o=attn_w_o.astype(BF16),
    )
    bs, n_past = cache_k.shape[:2]
    y_p, pool_p, k_p, v_p = _prompt_trunk(x_prompt, w)

    prev_s = jnp.pad(state_pool, ((0, 0), (0, 0), (POOL_HALO - POOL_BUF, 0), (0, 0)))
    y_s, pool_s, k_s, v_s = _sample_trunk(x_sample, prev_s, cache_k.reshape(bs, n_past, d),
                                          cache_v.reshape(bs, n_past, d), PAST_LEN, w)
    hd = (N_HEADS, HEAD_DIM)
    return (y_p, y_s, pool_p,
            k_p.reshape(k_p.shape[:2] + hd), v_p.reshape(v_p.shape[:2] + hd),
            pool_s,
            k_s.reshape(k_s.shape[:2] + hd), v_s.reshape(v_s.shape[:2] + hd))
```

```python
import functools
from typing import NamedTuple

import jax
import jax.numpy as jnp
from jax import lax
from jax.experimental import pallas as pl
from jax.experimental.pallas import tpu as pltpu

D_MODEL = 1024
D_FF = 2816
HEAD_DIM = 64
N_HEADS = D_MODEL // HEAD_DIM
POOL_WINDOWS = (2, 4, 8, 16)
POOL_GROUP_DIM = D_MODEL // len(POOL_WINDOWS)
POOL_BUF = max(POOL_WINDOWS) - 1
POOL_HALO = 16
POOL_TOP = 32
DIL_GROUPS = ((128, 1), (512, 4), (2048, 16))
N_DIL = len(DIL_GROUPS)
BAND = 128
PAST_LEN = 8192
RMS_EPS = 1e-6
NEG_INF = -1e30
ATTN_SCALE = HEAD_DIM ** -0.5
LANES = 128
N_SLABS = D_MODEL // LANES
TOKEN_TILE = 512
FFN_SPLIT_ROWS = 256
R16 = 16
ROWS16 = TOKEN_TILE // R16
VMEM_LIMIT_BYTES = 56 * 1024 * 1024

F32 = jnp.float32
BF16 = jnp.bfloat16


def _params(n_axes):
    return pltpu.CompilerParams(dimension_semantics=("arbitrary",) * n_axes,
                                vmem_limit_bytes=VMEM_LIMIT_BYTES)


def _resident(shape):
    zeros = (0,) * len(shape)
    return pl.BlockSpec(shape, lambda *_: zeros, pipeline_mode=pl.Buffered(1))


class _Sel(NamedTuple):
    arr: jax.Array
    idx: tuple = ()


def _member(sel):
    lead = len(sel.idx)
    shape = sel.arr.shape[lead:]
    index = tuple(sel.idx) + (0,) * len(shape)
    return pl.BlockSpec((None,) * lead + shape, lambda *_: index, pipeline_mode=pl.Buffered(1))


def _rms(x, g):
    return x * lax.rsqrt(jnp.mean(x * x, axis=-1, keepdims=True) + RMS_EPS) * g


def _dot(a, b):
    return jnp.dot(a, b, preferred_element_type=F32)


def _dot_nt(a, b):
    return lax.dot_general(a, b, (((1,), (1,)), ((), ())), preferred_element_type=F32)


def _token_tile(t):
    for tm in (TOKEN_TILE, 256, 128, 64, 32, 16, 8):
        if t % tm == 0:
            return tm
    raise ValueError(f"token count {t} is not a multiple of 8")


SLAB_SCRATCH = pltpu.VMEM((N_SLABS, TOKEN_TILE, LANES), F32)


def _slabs_put(scr, val):
    for j in range(N_SLABS):
        scr[j] = val[:, j * LANES:(j + 1) * LANES]


def _slabs_get(scr):
    return jnp.concatenate([scr[j] for j in range(N_SLABS)], axis=-1)


def _slabs_gather(scr, start, n, stride):
    return jnp.concatenate([scr[j, pl.ds(start, n, stride=stride), :] for j in range(N_SLABS)], axis=-1)


def _slabs_scatter(scr, start, n, stride, val):
    for j in range(N_SLABS):
        scr[j, pl.ds(start, n, stride=stride), :] = val[:, j * LANES:(j + 1) * LANES]


def _natural_to_by16(scr, val):
    _slabs_put(scr, val)
    return [_slabs_gather(scr, r, ROWS16, R16) for r in range(R16)]


def _by16_to_natural(scr, val):
    for r in range(R16):
        _slabs_scatter(scr, r, ROWS16, R16, val[r * ROWS16:(r + 1) * ROWS16])
    return _slabs_get(scr)


def _by4_to_by16(scr, val):
    _slabs_put(scr, val)
    per_r4 = TOKEN_TILE // 4
    parts = [None] * R16
    for r4 in range(4):
        for c in range(4):
            parts[4 * c + r4] = _slabs_gather(scr, r4 * per_r4 + c, ROWS16, 4)
    return jnp.concatenate(parts, axis=0)


def _by16_to_by4(scr, val):
    per_r4 = TOKEN_TILE // 4
    for r4 in range(4):
        for c in range(4):
            r16 = 4 * c + r4
            _slabs_scatter(scr, r4 * per_r4 + c, ROWS16, 4, val[r16 * ROWS16:(r16 + 1) * ROWS16])
    return _slabs_get(scr)


def _tile16_spec(b_tiles, c):
    return pl.BlockSpec((1, R16, ROWS16, c), lambda i: (i // b_tiles, 0, i % b_tiles, 0))


def _tile4_spec(b_tiles, c):
    return pl.BlockSpec((1, 4, TOKEN_TILE // 4, c), lambda i: (i // b_tiles, 0, i % b_tiles, 0))


def _ffn_body(x_ref, g_ref, wg_ref, wu_ref, wd_ref, o_ref, *scratch, g_row, order_in, order_out):
    d = D_MODEL

    def half_step(x):
        xn = _rms(x, g_ref[g_row:g_row + 1, :]).astype(BF16)
        hg = _dot(xn, wg_ref[...])
        hu = _dot(xn, wu_ref[...])
        a = (hg / (1.0 + jnp.exp(-hg)) * hu).astype(BF16)
        f = _dot(a, wd_ref[...])
        return x + 0.5 * _rms(f, g_ref[g_row + 1:g_row + 2, :])

    x = x_ref[...].reshape(-1, d)
    n = x.shape[0]
    if n % FFN_SPLIT_ROWS == 0 and n > FFN_SPLIT_ROWS:
        y = jnp.concatenate([half_step(x[i:i + FFN_SPLIT_ROWS]) for i in range(0, n, FFN_SPLIT_ROWS)], axis=0)
    else:
        y = half_step(x)
    if order_in == order_out:
        o_ref[...] = y.reshape(o_ref.shape)
    elif order_out == "by16":
        for r, rows in enumerate(_natural_to_by16(scratch[0], y)):
            o_ref[0, r] = rows
    else:
        o_ref[...] = _by16_to_natural(scratch[0], y)


def _ffn(x, g, g_row, wg, wu, wd, order_in="natural", order_out="natural", seq=None):
    d = D_MODEL
    regroup = order_in != order_out
    if regroup:
        b_tiles = seq // TOKEN_TILE
        t = x.size // d
        tm = TOKEN_TILE
        nat = pl.BlockSpec((tm, d), lambda i: (i, 0))
        p16 = _tile16_spec(b_tiles, d)
        in_spec, out_spec = (nat, p16) if order_out == "by16" else (p16, nat)
        out_shape = (t // seq, R16, seq // R16, d) if order_out == "by16" else (t, d)
        scratch = [SLAB_SCRATCH]
    else:
        t = x.shape[0]
        tm = _token_tile(t)
        in_spec = out_spec = pl.BlockSpec((tm, d), lambda i: (i, 0))
        out_shape = (t, d)
        scratch = []
    return pl.pallas_call(
        functools.partial(_ffn_body, g_row=g_row, order_in=order_in, order_out=order_out),
        grid=(t // tm,),
        in_specs=[in_spec, _member(g), _member(wg), _member(wu), _member(wd)],
        out_specs=out_spec,
        out_shape=jax.ShapeDtypeStruct(out_shape, F32),
        scratch_shapes=scratch,
        compiler_params=_params(1),
        name="ffn",
    )(x, g.arr, wg.arr, wu.arr, wd.arr)


def _proj_body(x_ref, g_ref, *refs, g_row, out_cfg):
    n_w = max(wi for wi, _, _ in out_cfg) + 1
    w_refs, o_refs = refs[:n_w], refs[n_w:]
    xn = _rms(x_ref[...], g_ref[g_row:g_row + 1, :]).astype(BF16)
    ys = [_dot(xn, w[...]) for w in w_refs]
    for o_ref, (wi, _, scale) in zip(o_refs, out_cfg):
        y = ys[wi] if scale == 1.0 else ys[wi] * scale
        o_ref[...] = y.astype(o_ref.dtype)


def _norm_proj(x, g, g_row, ws, out_cfg):
    t, d = x.shape
    tm = _token_tile(t)
    tile = pl.BlockSpec((tm, d), lambda i: (i, 0))
    n_out = [w.arr.shape[-1] for w in ws]
    return pl.pallas_call(
        functools.partial(_proj_body, g_row=g_row, out_cfg=out_cfg),
        grid=(t // tm,),
        in_specs=[tile, _member(g)] + [_member(w) for w in ws],
        out_specs=[pl.BlockSpec((tm, n_out[wi]), lambda i: (i, 0)) for wi, _, _ in out_cfg],
        out_shape=[jax.ShapeDtypeStruct((t, n_out[wi]), dt) for wi, dt, _ in out_cfg],
        compiler_params=_params(1),
        name="norm_proj",
    )(x, g.arr, *[w.arr for w in ws])


def _kv_streams_body(x_ref, g_ref, wk_ref, wv_ref, k32_ref, v32_ref, kn_ref, vn_ref, k4_ref, v4_ref,
                     k16_ref, v16_ref, scr):
    xn = _rms(x_ref[...], g_ref[...]).astype(BF16)
    per_r4 = TOKEN_TILE // 4
    for w_ref, o32, on, o4, o16 in ((wk_ref, k32_ref, kn_ref, k4_ref, k16_ref),
                                    (wv_ref, v32_ref, vn_ref, v4_ref, v16_ref)):
        y = _dot(xn, w_ref[...])
        o32[...] = y
        on[...] = y.astype(BF16)
        _slabs_put(scr, y)
        for r4 in range(4):
            o4[0, r4] = _slabs_gather(scr, r4, per_r4, 4).astype(BF16)
        for r in range(R16):
            o16[0, r] = _slabs_gather(scr, r, ROWS16, R16).astype(BF16)


def _kv_streams(x, g, wk, wv, b, s):
    t, d = x.shape
    b_tiles = s // TOKEN_TILE
    nat = pl.BlockSpec((TOKEN_TILE, d), lambda i: (i, 0))
    f32n = jax.ShapeDtypeStruct((t, d), F32)
    bfn = jax.ShapeDtypeStruct((t, d), BF16)
    bf4 = jax.ShapeDtypeStruct((b, 4, s // 4, d), BF16)
    bf16_ = jax.ShapeDtypeStruct((b, R16, s // R16, d), BF16)
    return pl.pallas_call(
        _kv_streams_body,
        grid=(t // TOKEN_TILE,),
        in_specs=[nat, _resident(g.shape), _resident(wk.shape), _resident(wv.shape)],
        out_specs=[nat, nat, nat, nat, _tile4_spec(b_tiles, d), _tile4_spec(b_tiles, d),
                   _tile16_spec(b_tiles, d), _tile16_spec(b_tiles, d)],
        out_shape=[f32n, f32n, bfn, bfn, bf4, bf4, bf16_, bf16_],
        scratch_shapes=[SLAB_SCRATCH],
        compiler_params=_params(1),
        name="kv_streams",
    )(x, g, wk, wv)


def _q_streams_body(x_ref, g_ref, w_ref, q0_ref, q1_ref, q2_ref, scr, *, g_row):
    d = D_MODEL
    xn = _rms(x_ref[0].reshape(TOKEN_TILE, d), g_ref[g_row:g_row + 1, :]).astype(BF16)
    y0 = _dot(xn, w_ref[:, 0:d]) * ATTN_SCALE
    q0_ref[0] = _by16_to_natural(scr, y0).astype(BF16)
    y1 = _dot(xn, w_ref[:, d:2 * d]) * ATTN_SCALE
    q1_ref[0] = _by16_to_by4(scr, y1).astype(BF16).reshape(4, TOKEN_TILE // 4, d)
    y2 = _dot(xn, w_ref[:, 2 * d:3 * d]) * ATTN_SCALE
    q2_ref[0] = y2.astype(BF16).reshape(R16, ROWS16, d)


def _q_streams(h16, g, g_row, wq):
    b, _, n16, d = h16.shape
    s = n16 * R16
    b_tiles = s // TOKEN_TILE
    return pl.pallas_call(
        functools.partial(_q_streams_body, g_row=g_row),
        grid=(b * b_tiles,),
        in_specs=[_tile16_spec(b_tiles, d), _member(g), _member(wq)],
        out_specs=[pl.BlockSpec((1, TOKEN_TILE, d), lambda i: (i // b_tiles, i % b_tiles, 0)),
                   _tile4_spec(b_tiles, d), _tile16_spec(b_tiles, d)],
        out_shape=[jax.ShapeDtypeStruct((b, s, d), BF16), jax.ShapeDtypeStruct((b, 4, s // 4, d), BF16),
                   jax.ShapeDtypeStruct((b, R16, n16, d), BF16)],
        scratch_shapes=[SLAB_SCRATCH],
        compiler_params=_params(1),
        name="q_streams",
    )(h16, g.arr, wq.arr)


def _oproj_body(o_ref, h_ref, g_ref, w_ref, out_ref, *, g_row):
    y = _dot(o_ref[...], w_ref[...])
    out_ref[...] = h_ref[...] + _rms(y, g_ref[g_row:g_row + 1, :])


def _out_proj(o, h, g, g_row, w):
    t, d = h.shape
    tm = _token_tile(t)
    tile = pl.BlockSpec((tm, d), lambda i: (i, 0))
    return pl.pallas_call(
        functools.partial(_oproj_body, g_row=g_row),
        grid=(t // tm,),
        in_specs=[tile, tile, _member(g), _member(w)],
        out_specs=tile,
        out_shape=jax.ShapeDtypeStruct((t, d), F32),
        compiler_params=_params(1),
        name="out_proj",
    )(o, h, g.arr, w.arr)


def _merge_oproj_body(o0_ref, l0_ref, o1_ref, l1_ref, o2_ref, l2_ref, h_ref, g_ref, w_ref, out_ref, scr, *,
                      g_row):
    d = D_MODEL
    o0 = jnp.concatenate(_natural_to_by16(scr, o0_ref[0]), axis=0)
    l0 = jnp.concatenate(_natural_to_by16(scr, l0_ref[0]), axis=0)
    o1 = _by4_to_by16(scr, o1_ref[0].reshape(TOKEN_TILE, d))
    l1 = _by4_to_by16(scr, l1_ref[0].reshape(TOKEN_TILE, d))
    o2 = o2_ref[0].reshape(TOKEN_TILE, d)
    l2 = l2_ref[0].reshape(TOKEN_TILE, d)
    mm = jnp.maximum(jnp.maximum(l0, l1), l2)
    w0 = jnp.exp(l0 - mm)
    w1 = jnp.exp(l1 - mm)
    w2 = jnp.exp(l2 - mm)
    o = (w0 * o0 + w1 * o1 + w2 * o2) / (w0 + w1 + w2)
    y = _dot(o.astype(BF16), w_ref[...])
    out_ref[0] = (h_ref[0].reshape(TOKEN_TILE, d) + _rms(y, g_ref[g_row:g_row + 1, :])).reshape(R16, ROWS16, d)


def _merge_out_proj(groups, h16, g, g_row, w):
    b, _, n16, d = h16.shape
    s = n16 * R16
    b_tiles = s // TOKEN_TILE
    nat = pl.BlockSpec((1, TOKEN_TILE, d), lambda i: (i // b_tiles, i % b_tiles, 0))
    p4 = _tile4_spec(b_tiles, d)
    p16 = _tile16_spec(b_tiles, d)
    (o0, l0), (o1, l1), (o2, l2) = groups
    return pl.pallas_call(
        functools.partial(_merge_oproj_body, g_row=g_row),
        grid=(b * b_tiles,),
        in_specs=[nat, nat, p4, p4, p16, p16, p16, _member(g), _member(w)],
        out_specs=p16,
        out_shape=jax.ShapeDtypeStruct(h16.shape, F32),
        scratch_shapes=[SLAB_SCRATCH],
        compiler_params=_params(1),
        name="merge_out_proj",
    )(o0, l0, o1, l1, o2, l2, h16, g.arr, w.arr)


def _pool_body(h_ref, prev_ref, g_ref, win_ref, wgrp_ref, scale_ref, wout_ref,
               out_ref, rows_ref, ext_ref, sa_ref, sb_ref, *, g_row, pos0, bb, ts):
    d = D_MODEL
    j = pl.program_id(1)
    top, end = POOL_TOP, POOL_TOP + ts

    @pl.when(j == 0)
    def _():
        ext_ref[:, 0:top - POOL_HALO, :] = jnp.zeros((bb, top - POOL_HALO, d), F32)
        ext_ref[:, top - POOL_HALO:top, :] = prev_ref[...]

    @pl.when(j > 0)
    def _():
        ext_ref[:, 0:top, :] = ext_ref[:, ts:end, :]

    h = h_ref[...].reshape(bb * ts, d)
    hn = _rms(h, g_ref[g_row:g_row + 1, :]).astype(BF16)
    u = _dot(hn, win_ref[...])
    ext_ref[:, top:end, :] = u.reshape(bb, ts, d)
    rows_ref[...] = ext_ref[:, end - POOL_HALO:end, :]

    def doubled(ref, cs, lo, shift):
        return ref[:, lo:end, cs] + ref[:, lo - shift:end - shift, cs]

    every = slice(None)
    pos = pos0 + j * ts + lax.broadcasted_iota(jnp.int32, (1, ts, 1), 1)
    zs = []
    for gi, win in enumerate(POOL_WINDOWS):
        cs = slice(gi * POOL_GROUP_DIM, (gi + 1) * POOL_GROUP_DIM)
        n_stage = win.bit_length() - 1
        src, src_cs = ext_ref, cs
        stage_refs = (sa_ref, sb_ref)
        for k in range(n_stage - 1):
            lo = top - 8 * (n_stage - 1 - k)
            dst = stage_refs[k % 2]
            dst[:, lo:end, :] = doubled(src, src_cs, lo, 1 << k)
            src, src_cs = dst, every
        tot = doubled(src, src_cs, top, win // 2)
        cur = ext_ref[:, top:end, cs]
        cnt = jnp.minimum(pos + 1, win).astype(F32)
        z = (tot / cnt - cur).reshape(bb * ts, POOL_GROUP_DIM).astype(BF16)
        zs.append(_dot(z, wgrp_ref[gi]))
    zc = (jnp.concatenate(zs, axis=-1) * scale_ref[...]).astype(BF16)
    y = _dot(zc, wout_ref[...])
    out_ref[...] = (h + _rms(y, g_ref[g_row + 1:g_row + 2, :])).reshape(bb, ts, d)


def _pool_mixer(h, prev, g, g_row, w_in, w_grp, scale, w_out, pos0):
    b, s, d = h.shape
    if s >= TOKEN_TILE:
        bb, ts = 1, TOKEN_TILE
    else:
        bb, ts = b, s
    n_t = s // ts
    assert s % ts == 0 and ts % 8 == 0 and (n_t == 1 or ts >= POOL_TOP)
    tile = pl.BlockSpec((bb, ts, d), lambda i, j: (i, j, 0))
    halo = pl.BlockSpec((bb, POOL_HALO, d), lambda i, j: (i, 0, 0))
    stage = pltpu.VMEM((bb, POOL_TOP + ts, POOL_GROUP_DIM), F32)
    return pl.pallas_call(
        functools.partial(_pool_body, g_row=g_row, pos0=pos0, bb=bb, ts=ts),
        grid=(b // bb, n_t),
        in_specs=[tile, halo, _member(g), _member(w_in), _member(w_grp), _member(scale), _member(w_out)],
        out_specs=[tile, halo],
        out_shape=[jax.ShapeDtypeStruct((b, s, d), F32),
                   jax.ShapeDtypeStruct((b, POOL_HALO, d), F32)],
        scratch_shapes=[pltpu.VMEM((bb, POOL_TOP + ts, d), F32), stage, stage],
        compiler_params=_params(2),
        name="pool_mixer",
    )(h, prev, g.arr, w_in.arr, w_grp.arr, scale.arr, w_out.arr)


def _alibi_slopes(n):
    return 2.0 ** (-8.0 * jnp.arange(1, n + 1, dtype=F32) / n)


def _band_table(dil, n_steps):
    qi = jnp.arange(BAND)[:, None]
    ki = jnp.arange(2 * BAND)[None, :]
    step = qi + BAND - ki
    valid = (step >= 0) & (step <= n_steps)
    bias = -_alibi_slopes(N_HEADS)[:, None, None] * (step * dil).astype(F32)[None]
    return jnp.where(valid[None], bias, NEG_INF)


def _pair_tables(dil, n_steps):
    tab = _band_table(dil, n_steps)
    both = tab.reshape(N_HEADS // 2, 2, BAND, 2 * BAND)
    with_prev = jnp.concatenate([both[:, 0], both[:, 1]], axis=-1)
    cur_only = jnp.concatenate([both[:, 0, :, BAND:], both[:, 1, :, BAND:]], axis=-1)
    return with_prev, cur_only


def _step_table(n_past, t_new, n_keys):
    dist = n_past + jnp.arange(t_new)[:, None] - jnp.arange(n_keys)[None, :]
    in_buf = (jnp.arange(n_keys) < n_past + t_new)[None, :]
    bias = -_alibi_slopes(N_HEADS)[:, None, None] * dist.astype(F32)[None]
    tabs = []
    for win, dil in DIL_GROUPS:
        valid = (dist >= 0) & (dist % dil == 0) & (dist <= win) & in_buf
        tabs.append(jnp.where(valid[None], bias, NEG_INF))
    return jnp.stack(tabs, axis=1).reshape(N_HEADS * N_DIL * t_new, n_keys)


def _band_attn_body(*refs, nb, lead):
    refs = list(refs)
    q_ref, kc_ref, vc_ref = [r.at[(0,) * lead] for r in refs[:3]]
    del refs[:3]
    if nb > 1:
        kp_ref, vp_ref = [r.at[(0,) * lead] for r in refs[:2]]
        del refs[:2]
        tab2_ref = refs.pop(0)
    tab1_ref = refs.pop(0)
    o_ref, l_ref = [r.at[(0,) * lead] for r in refs[:2]]
    s_ref, p_ref = refs[2:]
    n_pairs = N_HEADS // 2

    def run(with_prev):
        nk = 2 * BAND if with_prev else BAND
        low = lax.broadcasted_iota(jnp.int32, (nk, LANES), 1) < HEAD_DIM
        row_low = lax.broadcasted_iota(jnp.int32, (2 * nk, LANES), 0) < nk
        lane_low = lax.broadcasted_iota(jnp.int32, (2 * nk, LANES), 1) < HEAD_DIM
        ones_st = jnp.where(row_low == lane_low, 1.0, 0.0).astype(BF16)
        q_low = lax.broadcasted_iota(jnp.int32, (BAND, LANES), 1) < HEAD_DIM
        tab_ref = tab2_ref if with_prev else tab1_ref

        def stacked(c_ref, p_ref_, cs):
            x = c_ref[:, cs]
            if with_prev:
                x = jnp.concatenate([p_ref_[:, cs], x], axis=0)
            z = jnp.zeros_like(x)
            return jnp.concatenate([jnp.where(low, x, z), jnp.where(low, z, x)], axis=0)

        for j in range(n_pairs):
            cs = slice(j * LANES, (j + 1) * LANES)
            kst = stacked(kc_ref, kp_ref if with_prev else None, cs)
            s_ref[:, 2 * nk * j:2 * nk * (j + 1)] = _dot_nt(q_ref[:, cs], kst) + tab_ref[j]
        ms = []
        for h in range(N_HEADS):
            sh = s_ref[:, nk * h:nk * (h + 1)]
            m = jnp.max(sh, axis=-1, keepdims=True)
            p_ref[:, nk * h:nk * (h + 1)] = jnp.exp(sh - m).astype(BF16)
            ms.append(m)
        for j in range(n_pairs):
            cs = slice(j * LANES, (j + 1) * LANES)
            vst = stacked(vc_ref, vp_ref if with_prev else None, cs)
            r = _dot(p_ref[:, 2 * nk * j:2 * nk * (j + 1)], jnp.concatenate([vst, ones_st], axis=1))
            l = r[:, LANES:]
            o_ref[:, cs] = r[:, :LANES] / l
            m2 = jnp.where(q_low, jnp.broadcast_to(ms[2 * j], (BAND, LANES)),
                           jnp.broadcast_to(ms[2 * j + 1], (BAND, LANES)))
            l_ref[:, cs] = m2 + jnp.log(l)

    if nb == 1:
        run(False)
    else:
        n = pl.program_id(2)
        pl.when(n == 0)(lambda: run(False))
        pl.when(n > 0)(lambda: run(True))


def _band_attn_group(q, kb, vb, gi):
    win, dil = DIL_GROUPS[gi]
    d = D_MODEL
    if dil == 1:
        b, n, _ = q.shape
        lead = 1
        blk = (1, BAND, d)
        cur = pl.BlockSpec(blk, lambda i, r, j: (i, j, 0))
        prv = pl.BlockSpec(blk, lambda i, r, j: (i, jnp.maximum(j - 1, 0), 0))
    else:
        b, _, n, _ = q.shape
        lead = 2
        blk = (1, 1, BAND, d)
        cur = pl.BlockSpec(blk, lambda i, r, j: (i, r, j, 0))
        prv = pl.BlockSpec(blk, lambda i, r, j: (i, r, jnp.maximum(j - 1, 0), 0))
    nb = n // BAND
    assert n % BAND == 0
    tab2, tab1 = _pair_tables(dil, win // dil)
    in_specs = [cur, cur, cur]
    args = [q, kb, vb]
    if nb > 1:
        in_specs += [prv, prv, _resident(tab2.shape)]
        args += [kb, vb, tab2]
    in_specs.append(_resident(tab1.shape))
    args.append(tab1)
    out = jax.ShapeDtypeStruct(q.shape, F32)
    return pl.pallas_call(
        functools.partial(_band_attn_body, nb=nb, lead=lead),
        grid=(b, dil, nb),
        in_specs=in_specs,
        out_specs=[cur, cur],
        out_shape=[out, out],
        scratch_shapes=[pltpu.VMEM((BAND, N_HEADS * 2 * BAND), F32), pltpu.VMEM((BAND, N_HEADS * 2 * BAND), BF16)],
        compiler_params=_params(3),
        name=f"band_attn_g{gi}",
    )(*args)


STEP_HEADS = 4
STEP_COLS = STEP_HEADS * HEAD_DIM


def _step_attn_body(*refs, n_past, t_new, n_keys, write_cache):
    q_refs = refs[:N_DIL]
    ck_ref, cv_ref, kn_ref, vn_ref, tab_ref = refs[N_DIL:N_DIL + 5]
    refs = refs[N_DIL + 5:]
    o_ref = refs[0]
    if write_cache:
        ok_ref, ov_ref = refs[1:3]
        refs = refs[3:]
    else:
        refs = refs[1:]
    kb_ref, vb_ref = refs
    rows_g = N_DIL * t_new
    rows = STEP_HEADS * rows_g

    for c_ref, n_ref, b_ref in ((ck_ref, kn_ref, kb_ref), (cv_ref, vn_ref, vb_ref)):
        b_ref[0:n_past, :] = c_ref[0].astype(BF16)
        b_ref[n_past:n_past + t_new, :] = n_ref[0].astype(BF16)
        b_ref[n_past + t_new:n_keys, :] = jnp.zeros((n_keys - n_past - t_new, STEP_COLS), BF16)
    if write_cache:
        for c_ref, n_ref, out_ref in ((ck_ref, kn_ref, ok_ref), (cv_ref, vn_ref, ov_ref)):
            out_ref[0, 0:n_past - t_new, :] = c_ref[0, t_new:n_past, :]
            out_ref[0, n_past - t_new:n_past, :] = n_ref[0]

    q_rows = jnp.concatenate([r[0] for r in q_refs], axis=0)
    q_all = jnp.concatenate([q_rows] * STEP_HEADS, axis=0)
    row_head = lax.broadcasted_iota(jnp.int32, (rows, STEP_COLS), 0) // rows_g
    col_head = lax.broadcasted_iota(jnp.int32, (rows, STEP_COLS), 1) // HEAD_DIM
    own = row_head == col_head
    q_bd = jnp.where(own, q_all, jnp.zeros_like(q_all))
    s = _dot_nt(q_bd, kb_ref[...]) + tab_ref[...]
    m = jnp.max(s, axis=-1, keepdims=True)
    p = jnp.exp(s - m)
    l = jnp.sum(p, axis=-1, keepdims=True)
    o_full = _dot(p.astype(BF16), vb_ref[...]) / l
    lse = m + jnp.log(l)

    out = jnp.zeros((t_new, STEP_COLS), F32)
    col = lax.broadcasted_iota(jnp.int32, (t_new, STEP_COLS), 1) // HEAD_DIM
    for hh in range(STEP_HEADS):
        base = hh * rows_g
        lses = [lse[base + g * t_new:base + (g + 1) * t_new] for g in range(N_DIL)]
        mm = functools.reduce(jnp.maximum, lses)
        ws = [jnp.exp(x - mm) for x in lses]
        den = functools.reduce(lambda a, b: a + b, ws)
        acc = functools.reduce(
            lambda a, b: a + b,
            [ws[g] * o_full[base + g * t_new:base + (g + 1) * t_new] for g in range(N_DIL)])
        out = jnp.where(col == hh, acc / den, out)
    o_ref[0] = out.astype(o_ref.dtype)


def _step_attention(q, cache_k, cache_v, k_new, v_new, write_cache):
    b, t_new, _ = q.shape
    n_past = cache_k.shape[1]
    d = D_MODEL
    assert t_new % 8 == 0 and n_past % 8 == 0
    n_keys = -(-(n_past + t_new) // LANES) * LANES
    rows = STEP_HEADS * N_DIL * t_new
    tab = _step_table(n_past, t_new, n_keys)
    n_hg = N_HEADS // STEP_HEADS

    def q_spec(g):
        return pl.BlockSpec((1, t_new, STEP_COLS), lambda i, j: (i, 0, g * n_hg + j))

    cache_spec = pl.BlockSpec((1, n_past, STEP_COLS), lambda i, j: (i, 0, j))
    new_spec = pl.BlockSpec((1, t_new, STEP_COLS), lambda i, j: (i, 0, j))
    out_specs = [new_spec]
    out_shape = [jax.ShapeDtypeStruct((b, t_new, d), BF16)]
    if write_cache:
        out_specs += [cache_spec, cache_spec]
        out_shape += [jax.ShapeDtypeStruct((b, n_past, d), F32)] * 2
    return pl.pallas_call(
        functools.partial(_step_attn_body, n_past=n_past, t_new=t_new, n_keys=n_keys,
                          write_cache=write_cache),
        grid=(b, n_hg),
        in_specs=[q_spec(g) for g in range(N_DIL)]
        + [cache_spec, cache_spec, new_spec, new_spec,
           pl.BlockSpec((rows, n_keys), lambda i, j: (j, 0))],
        out_specs=out_specs,
        out_shape=out_shape,
        scratch_shapes=[pltpu.VMEM((n_keys, STEP_COLS), BF16)] * 2,
        compiler_params=_params(2),
        name="step_attn",
    )(q, q, q, cache_k, cache_v, k_new, v_new, tab)


G_FFN1, G_MIX, G_FFN2 = 0, 2, 4


def _ffn_w(w, layer, half):
    idx = (layer, half)
    return _Sel(w["ffn_w_gate"], idx), _Sel(w["ffn_w_up"], idx), _Sel(w["ffn_w_down"], idx)


def _pool_layers(h, b, s, pool_prev, pos0, w, n_a):
    d = D_MODEL
    pool_rows = []
    for layer in range(n_a):
        g = _Sel(w["norm_g"], (layer,))
        h = _ffn(h, g, G_FFN1, *_ffn_w(w, layer, 0))
        h3, rows = _pool_mixer(h.reshape(b, s, d), pool_prev[layer], g, G_MIX, _Sel(w["pool_w_in"], (layer,)),
                               _Sel(w["pool_w_grp"], (layer,)), _Sel(w["pool_scale"], (layer,)),
                               _Sel(w["pool_w_out"], (layer,)), pos0)
        h = h3.reshape(b * s, d)
        pool_rows.append(rows[:, POOL_HALO - POOL_BUF:])
        h = _ffn(h, g, G_FFN2, *_ffn_w(w, layer, 1))
    return h, jnp.stack(pool_rows)


def _prompt_trunk(x, w):
    b, s, d = x.shape
    t = b * s
    depth = w["norm_g"].shape[0]
    n_a = depth // 2
    assert s % (R16 * BAND) == 0 and s % TOKEN_TILE == 0
    empty = jnp.zeros((n_a, b, POOL_HALO, d), F32)
    h, pool_rows = _pool_layers(x.reshape(t, d), b, s, empty, 0, w, n_a)
    k32, v32, kn, vn, k4, v4, k16, v16 = _kv_streams(h, w["kv_norm"], w["w_k"], w["w_v"], b, s)
    kn, vn = kn.reshape(b, s, d), vn.reshape(b, s, d)
    for layer in range(n_a, depth):
        g = _Sel(w["norm_g"], (layer,))
        j = layer - n_a
        if layer == n_a:
            h = _ffn(h, g, G_FFN1, *_ffn_w(w, layer, 0), order_in="natural", order_out="by16", seq=s)
        else:
            h = _ffn(h, g, G_FFN1, *_ffn_w(w, layer, 0))
            h = h.reshape(b, R16, s // R16, d)
        q0, q1, q2 = _q_streams(h, g, G_MIX, _Sel(w["attn_w_q"], (j,)))
        groups = [_band_attn_group(q0, kn, vn, 0), _band_attn_group(q1, k4, v4, 1),
                  _band_attn_group(q2, k16, v16, 2)]
        h = _merge_out_proj(groups, h, g, G_MIX + 1, _Sel(w["attn_w_o"], (j,)))
        if layer == depth - 1:
            h = _ffn(h, g, G_FFN2, *_ffn_w(w, layer, 1), order_in="by16", order_out="natural", seq=s)
        else:
            h = _ffn(h.reshape(t, d), g, G_FFN2, *_ffn_w(w, layer, 1))
    buf = min(max(win for win, _ in DIL_GROUPS), s)
    k32, v32 = k32.reshape(b, s, d)[:, s - buf:], v32.reshape(b, s, d)[:, s - buf:]
    return h.reshape(b, s, d), pool_rows, k32, v32


def _sample_trunk(x, pool_prev, cache_k, cache_v, pos0, w):
    b, s, d = x.shape
    t = b * s
    depth = w["norm_g"].shape[0]
    n_a = depth // 2
    h, pool_rows = _pool_layers(x.reshape(t, d), b, s, pool_prev, pos0, w, n_a)
    k_new, v_new = _norm_proj(h, _Sel(w["kv_norm"]), 0, [_Sel(w["w_k"]), _Sel(w["w_v"])],
                              out_cfg=((0, F32, 1.0), (1, F32, 1.0)))
    k_new, v_new = k_new.reshape(b, s, d), v_new.reshape(b, s, d)
    for layer in range(n_a, depth):
        g = _Sel(w["norm_g"], (layer,))
        j = layer - n_a
        h = _ffn(h, g, G_FFN1, *_ffn_w(w, layer, 0))
        (q,) = _norm_proj(h, g, G_MIX, [_Sel(w["attn_w_q"], (j,))], out_cfg=((0, BF16, ATTN_SCALE),))
        res = _step_attention(q.reshape(b, s, N_DIL * d), cache_k, cache_v, k_new, v_new, write_cache=(j == 0))
        if j == 0:
            new_k, new_v = res[1], res[2]
        h = _out_proj(res[0].reshape(t, d), h, g, G_MIX + 1, _Sel(w["attn_w_o"], (j,)))
        h = _ffn(h, g, G_FFN2, *_ffn_w(w, layer, 1))
    return h.reshape(b, s, d), pool_rows, new_k, new_v


def kernel(x_prompt, x_sample, state_pool, cache_k, cache_v, norm_g, ffn_w_gate, ffn_w_up, ffn_w_down,
           pool_w_in, pool_w_grp, pool_scale, pool_w_out, kv_norm, w_k, w_v, attn_w_q, attn_w_o):
    d = D_MODEL
    w = dict(
        norm_g=norm_g,
        ffn_w_gate=ffn_w_gate.astype(BF16), ffn_w_up=ffn_w_up.astype(BF16), ffn_w_down=ffn_w_down.astype(BF16),
        pool_w_in=pool_w_in.astype(BF16), pool_w_grp=pool_w_grp.astype(BF16),
        pool_scale=pool_scale.reshape(-1, 1, d), pool_w_out=pool_w_out.astype(BF16),
        kv_norm=kv_norm.reshape(1, d), w_k=w_k.astype(BF16), w_v=w_v.astype(BF16),
        attn_w_q=attn_w_q.astype(BF16), attn_w_o=attn_w_o.astype(BF16),
    )
    bs, n_past = cache_k.shape[:2]
    y_p, pool_p, k_p, v_p = _prompt_trunk(x_prompt, w)

    prev_s = jnp.pad(state_pool, ((0, 0), (0, 0), (POOL_HALO - POOL_BUF, 0), (0, 0)))
    y_s, pool_s, k_s, v_s = _sample_trunk(x_sample, prev_s, cache_k.reshape(bs, n_past, d),
                                          cache_v.reshape(bs, n_past, d), PAST_LEN, w)
    hd = (N_HEADS, HEAD_DIM)
    return (y_p, y_s, pool_p,
            k_p.reshape(k_p.shape[:2] + hd), v_p.reshape(v_p.shape[:2] + hd),
            pool_s,
            k_s.reshape(k_s.shape[:2] + hd), v_s.reshape(v_s.shape[:2] + hd))
```

```python
import functools
from typing import NamedTuple

import jax
import jax.numpy as jnp
from jax import lax
from jax.experimental import pallas as pl
from jax.experimental.pallas import tpu as pltpu

D_MODEL = 1024
D_FF = 2816
HEAD_DIM = 64
N_HEADS = D_MODEL // HEAD_DIM
POOL_WINDOWS = (2, 4, 8, 16)
POOL_GROUP_DIM = D_MODEL // len(POOL_WINDOWS)
POOL_BUF = max(POOL_WINDOWS) - 1
POOL_HALO = 16
POOL_TOP = 32
DIL_GROUPS = ((128, 1), (512, 4), (2048, 16))
N_DIL = len(DIL_GROUPS)
BAND = 128
ATTN_SUBS = 2
PAST_LEN = 8192
RMS_EPS = 1e-6
NEG_INF = -1e30
ATTN_SCALE = HEAD_DIM ** -0.5
LOG2E = 1.4426950408889634
LN2 = 0.6931471805599453
LANES = 128
N_SLABS = D_MODEL // LANES
TOKEN_TILE = 512
FFN_SPLIT_ROWS = 256
R16 = 16
ROWS16 = TOKEN_TILE // R16
VMEM_LIMIT_BYTES = 56 * 1024 * 1024

F32 = jnp.float32
BF16 = jnp.bfloat16


def _params(n_axes):
    return pltpu.CompilerParams(dimension_semantics=("arbitrary",) * n_axes,
                                vmem_limit_bytes=VMEM_LIMIT_BYTES)


def _resident(shape):
    zeros = (0,) * len(shape)
    return pl.BlockSpec(shape, lambda *_: zeros, pipeline_mode=pl.Buffered(1))


class _Sel(NamedTuple):
    arr: jax.Array
    idx: tuple = ()


def _member(sel):
    lead = len(sel.idx)
    shape = sel.arr.shape[lead:]
    index = tuple(sel.idx) + (0,) * len(shape)
    return pl.BlockSpec((None,) * lead + shape, lambda *_: index, pipeline_mode=pl.Buffered(1))


def _rms(x, g):
    return x * lax.rsqrt(jnp.mean(x * x, axis=-1, keepdims=True) + RMS_EPS) * g


def _dot(a, b):
    return jnp.dot(a, b, preferred_element_type=F32)


def _dot_nt(a, b):
    return lax.dot_general(a, b, (((1,), (1,)), ((), ())), preferred_element_type=F32)


def _token_tile(t):
    for tm in (TOKEN_TILE, 256, 128, 64, 32, 16, 8):
        if t % tm == 0:
            return tm
    raise ValueError(f"token count {t} is not a multiple of 8")


SLAB_SCRATCH = pltpu.VMEM((N_SLABS, TOKEN_TILE, LANES), F32)


def _slabs_put(scr, val):
    for j in range(val.shape[-1] // LANES):
        scr[j] = val[:, j * LANES:(j + 1) * LANES]


def _slabs_get(scr, n_slabs):
    return jnp.concatenate([scr[j] for j in range(n_slabs)], axis=-1)


def _slabs_gather(scr, n_slabs, start, n, stride):
    return jnp.concatenate([scr[j, pl.ds(start, n, stride=stride), :] for j in range(n_slabs)], axis=-1)


def _slabs_scatter(scr, start, n, stride, val):
    for j in range(val.shape[-1] // LANES):
        scr[j, pl.ds(start, n, stride=stride), :] = val[:, j * LANES:(j + 1) * LANES]


def _natural_to_by16(scr, val):
    n_slabs = val.shape[-1] // LANES
    _slabs_put(scr, val)
    return [_slabs_gather(scr, n_slabs, r, ROWS16, R16) for r in range(R16)]


def _by16_to_natural(scr, val):
    for r in range(R16):
        _slabs_scatter(scr, r, ROWS16, R16, val[r * ROWS16:(r + 1) * ROWS16])
    return _slabs_get(scr, val.shape[-1] // LANES)


def _by4_to_by16(scr, val):
    n_slabs = val.shape[-1] // LANES
    _slabs_put(scr, val)
    per_r4 = TOKEN_TILE // 4
    parts = [None] * R16
    for r4 in range(4):
        for c in range(4):
            parts[4 * c + r4] = _slabs_gather(scr, n_slabs, r4 * per_r4 + c, ROWS16, 4)
    return jnp.concatenate(parts, axis=0)


def _by16_to_by4(scr, val):
    per_r4 = TOKEN_TILE // 4
    for r4 in range(4):
        for c in range(4):
            r16 = 4 * c + r4
            _slabs_scatter(scr, r4 * per_r4 + c, ROWS16, 4, val[r16 * ROWS16:(r16 + 1) * ROWS16])
    return _slabs_get(scr, val.shape[-1] // LANES)


def _tile16_spec(b_tiles, c):
    return pl.BlockSpec((1, R16, ROWS16, c), lambda i: (i // b_tiles, 0, i % b_tiles, 0))


def _tile4_spec(b_tiles, c):
    return pl.BlockSpec((1, 4, TOKEN_TILE // 4, c), lambda i: (i // b_tiles, 0, i % b_tiles, 0))


def _ffn_body(x_ref, g_ref, wg_ref, wu_ref, wd_ref, o_ref, *scratch, g_row, order_in, order_out):
    d = D_MODEL

    def half_step(x):
        xn = _rms(x, g_ref[g_row:g_row + 1, :]).astype(BF16)
        hg = _dot(xn, wg_ref[...])
        hu = _dot(xn, wu_ref[...])
        a = (hg / (1.0 + jnp.exp(-hg)) * hu).astype(BF16)
        f = _dot(a, wd_ref[...])
        return x + 0.5 * _rms(f, g_ref[g_row + 1:g_row + 2, :])

    x = x_ref[...].reshape(-1, d)
    n = x.shape[0]
    if n % FFN_SPLIT_ROWS == 0 and n > FFN_SPLIT_ROWS:
        y = jnp.concatenate([half_step(x[i:i + FFN_SPLIT_ROWS]) for i in range(0, n, FFN_SPLIT_ROWS)], axis=0)
    else:
        y = half_step(x)
    if order_in == order_out:
        o_ref[...] = y.reshape(o_ref.shape)
    elif order_out == "by16":
        for r, rows in enumerate(_natural_to_by16(scratch[0], y)):
            o_ref[0, r] = rows
    else:
        o_ref[...] = _by16_to_natural(scratch[0], y)


def _ffn(x, g, g_row, wg, wu, wd, order_in="natural", order_out="natural", seq=None):
    d = D_MODEL
    regroup = order_in != order_out
    if regroup:
        b_tiles = seq // TOKEN_TILE
        t = x.size // d
        tm = TOKEN_TILE
        nat = pl.BlockSpec((tm, d), lambda i: (i, 0))
        p16 = _tile16_spec(b_tiles, d)
        in_spec, out_spec = (nat, p16) if order_out == "by16" else (p16, nat)
        out_shape = (t // seq, R16, seq // R16, d) if order_out == "by16" else (t, d)
        scratch = [SLAB_SCRATCH]
    else:
        t = x.shape[0]
        tm = _token_tile(t)
        in_spec = out_spec = pl.BlockSpec((tm, d), lambda i: (i, 0))
        out_shape = (t, d)
        scratch = []
    return pl.pallas_call(
        functools.partial(_ffn_body, g_row=g_row, order_in=order_in, order_out=order_out),
        grid=(t // tm,),
        in_specs=[in_spec, _member(g), _member(wg), _member(wu), _member(wd)],
        out_specs=out_spec,
        out_shape=jax.ShapeDtypeStruct(out_shape, F32),
        scratch_shapes=scratch,
        compiler_params=_params(1),
        name="ffn",
    )(x, g.arr, wg.arr, wu.arr, wd.arr)


def _proj_body(x_ref, g_ref, *refs, g_row, out_cfg):
    n_w = max(wi for wi, _, _ in out_cfg) + 1
    w_refs, o_refs = refs[:n_w], refs[n_w:]
    xn = _rms(x_ref[...], g_ref[g_row:g_row + 1, :]).astype(BF16)
    ys = [_dot(xn, w[...]) for w in w_refs]
    for o_ref, (wi, _, scale) in zip(o_refs, out_cfg):
        y = ys[wi] if scale == 1.0 else ys[wi] * scale
        o_ref[...] = y.astype(o_ref.dtype)


def _norm_proj(x, g, g_row, ws, out_cfg):
    t, d = x.shape
    tm = _token_tile(t)
    tile = pl.BlockSpec((tm, d), lambda i: (i, 0))
    n_out = [w.arr.shape[-1] for w in ws]
    return pl.pallas_call(
        functools.partial(_proj_body, g_row=g_row, out_cfg=out_cfg),
        grid=(t // tm,),
        in_specs=[tile, _member(g)] + [_member(w) for w in ws],
        out_specs=[pl.BlockSpec((tm, n_out[wi]), lambda i: (i, 0)) for wi, _, _ in out_cfg],
        out_shape=[jax.ShapeDtypeStruct((t, n_out[wi]), dt) for wi, dt, _ in out_cfg],
        compiler_params=_params(1),
        name="norm_proj",
    )(x, g.arr, *[w.arr for w in ws])


def _kv_streams_body(x_ref, g_ref, wk_ref, wv_ref, k32_ref, v32_ref, kn_ref, vn_ref, k4_ref, v4_ref,
                     k16_ref, v16_ref, scr):
    xn = _rms(x_ref[...], g_ref[...]).astype(BF16)
    per_r4 = TOKEN_TILE // 4
    for w_ref, o32, on, o4, o16 in ((wk_ref, k32_ref, kn_ref, k4_ref, k16_ref),
                                    (wv_ref, v32_ref, vn_ref, v4_ref, v16_ref)):
        y = _dot(xn, w_ref[...])
        o32[...] = y
        on[...] = y.astype(BF16)
        _slabs_put(scr, y)
        for r4 in range(4):
            o4[0, r4] = _slabs_gather(scr, N_SLABS, r4, per_r4, 4).astype(BF16)
        for r in range(R16):
            o16[0, r] = _slabs_gather(scr, N_SLABS, r, ROWS16, R16).astype(BF16)


def _kv_streams(x, g, wk, wv, b, s):
    t, d = x.shape
    b_tiles = s // TOKEN_TILE
    nat = pl.BlockSpec((TOKEN_TILE, d), lambda i: (i, 0))
    f32n = jax.ShapeDtypeStruct((t, d), F32)
    bfn = jax.ShapeDtypeStruct((t, d), BF16)
    bf4 = jax.ShapeDtypeStruct((b, 4, s // 4, d), BF16)
    bf16_ = jax.ShapeDtypeStruct((b, R16, s // R16, d), BF16)
    return pl.pallas_call(
        _kv_streams_body,
        grid=(t // TOKEN_TILE,),
        in_specs=[nat, _resident(g.shape), _resident(wk.shape), _resident(wv.shape)],
        out_specs=[nat, nat, nat, nat, _tile4_spec(b_tiles, d), _tile4_spec(b_tiles, d),
                   _tile16_spec(b_tiles, d), _tile16_spec(b_tiles, d)],
        out_shape=[f32n, f32n, bfn, bfn, bf4, bf4, bf16_, bf16_],
        scratch_shapes=[SLAB_SCRATCH],
        compiler_params=_params(1),
        name="kv_streams",
    )(x, g, wk, wv)


def _q_streams_body(x_ref, g_ref, w_ref, q0_ref, q1_ref, q2_ref, scr, *, g_row):
    d = D_MODEL
    xn = _rms(x_ref[0].reshape(TOKEN_TILE, d), g_ref[g_row:g_row + 1, :]).astype(BF16)
    scale = ATTN_SCALE * LOG2E
    y0 = _dot(xn, w_ref[:, 0:d]) * scale
    q0_ref[0] = _by16_to_natural(scr, y0).astype(BF16)
    y1 = _dot(xn, w_ref[:, d:2 * d]) * scale
    q1_ref[0] = _by16_to_by4(scr, y1).astype(BF16).reshape(4, TOKEN_TILE // 4, d)
    y2 = _dot(xn, w_ref[:, 2 * d:3 * d]) * scale
    q2_ref[0] = y2.astype(BF16).reshape(R16, ROWS16, d)


def _q_streams(h16, g, g_row, wq):
    b, _, n16, d = h16.shape
    s = n16 * R16
    b_tiles = s // TOKEN_TILE
    return pl.pallas_call(
        functools.partial(_q_streams_body, g_row=g_row),
        grid=(b * b_tiles,),
        in_specs=[_tile16_spec(b_tiles, d), _member(g), _member(wq)],
        out_specs=[pl.BlockSpec((1, TOKEN_TILE, d), lambda i: (i // b_tiles, i % b_tiles, 0)),
                   _tile4_spec(b_tiles, d), _tile16_spec(b_tiles, d)],
        out_shape=[jax.ShapeDtypeStruct((b, s, d), BF16), jax.ShapeDtypeStruct((b, 4, s // 4, d), BF16),
                   jax.ShapeDtypeStruct((b, R16, n16, d), BF16)],
        scratch_shapes=[SLAB_SCRATCH],
        compiler_params=_params(1),
        name="q_streams",
    )(h16, g.arr, wq.arr)


def _oproj_body(o_ref, h_ref, g_ref, w_ref, out_ref, *, g_row):
    y = _dot(o_ref[...], w_ref[...])
    out_ref[...] = h_ref[...] + _rms(y, g_ref[g_row:g_row + 1, :])


def _out_proj(o, h, g, g_row, w):
    t, d = h.shape
    tm = _token_tile(t)
    tile = pl.BlockSpec((tm, d), lambda i: (i, 0))
    return pl.pallas_call(
        functools.partial(_oproj_body, g_row=g_row),
        grid=(t // tm,),
        in_specs=[tile, tile, _member(g), _member(w)],
        out_specs=tile,
        out_shape=jax.ShapeDtypeStruct((t, d), F32),
        compiler_params=_params(1),
        name="out_proj",
    )(o, h, g.arr, w.arr)


def _head_stat_lane(h):
    return (h % 2) * HEAD_DIM + h // 2


def _stat_expander():
    col_head = jnp.arange(D_MODEL) // HEAD_DIM
    row = jnp.arange(LANES)
    row_head = jnp.where(row % HEAD_DIM < N_HEADS // 2, 2 * (row % HEAD_DIM) + row // HEAD_DIM, -1)
    e = (row_head[:, None] == col_head[None, :]).astype(BF16)
    return jnp.concatenate([e, e], axis=0)


def _merge_oproj_body(o0_ref, l0_ref, o1_ref, l1_ref, o2_ref, l2_ref, h_ref, g_ref, w_ref, e_ref, out_ref, scr, *,
                      g_row):
    d = D_MODEL
    o0 = jnp.concatenate(_natural_to_by16(scr, o0_ref[0]), axis=0)
    l0 = jnp.concatenate(_natural_to_by16(scr, l0_ref[0]), axis=0)
    o1 = _by4_to_by16(scr, o1_ref[0].reshape(TOKEN_TILE, d))
    l1 = _by4_to_by16(scr, l1_ref[0].reshape(TOKEN_TILE, LANES))
    o2 = o2_ref[0].reshape(TOKEN_TILE, d)
    l2 = l2_ref[0].reshape(TOKEN_TILE, LANES)
    mm = jnp.maximum(jnp.maximum(l0, l1), l2)
    es = [jnp.exp(l - mm) for l in (l0, l1, l2)]
    den = es[0] + es[1] + es[2]
    o = None
    for e, og in zip(es, (o0, o1, o2)):
        wgt = e / den
        hi = wgt.astype(BF16)
        lo = (wgt - hi.astype(F32)).astype(BF16)
        term = _dot(jnp.concatenate([hi, lo], axis=1), e_ref[...]) * og
        o = term if o is None else o + term
    y = _dot(o.astype(BF16), w_ref[...])
    out_ref[0] = (h_ref[0].reshape(TOKEN_TILE, d) + _rms(y, g_ref[g_row:g_row + 1, :])).reshape(R16, ROWS16, d)


def _merge_out_proj(groups, h16, g, g_row, w):
    b, _, n16, d = h16.shape
    s = n16 * R16
    b_tiles = s // TOKEN_TILE
    (o0, l0), (o1, l1), (o2, l2) = groups
    expander = _stat_expander()

    def specs(c):
        return (pl.BlockSpec((1, TOKEN_TILE, c), lambda i: (i // b_tiles, i % b_tiles, 0)),
                _tile4_spec(b_tiles, c), _tile16_spec(b_tiles, c))

    (nat, p4, p16), (nat_s, p4_s, p16_s) = specs(d), specs(LANES)
    return pl.pallas_call(
        functools.partial(_merge_oproj_body, g_row=g_row),
        grid=(b * b_tiles,),
        in_specs=[nat, nat_s, p4, p4_s, p16, p16_s, p16, _member(g), _member(w), _resident(expander.shape)],
        out_specs=p16,
        out_shape=jax.ShapeDtypeStruct(h16.shape, F32),
        scratch_shapes=[SLAB_SCRATCH],
        compiler_params=_params(1),
        name="merge_out_proj",
    )(o0, l0, o1, l1, o2, l2, h16, g.arr, w.arr, expander)


def _pool_body(h_ref, prev_ref, g_ref, win_ref, wgrp_ref, scale_ref, wout_ref,
               out_ref, rows_ref, ext_ref, sa_ref, sb_ref, *, g_row, pos0, bb, ts):
    d = D_MODEL
    j = pl.program_id(1)
    top, end = POOL_TOP, POOL_TOP + ts

    @pl.when(j == 0)
    def _():
        ext_ref[:, 0:top - POOL_HALO, :] = jnp.zeros((bb, top - POOL_HALO, d), F32)
        ext_ref[:, top - POOL_HALO:top, :] = prev_ref[...]

    @pl.when(j > 0)
    def _():
        ext_ref[:, 0:top, :] = ext_ref[:, ts:end, :]

    h = h_ref[...].reshape(bb * ts, d)
    hn = _rms(h, g_ref[g_row:g_row + 1, :]).astype(BF16)
    u = _dot(hn, win_ref[...])
    ext_ref[:, top:end, :] = u.reshape(bb, ts, d)
    rows_ref[...] = ext_ref[:, end - POOL_HALO:end, :]

    def doubled(ref, cs, lo, shift):
        return ref[:, lo:end, cs] + ref[:, lo - shift:end - shift, cs]

    every = slice(None)
    pos = pos0 + j * ts + lax.broadcasted_iota(jnp.int32, (1, ts, 1), 1)
    zs = []
    for gi, win in enumerate(POOL_WINDOWS):
        cs = slice(gi * POOL_GROUP_DIM, (gi + 1) * POOL_GROUP_DIM)
        n_stage = win.bit_length() - 1
        src, src_cs = ext_ref, cs
        stage_refs = (sa_ref, sb_ref)
        for k in range(n_stage - 1):
            lo = top - 8 * (n_stage - 1 - k)
            dst = stage_refs[k % 2]
            dst[:, lo:end, :] = doubled(src, src_cs, lo, 1 << k)
            src, src_cs = dst, every
        tot = doubled(src, src_cs, top, win // 2)
        cur = ext_ref[:, top:end, cs]
        cnt = jnp.minimum(pos + 1, win).astype(F32)
        z = (tot / cnt - cur).reshape(bb * ts, POOL_GROUP_DIM).astype(BF16)
        zs.append(_dot(z, wgrp_ref[gi]))
    zc = (jnp.concatenate(zs, axis=-1) * scale_ref[...]).astype(BF16)
    y = _dot(zc, wout_ref[...])
    out_ref[...] = (h + _rms(y, g_ref[g_row + 1:g_row + 2, :])).reshape(bb, ts, d)


def _pool_mixer(h, prev, g, g_row, w_in, w_grp, scale, w_out, pos0):
    b, s, d = h.shape
    if s >= TOKEN_TILE:
        bb, ts = 1, TOKEN_TILE
    else:
        bb, ts = b, s
    n_t = s // ts
    assert s % ts == 0 and ts % 8 == 0 and (n_t == 1 or ts >= POOL_TOP)
    tile = pl.BlockSpec((bb, ts, d), lambda i, j: (i, j, 0))
    halo = pl.BlockSpec((bb, POOL_HALO, d), lambda i, j: (i, 0, 0))
    stage = pltpu.VMEM((bb, POOL_TOP + ts, POOL_GROUP_DIM), F32)
    return pl.pallas_call(
        functools.partial(_pool_body, g_row=g_row, pos0=pos0, bb=bb, ts=ts),
        grid=(b // bb, n_t),
        in_specs=[tile, halo, _member(g), _member(w_in), _member(w_grp), _member(scale), _member(w_out)],
        out_specs=[tile, halo],
        out_shape=[jax.ShapeDtypeStruct((b, s, d), F32),
                   jax.ShapeDtypeStruct((b, POOL_HALO, d), F32)],
        scratch_shapes=[pltpu.VMEM((bb, POOL_TOP + ts, d), F32), stage, stage],
        compiler_params=_params(2),
        name="pool_mixer",
    )(h, prev, g.arr, w_in.arr, w_grp.arr, scale.arr, w_out.arr)


def _alibi_slopes(n):
    return 2.0 ** (-8.0 * jnp.arange(1, n + 1, dtype=F32) / n)


def _band_table(dil, n_steps):
    qi = jnp.arange(BAND)[:, None]
    ki = jnp.arange(2 * BAND)[None, :]
    step = qi + BAND - ki
    valid = (step >= 0) & (step <= n_steps)
    bias = -_alibi_slopes(N_HEADS)[:, None, None] * (step * dil).astype(F32)[None]
    return jnp.where(valid[None], bias, NEG_INF)


def _pair_tables(dil, n_steps):
    tab = _band_table(dil, n_steps) * LOG2E
    both = tab.reshape(N_HEADS // 2, 2, BAND, 2 * BAND)
    with_prev = jnp.concatenate([both[:, 0], both[:, 1]], axis=-1)
    cur_only = jnp.concatenate([both[:, 0, :, BAND:], both[:, 1, :, BAND:]], axis=-1)
    return with_prev, cur_only


def _step_table(n_past, t_new, n_keys):
    dist = n_past + jnp.arange(t_new)[:, None] - jnp.arange(n_keys)[None, :]
    in_buf = (jnp.arange(n_keys) < n_past + t_new)[None, :]
    bias = -_alibi_slopes(N_HEADS)[:, None, None] * dist.astype(F32)[None]
    tabs = []
    for win, dil in DIL_GROUPS:
        valid = (dist >= 0) & (dist % dil == 0) & (dist <= win) & in_buf
        tabs.append(jnp.where(valid[None], bias, NEG_INF))
    return jnp.stack(tabs, axis=1).reshape(N_HEADS * N_DIL * t_new, n_keys)


def _band_attn_body(*refs, chained):
    if chained:
        q_ref, kc_ref, vc_ref, kp_ref, vp_ref, tab2_ref, tab1_ref, o_ref, l_ref, s_ref, p_ref = refs
    else:
        q_ref, kc_ref, vc_ref, tab1_ref, o_ref, l_ref, s_ref, p_ref = refs
        kp_ref = vp_ref = tab2_ref = None
    n_pairs = N_HEADS // 2
    lead = (0,) * (len(q_ref.shape) - (2 if chained else 3))

    def rows(ref, sub):
        view = ref.at[lead]
        return view.at[pl.ds(sub * BAND, BAND)] if chained else view.at[sub]

    def attend(sub, k_prev, v_prev):
        q_v, kc_v, vc_v, o_v, l_v = (rows(r, sub) for r in (q_ref, kc_ref, vc_ref, o_ref, l_ref))
        s_v, p_v = s_ref.at[sub], p_ref.at[sub]
        with_prev = k_prev is not None
        nk = 2 * BAND if with_prev else BAND
        low = lax.broadcasted_iota(jnp.int32, (nk, LANES), 1) < HEAD_DIM
        row_low = lax.broadcasted_iota(jnp.int32, (2 * nk, LANES), 0) < nk
        lane_low = lax.broadcasted_iota(jnp.int32, (2 * nk, LANES), 1) < HEAD_DIM
        ones_st = jnp.where(row_low == lane_low, 1.0, 0.0).astype(BF16)
        lane = lax.broadcasted_iota(jnp.int32, (BAND, LANES), 1)
        q_low = lane < HEAD_DIM
        tab_ref = tab2_ref if with_prev else tab1_ref

        def stacked(c_v, p_v_, cs):
            x = c_v[:, cs]
            if with_prev:
                x = jnp.concatenate([p_v_[:, cs], x], axis=0)
            z = jnp.zeros_like(x)
            return jnp.concatenate([jnp.where(low, x, z), jnp.where(low, z, x)], axis=0)

        for j in range(n_pairs):
            cs = slice(j * LANES, (j + 1) * LANES)
            s_v[:, 2 * nk * j:2 * nk * (j + 1)] = _dot_nt(q_v[:, cs], stacked(kc_v, k_prev, cs)) + tab_ref[j]
        ms = []
        for h in range(N_HEADS):
            sh = s_v[:, nk * h:nk * (h + 1)]
            m = jnp.max(sh, axis=-1, keepdims=True)
            p_v[:, nk * h:nk * (h + 1)] = jnp.exp2(sh - m).astype(BF16)
            ms.append(m)
        stats = jnp.zeros((BAND, LANES), F32)
        for j in range(n_pairs):
            cs = slice(j * LANES, (j + 1) * LANES)
            aug = jnp.concatenate([stacked(vc_v, v_prev, cs), ones_st], axis=1)
            r = _dot(p_v[:, 2 * nk * j:2 * nk * (j + 1)], aug)
            l = r[:, LANES:]
            o_v[:, cs] = r[:, :LANES] / l
            m2 = jnp.where(q_low, jnp.broadcast_to(ms[2 * j], (BAND, LANES)),
                           jnp.broadcast_to(ms[2 * j + 1], (BAND, LANES)))
            stats = jnp.where((lane & (HEAD_DIM - 1)) == j, (m2 + jnp.log2(l)) * LN2, stats)
        l_v[...] = stats

    if chained:
        first = pl.program_id(2) == 0
        kp_v, vp_v = kp_ref.at[lead], vp_ref.at[lead]
        pl.when(first)(lambda: attend(0, None, None))
        pl.when(jnp.logical_not(first))(lambda: attend(0, kp_v, vp_v))
        for sub in range(1, ATTN_SUBS):
            attend(sub, rows(kc_ref, sub - 1), rows(vc_ref, sub - 1))
    else:
        for sub in range(ATTN_SUBS):
            attend(sub, None, None)


def _band_attn_group(q, kb, vb, gi):
    win, dil = DIL_GROUPS[gi]
    d = D_MODEL
    n = q.shape[-2]
    b = q.shape[0]
    chained = n > BAND
    step_rows = ATTN_SUBS * BAND

    def spec(c):
        if dil == 1:
            return (pl.BlockSpec((1, step_rows, c), lambda i, r, j: (i, j, 0)),
                    pl.BlockSpec((1, BAND, c), lambda i, r, j: (i, jnp.maximum(ATTN_SUBS * j - 1, 0), 0)))
        if chained:
            return (pl.BlockSpec((1, 1, step_rows, c), lambda i, r, j: (i, r, j, 0)),
                    pl.BlockSpec((1, 1, BAND, c), lambda i, r, j: (i, r, jnp.maximum(ATTN_SUBS * j - 1, 0), 0)))
        return pl.BlockSpec((1, ATTN_SUBS, BAND, c), lambda i, r, j: (i, r, 0, 0)), None

    if chained:
        assert n % step_rows == 0
        grid = (b, dil, n // step_rows)
    else:
        assert n == BAND and dil % ATTN_SUBS == 0
        grid = (b, dil // ATTN_SUBS, 1)
    cur, prv = spec(d)
    stat, _ = spec(LANES)
    tab2, tab1 = _pair_tables(dil, win // dil)
    in_specs = [cur, cur, cur]
    args = [q, kb, vb]
    if chained:
        in_specs += [prv, prv, _resident(tab2.shape)]
        args += [kb, vb, tab2]
    in_specs.append(_resident(tab1.shape))
    args.append(tab1)
    width = N_HEADS * 2 * BAND
    return pl.pallas_call(
        functools.partial(_band_attn_body, chained=chained),
        grid=grid,
        in_specs=in_specs,
        out_specs=[cur, stat],
        out_shape=[jax.ShapeDtypeStruct(q.shape, F32), jax.ShapeDtypeStruct(q.shape[:-1] + (LANES,), F32)],
        scratch_shapes=[pltpu.VMEM((ATTN_SUBS, BAND, width), F32), pltpu.VMEM((ATTN_SUBS, BAND, width), BF16)],
        compiler_params=_params(3),
        name=f"band_attn_g{gi}",
    )(*args)


STEP_HEADS = 4
STEP_COLS = STEP_HEADS * HEAD_DIM


def _step_attn_body(*refs, n_past, t_new, n_keys, write_cache):
    q_refs = refs[:N_DIL]
    ck_ref, cv_ref, kn_ref, vn_ref, tab_ref = refs[N_DIL:N_DIL + 5]
    refs = refs[N_DIL + 5:]
    o_ref = refs[0]
    if write_cache:
        ok_ref, ov_ref = refs[1:3]
        refs = refs[3:]
    else:
        refs = refs[1:]
    kb_ref, vb_ref = refs
    rows_g = N_DIL * t_new
    rows = STEP_HEADS * rows_g

    for c_ref, n_ref, b_ref in ((ck_ref, kn_ref, kb_ref), (cv_ref, vn_ref, vb_ref)):
        b_ref[0:n_past, :] = c_ref[0].astype(BF16)
        b_ref[n_past:n_past + t_new, :] = n_ref[0].astype(BF16)
        b_ref[n_past + t_new:n_keys, :] = jnp.zeros((n_keys - n_past - t_new, STEP_COLS), BF16)
    if write_cache:
        for c_ref, n_ref, out_ref in ((ck_ref, kn_ref, ok_ref), (cv_ref, vn_ref, ov_ref)):
            out_ref[0, 0:n_past - t_new, :] = c_ref[0, t_new:n_past, :]
            out_ref[0, n_past - t_new:n_past, :] = n_ref[0]

    q_rows = jnp.concatenate([r[0] for r in q_refs], axis=0)
    q_all = jnp.concatenate([q_rows] * STEP_HEADS, axis=0)
    row_head = lax.broadcasted_iota(jnp.int32, (rows, STEP_COLS), 0) // rows_g
    col_head = lax.broadcasted_iota(jnp.int32, (rows, STEP_COLS), 1) // HEAD_DIM
    own = row_head == col_head
    q_bd = jnp.where(own, q_all, jnp.zeros_like(q_all))
    s = _dot_nt(q_bd, kb_ref[...]) + tab_ref[...]
    m = jnp.max(s, axis=-1, keepdims=True)
    p = jnp.exp(s - m)
    l = jnp.sum(p, axis=-1, keepdims=True)
    o_full = _dot(p.astype(BF16), vb_ref[...]) / l
    lse = m + jnp.log(l)

    out = jnp.zeros((t_new, STEP_COLS), F32)
    col = lax.broadcasted_iota(jnp.int32, (t_new, STEP_COLS), 1) // HEAD_DIM
    for hh in range(STEP_HEADS):
        base = hh * rows_g
        lses = [lse[base + g * t_new:base + (g + 1) * t_new] for g in range(N_DIL)]
        mm = functools.reduce(jnp.maximum, lses)
        ws = [jnp.exp(x - mm) for x in lses]
        den = functools.reduce(lambda a, b: a + b, ws)
        acc = functools.reduce(
            lambda a, b: a + b,
            [ws[g] * o_full[base + g * t_new:base + (g + 1) * t_new] for g in range(N_DIL)])
        out = jnp.where(col == hh, acc / den, out)
    o_ref[0] = out.astype(o_ref.dtype)


def _step_attention(q, cache_k, cache_v, k_new, v_new, write_cache):
    b, t_new, _ = q.shape
    n_past = cache_k.shape[1]
    d = D_MODEL
    assert t_new % 8 == 0 and n_past % 8 == 0
    n_keys = -(-(n_past + t_new) // LANES) * LANES
    rows = STEP_HEADS * N_DIL * t_new
    tab = _step_table(n_past, t_new, n_keys)
    n_hg = N_HEADS // STEP_HEADS

    def q_spec(g):
        return pl.BlockSpec((1, t_new, STEP_COLS), lambda i, j: (i, 0, g * n_hg + j))

    cache_spec = pl.BlockSpec((1, n_past, STEP_COLS), lambda i, j: (i, 0, j))
    new_spec = pl.BlockSpec((1, t_new, STEP_COLS), lambda i, j: (i, 0, j))
    out_specs = [new_spec]
    out_shape = [jax.ShapeDtypeStruct((b, t_new, d), BF16)]
    if write_cache:
        out_specs += [cache_spec, cache_spec]
        out_shape += [jax.ShapeDtypeStruct((b, n_past, d), F32)] * 2
    return pl.pallas_call(
        functools.partial(_step_attn_body, n_past=n_past, t_new=t_new, n_keys=n_keys,
                          write_cache=write_cache),
        grid=(b, n_hg),
        in_specs=[q_spec(g) for g in range(N_DIL)]
        + [cache_spec, cache_spec, new_spec, new_spec,
           pl.BlockSpec((rows, n_keys), lambda i, j: (j, 0))],
        out_specs=out_specs,
        out_shape=out_shape,
        scratch_shapes=[pltpu.VMEM((n_keys, STEP_COLS), BF16)] * 2,
        compiler_params=_params(2),
        name="step_attn",
    )(q, q, q, cache_k, cache_v, k_new, v_new, tab)


G_FFN1, G_MIX, G_FFN2 = 0, 2, 4


def _ffn_w(w, layer, half):
    idx = (layer, half)
    return _Sel(w["ffn_w_gate"], idx), _Sel(w["ffn_w_up"], idx), _Sel(w["ffn_w_down"], idx)


def _pool_layers(h, b, s, pool_prev, pos0, w, n_a):
    d = D_MODEL
    pool_rows = []
    for layer in range(n_a):
        g = _Sel(w["norm_g"], (layer,))
        h = _ffn(h, g, G_FFN1, *_ffn_w(w, layer, 0))
        h3, rows = _pool_mixer(h.reshape(b, s, d), pool_prev[layer], g, G_MIX, _Sel(w["pool_w_in"], (layer,)),
                               _Sel(w["pool_w_grp"], (layer,)), _Sel(w["pool_scale"], (layer,)),
                               _Sel(w["pool_w_out"], (layer,)), pos0)
        h = h3.reshape(b * s, d)
        pool_rows.append(rows[:, POOL_HALO - POOL_BUF:])
        h = _ffn(h, g, G_FFN2, *_ffn_w(w, layer, 1))
    return h, jnp.stack(pool_rows)


def _prompt_trunk(x, w):
    b, s, d = x.shape
    t = b * s
    depth = w["norm_g"].shape[0]
    n_a = depth // 2
    assert s % (R16 * BAND) == 0 and s % TOKEN_TILE == 0
    empty = jnp.zeros((n_a, b, POOL_HALO, d), F32)
    h, pool_rows = _pool_layers(x.reshape(t, d), b, s, empty, 0, w, n_a)
    k32, v32, kn, vn, k4, v4, k16, v16 = _kv_streams(h, w["kv_norm"], w["w_k"], w["w_v"], b, s)
    kn, vn = kn.reshape(b, s, d), vn.reshape(b, s, d)
    for layer in range(n_a, depth):
        g = _Sel(w["norm_g"], (layer,))
        j = layer - n_a
        if layer == n_a:
            h = _ffn(h, g, G_FFN1, *_ffn_w(w, layer, 0), order_in="natural", order_out="by16", seq=s)
        else:
            h = _ffn(h, g, G_FFN1, *_ffn_w(w, layer, 0))
            h = h.reshape(b, R16, s // R16, d)
        q0, q1, q2 = _q_streams(h, g, G_MIX, _Sel(w["attn_w_q"], (j,)))
        groups = [_band_attn_group(q0, kn, vn, 0), _band_attn_group(q1, k4, v4, 1),
                  _band_attn_group(q2, k16, v16, 2)]
        h = _merge_out_proj(groups, h, g, G_MIX + 1, _Sel(w["attn_w_o"], (j,)))
        if layer == depth - 1:
            h = _ffn(h, g, G_FFN2, *_ffn_w(w, layer, 1), order_in="by16", order_out="natural", seq=s)
        else:
            h = _ffn(h.reshape(t, d), g, G_FFN2, *_ffn_w(w, layer, 1))
    buf = min(max(win for win, _ in DIL_GROUPS), s)
    k32, v32 = k32.reshape(b, s, d)[:, s - buf:], v32.reshape(b, s, d)[:, s - buf:]
    return h.reshape(b, s, d), pool_rows, k32, v32


def _sample_trunk(x, pool_prev, cache_k, cache_v, pos0, w):
    b, s, d = x.shape
    t = b * s
    depth = w["norm_g"].shape[0]
    n_a = depth // 2
    h, pool_rows = _pool_layers(x.reshape(t, d), b, s, pool_prev, pos0, w, n_a)
    k_new, v_new = _norm_proj(h, _Sel(w["kv_norm"]), 0, [_Sel(w["w_k"]), _Sel(w["w_v"])],
                              out_cfg=((0, F32, 1.0), (1, F32, 1.0)))
    k_new, v_new = k_new.reshape(b, s, d), v_new.reshape(b, s, d)
    for layer in range(n_a, depth):
        g = _Sel(w["norm_g"], (layer,))
        j = layer - n_a
        h = _ffn(h, g, G_FFN1, *_ffn_w(w, layer, 0))
        (q,) = _norm_proj(h, g, G_MIX, [_Sel(w["attn_w_q"], (j,))], out_cfg=((0, BF16, ATTN_SCALE),))
        res = _step_attention(q.reshape(b, s, N_DIL * d), cache_k, cache_v, k_new, v_new, write_cache=(j == 0))
        if j == 0:
            new_k, new_v = res[1], res[2]
        h = _out_proj(res[0].reshape(t, d), h, g, G_MIX + 1, _Sel(w["attn_w_o"], (j,)))
        h = _ffn(h, g, G_FFN2, *_ffn_w(w, layer, 1))
    return h.reshape(b, s, d), pool_rows, new_k, new_v


def kernel(x_prompt, x_sample, state_pool, cache_k, cache_v, norm_g, ffn_w_gate, ffn_w_up, ffn_w_down,
           pool_w_in, pool_w_grp, pool_scale, pool_w_out, kv_norm, w_k, w_v, attn_w_q, attn_w_o):
    d = D_MODEL
    w = dict(
        norm_g=norm_g,
        ffn_w_gate=ffn_w_gate.astype(BF16), ffn_w_up=ffn_w_up.astype(BF16), ffn_w_down=ffn_w_down.astype(BF16),
        pool_w_in=pool_w_in.astype(BF16), pool_w_grp=pool_w_grp.astype(BF16),
        pool_scale=pool_scale.reshape(-1, 1, d), pool_w_out=pool_w_out.astype(BF16),
        kv_norm=kv_norm.reshape(1, d), w_k=w_k.astype(BF16), w_v=w_v.astype(BF16),
        attn_w_q=attn_w_q.astype(BF16), attn_w_o=attn_w_o.astype(BF16),
    )
    bs, n_past = cache_k.shape[:2]
    y_p, pool_p, k_p, v_p = _prompt_trunk(x_prompt, w)

    prev_s = jnp.pad(state_pool, ((0, 0), (0, 0), (POOL_HALO - POOL_BUF, 0), (0, 0)))
    y_s, pool_s, k_s, v_s = _sample_trunk(x_sample, prev_s, cache_k.reshape(bs, n_past, d),
                                          cache_v.reshape(bs, n_past, d), PAST_LEN, w)
    hd = (N_HEADS, HEAD_DIM)
    return (y_p, y_s, pool_p,
            k_p.reshape(k_p.shape[:2] + hd), v_p.reshape(v_p.shape[:2] + hd),
            pool_s,
            k_s.reshape(k_s.shape[:2] + hd), v_s.reshape(v_s.shape[:2] + hd))
```

```python
import functools
import math
from typing import NamedTuple

import jax
import jax.numpy as jnp
from jax import lax
from jax.experimental import pallas as pl
from jax.experimental.pallas import tpu as pltpu

D_MODEL = 1024
D_FF = 2816
HEAD_DIM = 64
N_HEADS = D_MODEL // HEAD_DIM
POOL_WINDOWS = (2, 4, 8, 16)
POOL_GROUP_DIM = D_MODEL // len(POOL_WINDOWS)
POOL_BUF = max(POOL_WINDOWS) - 1
POOL_HALO = 16
POOL_TOP = 32
DIL_GROUPS = ((128, 1), (512, 4), (2048, 16))
N_DIL = len(DIL_GROUPS)
BAND = 128
ATTN_SUBS = 2
PAST_LEN = 8192
RMS_EPS = 1e-6
NEG_INF = -1e30
ATTN_SCALE = HEAD_DIM ** -0.5
LOG2E = 1.4426950408889634
LN2 = 0.6931471805599453
LANES = 128
N_SLABS = D_MODEL // LANES
TOKEN_TILE = 512
FFN_TILE = 1024
FFN_SPLIT_ROWS = 256
R16 = 16
ROWS16 = TOKEN_TILE // R16
VMEM_LIMIT_BYTES = 56 * 1024 * 1024

F32 = jnp.float32
BF16 = jnp.bfloat16


def _params(n_axes):
    return pltpu.CompilerParams(dimension_semantics=("arbitrary",) * n_axes,
                                vmem_limit_bytes=VMEM_LIMIT_BYTES)


def _resident(shape):
    zeros = (0,) * len(shape)
    return pl.BlockSpec(shape, lambda *_: zeros, pipeline_mode=pl.Buffered(1))


class _Sel(NamedTuple):
    arr: jax.Array
    idx: tuple = ()


def _member(sel):
    lead = len(sel.idx)
    shape = sel.arr.shape[lead:]
    index = tuple(sel.idx) + (0,) * len(shape)
    return pl.BlockSpec((None,) * lead + shape, lambda *_: index, pipeline_mode=pl.Buffered(1))


def _rms(x, g):
    return x * lax.rsqrt(jnp.mean(x * x, axis=-1, keepdims=True) + RMS_EPS) * g


def _dot(a, b):
    return jnp.dot(a, b, preferred_element_type=F32)


def _dot_nt(a, b):
    return lax.dot_general(a, b, (((1,), (1,)), ((), ())), preferred_element_type=F32)


def _token_tile(t, largest=TOKEN_TILE):
    for tm in (largest, TOKEN_TILE, 256, 128, 64, 32, 16, 8):
        if t % tm == 0:
            return tm
    raise ValueError(f"token count {t} is not a multiple of 8")


SLAB_SCRATCH = pltpu.VMEM((N_SLABS, TOKEN_TILE, LANES), F32)


def _slabs_put(scr, val):
    for j in range(val.shape[-1] // LANES):
        scr[j, 0:val.shape[0], :] = val[:, j * LANES:(j + 1) * LANES]


def _slabs_get(scr, n_slabs, n_rows=TOKEN_TILE):
    return jnp.concatenate([scr[j, 0:n_rows, :] for j in range(n_slabs)], axis=-1)


def _slabs_gather(scr, n_slabs, start, n, stride):
    return jnp.concatenate([scr[j, pl.ds(start, n, stride=stride), :] for j in range(n_slabs)], axis=-1)


def _slabs_scatter(scr, start, n, stride, val):
    for j in range(val.shape[-1] // LANES):
        scr[j, pl.ds(start, n, stride=stride), :] = val[:, j * LANES:(j + 1) * LANES]


def _natural_to_by16(scr, val):
    n_slabs, n = val.shape[-1] // LANES, val.shape[0] // R16
    _slabs_put(scr, val)
    return [_slabs_gather(scr, n_slabs, r, n, R16) for r in range(R16)]


def _by16_to_natural(scr, val):
    n = val.shape[0] // R16
    for r in range(R16):
        _slabs_scatter(scr, r, n, R16, val[r * n:(r + 1) * n])
    return _slabs_get(scr, val.shape[-1] // LANES, val.shape[0])


def _by4_to_by16(scr, val):
    n_slabs = val.shape[-1] // LANES
    _slabs_put(scr, val)
    per_r4 = TOKEN_TILE // 4
    parts = [None] * R16
    for r4 in range(4):
        for c in range(4):
            parts[4 * c + r4] = _slabs_gather(scr, n_slabs, r4 * per_r4 + c, ROWS16, 4)
    return jnp.concatenate(parts, axis=0)


def _by16_to_by4(scr, val):
    per_r4 = TOKEN_TILE // 4
    for r4 in range(4):
        for c in range(4):
            r16 = 4 * c + r4
            _slabs_scatter(scr, r4 * per_r4 + c, ROWS16, 4, val[r16 * ROWS16:(r16 + 1) * ROWS16])
    return _slabs_get(scr, val.shape[-1] // LANES)


def _tile16_spec(b_tiles, c):
    return pl.BlockSpec((1, R16, ROWS16, c), lambda i: (i // b_tiles, 0, i % b_tiles, 0))


def _tile4_spec(b_tiles, c):
    return pl.BlockSpec((1, 4, TOKEN_TILE // 4, c), lambda i: (i // b_tiles, 0, i % b_tiles, 0))


def _ffn_body(x_ref, g_ref, wg_ref, wu_ref, wd_ref, o_ref, *scratch, g_row, order_in, order_out):
    d = D_MODEL

    def half_step(x):
        xn = _rms(x, g_ref[g_row:g_row + 1, :]).astype(BF16)
        hg = _dot(xn, wg_ref[...])
        hu = _dot(xn, wu_ref[...])
        a = (hg / (1.0 + jnp.exp(-hg)) * hu).astype(BF16)
        f = _dot(a, wd_ref[...])
        return x + 0.5 * _rms(f, g_ref[g_row + 1:g_row + 2, :])

    n = math.prod(x_ref.shape) // d
    rows = FFN_SPLIT_ROWS if n % FFN_SPLIT_ROWS == 0 else n
    per_res = rows // R16
    for k in range(n // rows):
        if order_in == "by16":
            x = x_ref[0, :, k * per_res:(k + 1) * per_res, :].reshape(rows, d)
        else:
            x = x_ref[k * rows:(k + 1) * rows, :]
        y = half_step(x)
        if order_out == "by16" and order_in == "by16":
            o_ref[0, :, k * per_res:(k + 1) * per_res, :] = y.reshape(R16, per_res, d)
        elif order_out == "by16":
            for r, part in enumerate(_natural_to_by16(scratch[0], y)):
                o_ref[0, r, k * per_res:(k + 1) * per_res, :] = part
        elif order_in == "by16":
            o_ref[k * rows:(k + 1) * rows, :] = _by16_to_natural(scratch[0], y)
        else:
            o_ref[k * rows:(k + 1) * rows, :] = y


def _ffn(x, g, g_row, wg, wu, wd, order_in="natural", order_out="natural", seq=None):
    d = D_MODEL
    regroup = order_in != order_out
    if regroup:
        b_tiles = seq // TOKEN_TILE
        t = x.size // d
        tm = TOKEN_TILE
        nat = pl.BlockSpec((tm, d), lambda i: (i, 0))
        p16 = _tile16_spec(b_tiles, d)
        in_spec, out_spec = (nat, p16) if order_out == "by16" else (p16, nat)
        out_shape = (t // seq, R16, seq // R16, d) if order_out == "by16" else (t, d)
        scratch = [SLAB_SCRATCH]
    else:
        t = x.shape[0]
        tm = _token_tile(t, FFN_TILE)
        in_spec = out_spec = pl.BlockSpec((tm, d), lambda i: (i, 0))
        out_shape = (t, d)
        scratch = []
    return pl.pallas_call(
        functools.partial(_ffn_body, g_row=g_row, order_in=order_in, order_out=order_out),
        grid=(t // tm,),
        in_specs=[in_spec, _member(g), _member(wg), _member(wu), _member(wd)],
        out_specs=out_spec,
        out_shape=jax.ShapeDtypeStruct(out_shape, F32),
        scratch_shapes=scratch,
        compiler_params=_params(1),
        name="ffn",
    )(x, g.arr, wg.arr, wu.arr, wd.arr)


def _proj_body(x_ref, g_ref, *refs, g_row, out_cfg):
    n_w = max(wi for wi, _, _ in out_cfg) + 1
    w_refs, o_refs = refs[:n_w], refs[n_w:]
    xn = _rms(x_ref[...], g_ref[g_row:g_row + 1, :]).astype(BF16)
    ys = [_dot(xn, w[...]) for w in w_refs]
    for o_ref, (wi, _, scale) in zip(o_refs, out_cfg):
        y = ys[wi] if scale == 1.0 else ys[wi] * scale
        o_ref[...] = y.astype(o_ref.dtype)


def _norm_proj(x, g, g_row, ws, out_cfg):
    t, d = x.shape
    tm = _token_tile(t)
    tile = pl.BlockSpec((tm, d), lambda i: (i, 0))
    n_out = [w.arr.shape[-1] for w in ws]
    return pl.pallas_call(
        functools.partial(_proj_body, g_row=g_row, out_cfg=out_cfg),
        grid=(t // tm,),
        in_specs=[tile, _member(g)] + [_member(w) for w in ws],
        out_specs=[pl.BlockSpec((tm, n_out[wi]), lambda i: (i, 0)) for wi, _, _ in out_cfg],
        out_shape=[jax.ShapeDtypeStruct((t, n_out[wi]), dt) for wi, dt, _ in out_cfg],
        compiler_params=_params(1),
        name="norm_proj",
    )(x, g.arr, *[w.arr for w in ws])


def _kv_streams_body(x_ref, g_ref, wk_ref, wv_ref, k32_ref, v32_ref, kn_ref, vn_ref, k4_ref, v4_ref,
                     k16_ref, v16_ref, scr):
    xn = _rms(x_ref[...], g_ref[...]).astype(BF16)
    per_r4 = TOKEN_TILE // 4
    for w_ref, o32, on, o4, o16 in ((wk_ref, k32_ref, kn_ref, k4_ref, k16_ref),
                                    (wv_ref, v32_ref, vn_ref, v4_ref, v16_ref)):
        y = _dot(xn, w_ref[...])
        o32[0] = y
        on[...] = y.astype(BF16)
        _slabs_put(scr, y)
        for r4 in range(4):
            o4[0, r4] = _slabs_gather(scr, N_SLABS, r4, per_r4, 4).astype(BF16)
        for r in range(R16):
            o16[0, r] = _slabs_gather(scr, N_SLABS, r, ROWS16, R16).astype(BF16)


def _kv_streams(x, g, wk, wv, b, s):
    t, d = x.shape
    b_tiles = s // TOKEN_TILE
    nat = pl.BlockSpec((TOKEN_TILE, d), lambda i: (i, 0))
    f32n = jax.ShapeDtypeStruct((b, s, d), F32)
    nat3 = pl.BlockSpec((1, TOKEN_TILE, d), lambda i: (i // b_tiles, i % b_tiles, 0))
    bfn = jax.ShapeDtypeStruct((t, d), BF16)
    bf4 = jax.ShapeDtypeStruct((b, 4, s // 4, d), BF16)
    bf16_ = jax.ShapeDtypeStruct((b, R16, s // R16, d), BF16)
    return pl.pallas_call(
        _kv_streams_body,
        grid=(t // TOKEN_TILE,),
        in_specs=[nat, _resident(g.shape), _resident(wk.shape), _resident(wv.shape)],
        out_specs=[nat3, nat3, nat, nat, _tile4_spec(b_tiles, d), _tile4_spec(b_tiles, d),
                   _tile16_spec(b_tiles, d), _tile16_spec(b_tiles, d)],
        out_shape=[f32n, f32n, bfn, bfn, bf4, bf4, bf16_, bf16_],
        scratch_shapes=[SLAB_SCRATCH],
        compiler_params=_params(1),
        name="kv_streams",
    )(x, g, wk, wv)


def _q_streams_body(x_ref, g_ref, w_ref, q0_ref, q1_ref, q2_ref, scr, *, g_row):
    d = D_MODEL
    xn = _rms(x_ref[0].reshape(TOKEN_TILE, d), g_ref[g_row:g_row + 1, :]).astype(BF16)
    scale = ATTN_SCALE * LOG2E
    y0 = _dot(xn, w_ref[:, 0:d]) * scale
    q0_ref[0] = _by16_to_natural(scr, y0).astype(BF16)
    y1 = _dot(xn, w_ref[:, d:2 * d]) * scale
    q1_ref[0] = _by16_to_by4(scr, y1).astype(BF16).reshape(4, TOKEN_TILE // 4, d)
    y2 = _dot(xn, w_ref[:, 2 * d:3 * d]) * scale
    q2_ref[0] = y2.astype(BF16).reshape(R16, ROWS16, d)


def _q_streams(h16, g, g_row, wq):
    b, _, n16, d = h16.shape
    s = n16 * R16
    b_tiles = s // TOKEN_TILE
    return pl.pallas_call(
        functools.partial(_q_streams_body, g_row=g_row),
        grid=(b * b_tiles,),
        in_specs=[_tile16_spec(b_tiles, d), _member(g), _member(wq)],
        out_specs=[pl.BlockSpec((1, TOKEN_TILE, d), lambda i: (i // b_tiles, i % b_tiles, 0)),
                   _tile4_spec(b_tiles, d), _tile16_spec(b_tiles, d)],
        out_shape=[jax.ShapeDtypeStruct((b, s, d), BF16), jax.ShapeDtypeStruct((b, 4, s // 4, d), BF16),
                   jax.ShapeDtypeStruct((b, R16, n16, d), BF16)],
        scratch_shapes=[SLAB_SCRATCH],
        compiler_params=_params(1),
        name="q_streams",
    )(h16, g.arr, wq.arr)


def _oproj_body(o_ref, h_ref, g_ref, w_ref, out_ref, *, g_row):
    y = _dot(o_ref[...], w_ref[...])
    out_ref[...] = h_ref[...] + _rms(y, g_ref[g_row:g_row + 1, :])


def _out_proj(o, h, g, g_row, w):
    t, d = h.shape
    tm = _token_tile(t)
    tile = pl.BlockSpec((tm, d), lambda i: (i, 0))
    return pl.pallas_call(
        functools.partial(_oproj_body, g_row=g_row),
        grid=(t // tm,),
        in_specs=[tile, tile, _member(g), _member(w)],
        out_specs=tile,
        out_shape=jax.ShapeDtypeStruct((t, d), F32),
        compiler_params=_params(1),
        name="out_proj",
    )(o, h, g.arr, w.arr)


def _head_stat_lane(h):
    return (h % 2) * HEAD_DIM + h // 2


def _stat_expander():
    col_head = jnp.arange(D_MODEL) // HEAD_DIM
    row = jnp.arange(LANES)
    row_head = jnp.where(row % HEAD_DIM < N_HEADS // 2, 2 * (row % HEAD_DIM) + row // HEAD_DIM, -1)
    e = (row_head[:, None] == col_head[None, :]).astype(BF16)
    return jnp.concatenate([e, e], axis=0)


def _merge_oproj_body(o0_ref, l0_ref, o1_ref, l1_ref, o2_ref, l2_ref, h_ref, g_ref, w_ref, e_ref, out_ref, scr, *,
                      g_row):
    d = D_MODEL
    o0 = jnp.concatenate(_natural_to_by16(scr, o0_ref[0]), axis=0)
    l0 = jnp.concatenate(_natural_to_by16(scr, l0_ref[0]), axis=0)
    o1 = _by4_to_by16(scr, o1_ref[0].reshape(TOKEN_TILE, d))
    l1 = _by4_to_by16(scr, l1_ref[0].reshape(TOKEN_TILE, LANES))
    o2 = o2_ref[0].reshape(TOKEN_TILE, d)
    l2 = l2_ref[0].reshape(TOKEN_TILE, LANES)
    mm = jnp.maximum(jnp.maximum(l0, l1), l2)
    es = [jnp.exp(l - mm) for l in (l0, l1, l2)]
    den = es[0] + es[1] + es[2]
    o = None
    for e, og in zip(es, (o0, o1, o2)):
        wgt = e / den
        hi = wgt.astype(BF16)
        lo = (wgt - hi.astype(F32)).astype(BF16)
        term = _dot(jnp.concatenate([hi, lo], axis=1), e_ref[...]) * og
        o = term if o is None else o + term
    y = _dot(o.astype(BF16), w_ref[...])
    out_ref[0] = (h_ref[0].reshape(TOKEN_TILE, d) + _rms(y, g_ref[g_row:g_row + 1, :])).reshape(R16, ROWS16, d)


def _merge_out_proj(groups, h16, g, g_row, w):
    b, _, n16, d = h16.shape
    s = n16 * R16
    b_tiles = s // TOKEN_TILE
    (o0, l0), (o1, l1), (o2, l2) = groups
    expander = _stat_expander()

    def specs(c):
        return (pl.BlockSpec((1, TOKEN_TILE, c), lambda i: (i // b_tiles, i % b_tiles, 0)),
                _tile4_spec(b_tiles, c), _tile16_spec(b_tiles, c))

    (nat, p4, p16), (nat_s, p4_s, p16_s) = specs(d), specs(LANES)
    return pl.pallas_call(
        functools.partial(_merge_oproj_body, g_row=g_row),
        grid=(b * b_tiles,),
        in_specs=[nat, nat_s, p4, p4_s, p16, p16_s, p16, _member(g), _member(w), _resident(expander.shape)],
        out_specs=p16,
        out_shape=jax.ShapeDtypeStruct(h16.shape, F32),
        scratch_shapes=[SLAB_SCRATCH],
        compiler_params=_params(1),
        name="merge_out_proj",
    )(o0, l0, o1, l1, o2, l2, h16, g.arr, w.arr, expander)


def _pool_body(h_ref, prev_ref, g_ref, win_ref, wgrp_ref, scale_ref, wout_ref,
               out_ref, rows_ref, ext_ref, sa_ref, sb_ref, *, g_row, pos0, bb, ts):
    d = D_MODEL
    j = pl.program_id(1)
    top, end = POOL_TOP, POOL_TOP + ts

    @pl.when(j == 0)
    def _():
        ext_ref[:, 0:top - POOL_HALO, :] = jnp.zeros((bb, top - POOL_HALO, d), F32)
        ext_ref[:, top - POOL_HALO:top, :] = prev_ref[...]

    @pl.when(j > 0)
    def _():
        ext_ref[:, 0:top, :] = ext_ref[:, ts:end, :]

    h = h_ref[...].reshape(bb * ts, d)
    hn = _rms(h, g_ref[g_row:g_row + 1, :]).astype(BF16)
    u = _dot(hn, win_ref[...])
    ext_ref[:, top:end, :] = u.reshape(bb, ts, d)
    rows_ref[...] = ext_ref[:, end - POOL_HALO:end, :]

    def doubled(ref, cs, lo, shift):
        return ref[:, lo:end, cs] + ref[:, lo - shift:end - shift, cs]

    every = slice(None)
    pos = pos0 + j * ts + lax.broadcasted_iota(jnp.int32, (1, ts, 1), 1)
    zs = []
    for gi, win in enumerate(POOL_WINDOWS):
        cs = slice(gi * POOL_GROUP_DIM, (gi + 1) * POOL_GROUP_DIM)
        n_stage = win.bit_length() - 1
        src, src_cs = ext_ref, cs
        stage_refs = (sa_ref, sb_ref)
        for k in range(n_stage - 1):
            lo = top - 8 * (n_stage - 1 - k)
            dst = stage_refs[k % 2]
            dst[:, lo:end, :] = doubled(src, src_cs, lo, 1 << k)
            src, src_cs = dst, every
        tot = doubled(src, src_cs, top, win // 2)
        cur = ext_ref[:, top:end, cs]
        cnt = jnp.minimum(pos + 1, win).astype(F32)
        z = (tot / cnt - cur).reshape(bb * ts, POOL_GROUP_DIM).astype(BF16)
        zs.append(_dot(z, wgrp_ref[gi]))
    zc = (jnp.concatenate(zs, axis=-1) * scale_ref[...]).astype(BF16)
    y = _dot(zc, wout_ref[...])
    out_ref[...] = (h + _rms(y, g_ref[g_row + 1:g_row + 2, :])).reshape(bb, ts, d)


def _pool_mixer(h, prev, g, g_row, w_in, w_grp, scale, w_out, pos0):
    b, s, d = h.shape
    if s >= TOKEN_TILE:
        bb, ts = 1, TOKEN_TILE
    else:
        bb, ts = b, s
    n_t = s // ts
    assert s % ts == 0 and ts % 8 == 0 and (n_t == 1 or ts >= POOL_TOP)
    tile = pl.BlockSpec((bb, ts, d), lambda i, j: (i, j, 0))
    halo = pl.BlockSpec((bb, POOL_HALO, d), lambda i, j: (i, 0, 0))
    stage = pltpu.VMEM((bb, POOL_TOP + ts, POOL_GROUP_DIM), F32)
    return pl.pallas_call(
        functools.partial(_pool_body, g_row=g_row, pos0=pos0, bb=bb, ts=ts),
        grid=(b // bb, n_t),
        in_specs=[tile, halo, _member(g), _member(w_in), _member(w_grp), _member(scale), _member(w_out)],
        out_specs=[tile, halo],
        out_shape=[jax.ShapeDtypeStruct((b, s, d), F32),
                   jax.ShapeDtypeStruct((b, POOL_HALO, d), F32)],
        scratch_shapes=[pltpu.VMEM((bb, POOL_TOP + ts, d), F32), stage, stage],
        compiler_params=_params(2),
        name="pool_mixer",
    )(h, prev, g.arr, w_in.arr, w_grp.arr, scale.arr, w_out.arr)


def _alibi_slopes(n):
    return 2.0 ** (-8.0 * jnp.arange(1, n + 1, dtype=F32) / n)


def _band_table(dil, n_steps):
    qi = jnp.arange(BAND)[:, None]
    ki = jnp.arange(2 * BAND)[None, :]
    step = qi + BAND - ki
    valid = (step >= 0) & (step <= n_steps)
    bias = -_alibi_slopes(N_HEADS)[:, None, None] * (step * dil).astype(F32)[None]
    return jnp.where(valid[None], bias, NEG_INF)


def _pair_tables(dil, n_steps):
    tab = _band_table(dil, n_steps) * LOG2E
    both = tab.reshape(N_HEADS // 2, 2, BAND, 2 * BAND)
    with_prev = jnp.concatenate([both[:, 0], both[:, 1]], axis=-1)
    cur_only = jnp.concatenate([both[:, 0, :, BAND:], both[:, 1, :, BAND:]], axis=-1)
    return with_prev, cur_only


def _step_table(n_past, t_new, n_keys):
    dist = n_past + jnp.arange(t_new)[:, None] - jnp.arange(n_keys)[None, :]
    in_buf = (jnp.arange(n_keys) < n_past + t_new)[None, :]
    bias = -_alibi_slopes(N_HEADS)[:, None, None] * dist.astype(F32)[None]
    tabs = []
    for win, dil in DIL_GROUPS:
        valid = (dist >= 0) & (dist % dil == 0) & (dist <= win) & in_buf
        tabs.append(jnp.where(valid[None], bias, NEG_INF))
    return jnp.stack(tabs, axis=1).reshape(N_HEADS * N_DIL * t_new, n_keys)


def _band_attn_body(*refs, chained):
    if chained:
        q_ref, kc_ref, vc_ref, kp_ref, vp_ref, tab2_ref, tab1_ref, o_ref, l_ref, s_ref, p_ref = refs
    else:
        q_ref, kc_ref, vc_ref, tab1_ref, o_ref, l_ref, s_ref, p_ref = refs
        kp_ref = vp_ref = tab2_ref = None
    n_pairs = N_HEADS // 2
    lead = (0,) * (len(q_ref.shape) - (2 if chained else 3))

    def rows(ref, sub):
        view = ref.at[lead]
        return view.at[pl.ds(sub * BAND, BAND)] if chained else view.at[sub]

    def attend(sub, k_prev, v_prev):
        q_v, kc_v, vc_v, o_v, l_v = (rows(r, sub) for r in (q_ref, kc_ref, vc_ref, o_ref, l_ref))
        s_v, p_v = s_ref.at[sub], p_ref.at[sub]
        with_prev = k_prev is not None
        nk = 2 * BAND if with_prev else BAND
        low = lax.broadcasted_iota(jnp.int32, (nk, LANES), 1) < HEAD_DIM
        row_low = lax.broadcasted_iota(jnp.int32, (2 * nk, LANES), 0) < nk
        lane_low = lax.broadcasted_iota(jnp.int32, (2 * nk, LANES), 1) < HEAD_DIM
        ones_st = jnp.where(row_low == lane_low, 1.0, 0.0).astype(BF16)
        lane = lax.broadcasted_iota(jnp.int32, (BAND, LANES), 1)
        q_low = lane < HEAD_DIM
        tab_ref = tab2_ref if with_prev else tab1_ref

        def stacked(c_v, p_v_, cs):
            x = c_v[:, cs]
            if with_prev:
                x = jnp.concatenate([p_v_[:, cs], x], axis=0)
            z = jnp.zeros_like(x)
            return jnp.concatenate([jnp.where(low, x, z), jnp.where(low, z, x)], axis=0)

        for j in range(n_pairs):
            cs = slice(j * LANES, (j + 1) * LANES)
            s_v[:, 2 * nk * j:2 * nk * (j + 1)] = _dot_nt(q_v[:, cs], stacked(kc_v, k_prev, cs)) + tab_ref[j]
        ms = []
        for h in range(N_HEADS):
            sh = s_v[:, nk * h:nk * (h + 1)]
            m = jnp.max(sh, axis=-1, keepdims=True)
            p_v[:, nk * h:nk * (h + 1)] = jnp.exp2(sh - m).astype(BF16)
            ms.append(m)
        stats = jnp.zeros((BAND, LANES), F32)
        for j in range(n_pairs):
            cs = slice(j * LANES, (j + 1) * LANES)
            aug = jnp.concatenate([stacked(vc_v, v_prev, cs), ones_st], axis=1)
            r = _dot(p_v[:, 2 * nk * j:2 * nk * (j + 1)], aug)
            l = r[:, LANES:]
            o_v[:, cs] = r[:, :LANES] / l
            m2 = jnp.where(q_low, jnp.broadcast_to(ms[2 * j], (BAND, LANES)),
                           jnp.broadcast_to(ms[2 * j + 1], (BAND, LANES)))
            stats = jnp.where((lane & (HEAD_DIM - 1)) == j, (m2 + jnp.log2(l)) * LN2, stats)
        l_v[...] = stats

    if chained:
        first = pl.program_id(2) == 0
        kp_v, vp_v = kp_ref.at[lead], vp_ref.at[lead]
        pl.when(first)(lambda: attend(0, None, None))
        pl.when(jnp.logical_not(first))(lambda: attend(0, kp_v, vp_v))
        for sub in range(1, ATTN_SUBS):
            attend(sub, rows(kc_ref, sub - 1), rows(vc_ref, sub - 1))
    else:
        for sub in range(ATTN_SUBS):
            attend(sub, None, None)


def _band_attn_group(q, kb, vb, gi):
    win, dil = DIL_GROUPS[gi]
    d = D_MODEL
    n = q.shape[-2]
    b = q.shape[0]
    chained = n > BAND
    step_rows = ATTN_SUBS * BAND

    def spec(c):
        if dil == 1:
            return (pl.BlockSpec((1, step_rows, c), lambda i, r, j: (i, j, 0)),
                    pl.BlockSpec((1, BAND, c), lambda i, r, j: (i, jnp.maximum(ATTN_SUBS * j - 1, 0), 0)))
        if chained:
            return (pl.BlockSpec((1, 1, step_rows, c), lambda i, r, j: (i, r, j, 0)),
                    pl.BlockSpec((1, 1, BAND, c), lambda i, r, j: (i, r, jnp.maximum(ATTN_SUBS * j - 1, 0), 0)))
        return pl.BlockSpec((1, ATTN_SUBS, BAND, c), lambda i, r, j: (i, r, 0, 0)), None

    if chained:
        assert n % step_rows == 0
        grid = (b, dil, n // step_rows)
    else:
        assert n == BAND and dil % ATTN_SUBS == 0
        grid = (b, dil // ATTN_SUBS, 1)
    cur, prv = spec(d)
    stat, _ = spec(LANES)
    tab2, tab1 = _pair_tables(dil, win // dil)
    in_specs = [cur, cur, cur]
    args = [q, kb, vb]
    if chained:
        in_specs += [prv, prv, _resident(tab2.shape)]
        args += [kb, vb, tab2]
    in_specs.append(_resident(tab1.shape))
    args.append(tab1)
    width = N_HEADS * 2 * BAND
    return pl.pallas_call(
        functools.partial(_band_attn_body, chained=chained),
        grid=grid,
        in_specs=in_specs,
        out_specs=[cur, stat],
        out_shape=[jax.ShapeDtypeStruct(q.shape, F32), jax.ShapeDtypeStruct(q.shape[:-1] + (LANES,), F32)],
        scratch_shapes=[pltpu.VMEM((ATTN_SUBS, BAND, width), F32), pltpu.VMEM((ATTN_SUBS, BAND, width), BF16)],
        compiler_params=_params(3),
        name=f"band_attn_g{gi}",
    )(*args)


STEP_HEADS = 4
STEP_COLS = STEP_HEADS * HEAD_DIM


def _step_attn_body(*refs, n_past, t_new, n_keys, write_cache):
    q_refs = refs[:N_DIL]
    ck_ref, cv_ref, kn_ref, vn_ref, tab_ref = refs[N_DIL:N_DIL + 5]
    refs = refs[N_DIL + 5:]
    o_ref = refs[0]
    if write_cache:
        ok_ref, ov_ref = refs[1:3]
        refs = refs[3:]
    else:
        refs = refs[1:]
    kb_ref, vb_ref = refs
    rows_g = N_DIL * t_new
    rows = STEP_HEADS * rows_g

    for c_ref, n_ref, b_ref in ((ck_ref, kn_ref, kb_ref), (cv_ref, vn_ref, vb_ref)):
        b_ref[0:n_past, :] = c_ref[0].astype(BF16)
        b_ref[n_past:n_past + t_new, :] = n_ref[0].astype(BF16)
        b_ref[n_past + t_new:n_keys, :] = jnp.zeros((n_keys - n_past - t_new, STEP_COLS), BF16)
    if write_cache:
        for c_ref, n_ref, out_ref in ((ck_ref, kn_ref, ok_ref), (cv_ref, vn_ref, ov_ref)):
            out_ref[0, 0:n_past - t_new, :] = c_ref[0, t_new:n_past, :]
            out_ref[0, n_past - t_new:n_past, :] = n_ref[0]

    q_rows = jnp.concatenate([r[0] for r in q_refs], axis=0)
    q_all = jnp.concatenate([q_rows] * STEP_HEADS, axis=0)
    row_head = lax.broadcasted_iota(jnp.int32, (rows, STEP_COLS), 0) // rows_g
    col_head = lax.broadcasted_iota(jnp.int32, (rows, STEP_COLS), 1) // HEAD_DIM
    own = row_head == col_head
    q_bd = jnp.where(own, q_all, jnp.zeros_like(q_all))
    s = _dot_nt(q_bd, kb_ref[...]) + tab_ref[...]
    m = jnp.max(s, axis=-1, keepdims=True)
    p = jnp.exp(s - m)
    l = jnp.sum(p, axis=-1, keepdims=True)
    o_full = _dot(p.astype(BF16), vb_ref[...]) / l
    lse = m + jnp.log(l)

    out = jnp.zeros((t_new, STEP_COLS), F32)
    col = lax.broadcasted_iota(jnp.int32, (t_new, STEP_COLS), 1) // HEAD_DIM
    for hh in range(STEP_HEADS):
        base = hh * rows_g
        lses = [lse[base + g * t_new:base + (g + 1) * t_new] for g in range(N_DIL)]
        mm = functools.reduce(jnp.maximum, lses)
        ws = [jnp.exp(x - mm) for x in lses]
        den = functools.reduce(lambda a, b: a + b, ws)
        acc = functools.reduce(
            lambda a, b: a + b,
            [ws[g] * o_full[base + g * t_new:base + (g + 1) * t_new] for g in range(N_DIL)])
        out = jnp.where(col == hh, acc / den, out)
    o_ref[0] = out.astype(o_ref.dtype)


def _step_attention(q, cache_k, cache_v, k_new, v_new, write_cache):
    b, t_new, _ = q.shape
    n_past = cache_k.shape[1]
    d = D_MODEL
    assert t_new % 8 == 0 and n_past % 8 == 0
    n_keys = -(-(n_past + t_new) // LANES) * LANES
    rows = STEP_HEADS * N_DIL * t_new
    tab = _step_table(n_past, t_new, n_keys)
    n_hg = N_HEADS // STEP_HEADS

    def q_spec(g):
        return pl.BlockSpec((1, t_new, STEP_COLS), lambda i, j: (i, 0, g * n_hg + j))

    cache_spec = pl.BlockSpec((1, n_past, STEP_COLS), lambda i, j: (i, 0, j))
    new_spec = pl.BlockSpec((1, t_new, STEP_COLS), lambda i, j: (i, 0, j))
    out_specs = [new_spec]
    out_shape = [jax.ShapeDtypeStruct((b, t_new, d), BF16)]
    if write_cache:
        out_specs += [cache_spec, cache_spec]
        out_shape += [jax.ShapeDtypeStruct((b, n_past, d), F32)] * 2
    return pl.pallas_call(
        functools.partial(_step_attn_body, n_past=n_past, t_new=t_new, n_keys=n_keys,
                          write_cache=write_cache),
        grid=(b, n_hg),
        in_specs=[q_spec(g) for g in range(N_DIL)]
        + [cache_spec, cache_spec, new_spec, new_spec,
           pl.BlockSpec((rows, n_keys), lambda i, j: (j, 0))],
        out_specs=out_specs,
        out_shape=out_shape,
        scratch_shapes=[pltpu.VMEM((n_keys, STEP_COLS), BF16)] * 2,
        compiler_params=_params(2),
        name="step_attn",
    )(q, q, q, cache_k, cache_v, k_new, v_new, tab)


G_FFN1, G_MIX, G_FFN2 = 0, 2, 4


def _ffn_w(w, layer, half):
    idx = (layer, half)
    return _Sel(w["ffn_w_gate"], idx), _Sel(w["ffn_w_up"], idx), _Sel(w["ffn_w_down"], idx)


def _pool_layers(h, b, s, pool_prev, pos0, w, n_a):
    d = D_MODEL
    pool_rows = []
    for layer in range(n_a):
        g = _Sel(w["norm_g"], (layer,))
        h = _ffn(h, g, G_FFN1, *_ffn_w(w, layer, 0))
        h3, rows = _pool_mixer(h.reshape(b, s, d), pool_prev[layer], g, G_MIX, _Sel(w["pool_w_in"], (layer,)),
                               _Sel(w["pool_w_grp"], (layer,)), _Sel(w["pool_scale"], (layer,)),
                               _Sel(w["pool_w_out"], (layer,)), pos0)
        h = h3.reshape(b * s, d)
        pool_rows.append(rows[:, POOL_HALO - POOL_BUF:])
        h = _ffn(h, g, G_FFN2, *_ffn_w(w, layer, 1))
    return h, jnp.stack(pool_rows)


def _prompt_trunk(x, w):
    b, s, d = x.shape
    t = b * s
    depth = w["norm_g"].shape[0]
    n_a = depth // 2
    assert s % (R16 * BAND) == 0 and s % TOKEN_TILE == 0
    empty = jnp.zeros((n_a, b, POOL_HALO, d), F32)
    h, pool_rows = _pool_layers(x.reshape(t, d), b, s, empty, 0, w, n_a)
    k32, v32, kn, vn, k4, v4, k16, v16 = _kv_streams(h, w["kv_norm"], w["w_k"], w["w_v"], b, s)
    kn, vn = kn.reshape(b, s, d), vn.reshape(b, s, d)
    for layer in range(n_a, depth):
        g = _Sel(w["norm_g"], (layer,))
        j = layer - n_a
        if layer == n_a:
            h = _ffn(h, g, G_FFN1, *_ffn_w(w, layer, 0), order_in="natural", order_out="by16", seq=s)
        else:
            h = _ffn(h, g, G_FFN1, *_ffn_w(w, layer, 0))
            h = h.reshape(b, R16, s // R16, d)
        q0, q1, q2 = _q_streams(h, g, G_MIX, _Sel(w["attn_w_q"], (j,)))
        groups = [_band_attn_group(q0, kn, vn, 0), _band_attn_group(q1, k4, v4, 1),
                  _band_attn_group(q2, k16, v16, 2)]
        h = _merge_out_proj(groups, h, g, G_MIX + 1, _Sel(w["attn_w_o"], (j,)))
        if layer == depth - 1:
            h = _ffn(h, g, G_FFN2, *_ffn_w(w, layer, 1), order_in="by16", order_out="natural", seq=s)
        else:
            h = _ffn(h.reshape(t, d), g, G_FFN2, *_ffn_w(w, layer, 1))
    buf = min(max(win for win, _ in DIL_GROUPS), s)
    k32, v32 = k32[:, s - buf:], v32[:, s - buf:]
    return h.reshape(b, s, d), pool_rows, k32, v32


def _sample_trunk(x, pool_prev, cache_k, cache_v, pos0, w):
    b, s, d = x.shape
    t = b * s
    depth = w["norm_g"].shape[0]
    n_a = depth // 2
    h, pool_rows = _pool_layers(x.reshape(t, d), b, s, pool_prev, pos0, w, n_a)
    k_new, v_new = _norm_proj(h, _Sel(w["kv_norm"]), 0, [_Sel(w["w_k"]), _Sel(w["w_v"])],
                              out_cfg=((0, F32, 1.0), (1, F32, 1.0)))
    k_new, v_new = k_new.reshape(b, s, d), v_new.reshape(b, s, d)
    for layer in range(n_a, depth):
        g = _Sel(w["norm_g"], (layer,))
        j = layer - n_a
        h = _ffn(h, g, G_FFN1, *_ffn_w(w, layer, 0))
        (q,) = _norm_proj(h, g, G_MIX, [_Sel(w["attn_w_q"], (j,))], out_cfg=((0, BF16, ATTN_SCALE),))
        res = _step_attention(q.reshape(b, s, N_DIL * d), cache_k, cache_v, k_new, v_new, write_cache=(j == 0))
        if j == 0:
            new_k, new_v = res[1], res[2]
        h = _out_proj(res[0].reshape(t, d), h, g, G_MIX + 1, _Sel(w["attn_w_o"], (j,)))
        h = _ffn(h, g, G_FFN2, *_ffn_w(w, layer, 1))
    return h.reshape(b, s, d), pool_rows, new_k, new_v


def kernel(x_prompt, x_sample, state_pool, cache_k, cache_v, norm_g, ffn_w_gate, ffn_w_up, ffn_w_down,
           pool_w_in, pool_w_grp, pool_scale, pool_w_out, kv_norm, w_k, w_v, attn_w_q, attn_w_o):
    d = D_MODEL
    w = dict(
        norm_g=norm_g,
        ffn_w_gate=ffn_w_gate.astype(BF16), ffn_w_up=ffn_w_up.astype(BF16), ffn_w_down=ffn_w_down.astype(BF16),
        pool_w_in=pool_w_in.astype(BF16), pool_w_grp=pool_w_grp.astype(BF16),
        pool_scale=pool_scale.reshape(-1, 1, d), pool_w_out=pool_w_out.astype(BF16),
        kv_norm=kv_norm.reshape(1, d), w_k=w_k.astype(BF16), w_v=w_v.astype(BF16),
        attn_w_q=attn_w_q.astype(BF16), attn_w_o=attn_w_o.astype(BF16),
    )
    bs, n_past = cache_k.shape[:2]
    y_p, pool_p, k_p, v_p = _prompt_trunk(x_prompt, w)

    prev_s = jnp.pad(state_pool, ((0, 0), (0, 0), (POOL_HALO - POOL_BUF, 0), (0, 0)))
    y_s, pool_s, k_s, v_s = _sample_trunk(x_sample, prev_s, cache_k.reshape(bs, n_past, d),
                                          cache_v.reshape(bs, n_past, d), PAST_LEN, w)
    hd = (N_HEADS, HEAD_DIM)
    return (y_p, y_s, pool_p,
            k_p.reshape(k_p.shape[:2] + hd), v_p.reshape(v_p.shape[:2] + hd),
            pool_s,
            k_s.reshape(k_s.shape[:2] + hd), v_s.reshape(v_s.shape[:2] + hd))
```

```python
import functools
import math
from typing import NamedTuple

import jax
import jax.numpy as jnp
from jax import lax
from jax.experimental import pallas as pl
from jax.experimental.pallas import tpu as pltpu

D_MODEL = 1024
D_FF = 2816
HEAD_DIM = 64
N_HEADS = D_MODEL // HEAD_DIM
POOL_WINDOWS = (2, 4, 8, 16)
POOL_GROUP_DIM = D_MODEL // len(POOL_WINDOWS)
POOL_BUF = max(POOL_WINDOWS) - 1
POOL_HALO = 16
POOL_TOP = 32
DIL_GROUPS = ((128, 1), (512, 4), (2048, 16))
N_DIL = len(DIL_GROUPS)
BAND = 128
ATTN_SUBS = 2
PAST_LEN = 8192
RMS_EPS = 1e-6
NEG_INF = -1e30
ATTN_SCALE = HEAD_DIM ** -0.5
LOG2E = 1.4426950408889634
LN2 = 0.6931471805599453
LANES = 128
N_SLABS = D_MODEL // LANES
TOKEN_TILE = 512
FFN_TILE = 512
FFN_SPLIT_ROWS = 256
R16 = 16
ROWS16 = TOKEN_TILE // R16
VMEM_LIMIT_BYTES = 56 * 1024 * 1024

F32 = jnp.float32
BF16 = jnp.bfloat16


def _params(n_axes):
    return pltpu.CompilerParams(dimension_semantics=("arbitrary",) * n_axes,
                                vmem_limit_bytes=VMEM_LIMIT_BYTES)


def _resident(shape):
    zeros = (0,) * len(shape)
    return pl.BlockSpec(shape, lambda *_: zeros, pipeline_mode=pl.Buffered(1))


class _Sel(NamedTuple):
    arr: jax.Array
    idx: tuple = ()


def _member(sel):
    lead = len(sel.idx)
    shape = sel.arr.shape[lead:]
    index = tuple(sel.idx) + (0,) * len(shape)
    return pl.BlockSpec((None,) * lead + shape, lambda *_: index, pipeline_mode=pl.Buffered(1))


def _rms(x, g):
    return x * lax.rsqrt(jnp.mean(x * x, axis=-1, keepdims=True) + RMS_EPS) * g


def _dot(a, b):
    return jnp.dot(a, b, preferred_element_type=F32)


def _dot_nt(a, b):
    return lax.dot_general(a, b, (((1,), (1,)), ((), ())), preferred_element_type=F32)


def _token_tile(t, largest=TOKEN_TILE):
    for tm in (largest, TOKEN_TILE, 256, 128, 64, 32, 16, 8):
        if t % tm == 0:
            return tm
    raise ValueError(f"token count {t} is not a multiple of 8")


SLAB_SCRATCH = pltpu.VMEM((N_SLABS, TOKEN_TILE, LANES), F32)


def _slabs_put(scr, val):
    for j in range(val.shape[-1] // LANES):
        scr[j, 0:val.shape[0], :] = val[:, j * LANES:(j + 1) * LANES]


def _slabs_get(scr, n_slabs, n_rows=TOKEN_TILE):
    return jnp.concatenate([scr[j, 0:n_rows, :] for j in range(n_slabs)], axis=-1)


def _slabs_gather(scr, n_slabs, start, n, stride):
    return jnp.concatenate([scr[j, pl.ds(start, n, stride=stride), :] for j in range(n_slabs)], axis=-1)


def _slabs_scatter(scr, start, n, stride, val):
    for j in range(val.shape[-1] // LANES):
        scr[j, pl.ds(start, n, stride=stride), :] = val[:, j * LANES:(j + 1) * LANES]


def _natural_to_by16(scr, val):
    n_slabs, n = val.shape[-1] // LANES, val.shape[0] // R16
    _slabs_put(scr, val)
    return [_slabs_gather(scr, n_slabs, r, n, R16) for r in range(R16)]


def _by16_to_natural(scr, val):
    n = val.shape[0] // R16
    for r in range(R16):
        _slabs_scatter(scr, r, n, R16, val[r * n:(r + 1) * n])
    return _slabs_get(scr, val.shape[-1] // LANES, val.shape[0])


def _by4_to_by16(scr, val):
    n_slabs = val.shape[-1] // LANES
    _slabs_put(scr, val)
    per_r4 = TOKEN_TILE // 4
    parts = [None] * R16
    for r4 in range(4):
        for c in range(4):
            parts[4 * c + r4] = _slabs_gather(scr, n_slabs, r4 * per_r4 + c, ROWS16, 4)
    return jnp.concatenate(parts, axis=0)


def _by16_to_by4(scr, val):
    per_r4 = TOKEN_TILE // 4
    for r4 in range(4):
        for c in range(4):
            r16 = 4 * c + r4
            _slabs_scatter(scr, r4 * per_r4 + c, ROWS16, 4, val[r16 * ROWS16:(r16 + 1) * ROWS16])
    return _slabs_get(scr, val.shape[-1] // LANES)


def _tile16_spec(b_tiles, c):
    return pl.BlockSpec((1, R16, ROWS16, c), lambda i: (i // b_tiles, 0, i % b_tiles, 0))


def _tile4_spec(b_tiles, c):
    return pl.BlockSpec((1, 4, TOKEN_TILE // 4, c), lambda i: (i // b_tiles, 0, i % b_tiles, 0))


def _ffn_body(x_ref, g_ref, wg_ref, wu_ref, wd_ref, o_ref, *scratch, g_row, order_in, order_out):
    d = D_MODEL

    def half_step(x):
        xn = _rms(x, g_ref[g_row:g_row + 1, :]).astype(BF16)
        hg = _dot(xn, wg_ref[...])
        hu = _dot(xn, wu_ref[...])
        a = (hg / (1.0 + jnp.exp(-hg)) * hu).astype(BF16)
        f = _dot(a, wd_ref[...])
        return x + 0.5 * _rms(f, g_ref[g_row + 1:g_row + 2, :])

    n = math.prod(x_ref.shape) // d
    rows = FFN_SPLIT_ROWS if n % FFN_SPLIT_ROWS == 0 else n
    per_res = rows // R16
    for k in range(n // rows):
        if order_in == "by16":
            x = x_ref[0, :, k * per_res:(k + 1) * per_res, :].reshape(rows, d)
        else:
            x = x_ref[k * rows:(k + 1) * rows, :]
        y = half_step(x)
        if order_out == "by16" and order_in == "by16":
            o_ref[0, :, k * per_res:(k + 1) * per_res, :] = y.reshape(R16, per_res, d)
        elif order_out == "by16":
            for r, part in enumerate(_natural_to_by16(scratch[0], y)):
                o_ref[0, r, k * per_res:(k + 1) * per_res, :] = part
        elif order_in == "by16":
            o_ref[k * rows:(k + 1) * rows, :] = _by16_to_natural(scratch[0], y)
        else:
            o_ref[k * rows:(k + 1) * rows, :] = y


def _ffn(x, g, g_row, wg, wu, wd, order_in="natural", order_out="natural", seq=None):
    d = D_MODEL
    regroup = order_in != order_out
    if regroup:
        b_tiles = seq // TOKEN_TILE
        t = x.size // d
        tm = TOKEN_TILE
        nat = pl.BlockSpec((tm, d), lambda i: (i, 0))
        p16 = _tile16_spec(b_tiles, d)
        in_spec, out_spec = (nat, p16) if order_out == "by16" else (p16, nat)
        out_shape = (t // seq, R16, seq // R16, d) if order_out == "by16" else (t, d)
        scratch = [SLAB_SCRATCH]
    else:
        t = x.shape[0]
        tm = _token_tile(t, FFN_TILE)
        in_spec = out_spec = pl.BlockSpec((tm, d), lambda i: (i, 0))
        out_shape = (t, d)
        scratch = []
    return pl.pallas_call(
        functools.partial(_ffn_body, g_row=g_row, order_in=order_in, order_out=order_out),
        grid=(t // tm,),
        in_specs=[in_spec, _member(g), _member(wg), _member(wu), _member(wd)],
        out_specs=out_spec,
        out_shape=jax.ShapeDtypeStruct(out_shape, F32),
        scratch_shapes=scratch,
        compiler_params=_params(1),
        name="ffn",
    )(x, g.arr, wg.arr, wu.arr, wd.arr)


def _proj_body(x_ref, g_ref, *refs, g_row, out_cfg):
    n_w = max(wi for wi, _, _ in out_cfg) + 1
    w_refs, o_refs = refs[:n_w], refs[n_w:]
    xn = _rms(x_ref[...], g_ref[g_row:g_row + 1, :]).astype(BF16)
    ys = [_dot(xn, w[...]) for w in w_refs]
    for o_ref, (wi, _, scale) in zip(o_refs, out_cfg):
        y = ys[wi] if scale == 1.0 else ys[wi] * scale
        o_ref[...] = y.astype(o_ref.dtype)


def _norm_proj(x, g, g_row, ws, out_cfg):
    t, d = x.shape
    tm = _token_tile(t)
    tile = pl.BlockSpec((tm, d), lambda i: (i, 0))
    n_out = [w.arr.shape[-1] for w in ws]
    return pl.pallas_call(
        functools.partial(_proj_body, g_row=g_row, out_cfg=out_cfg),
        grid=(t // tm,),
        in_specs=[tile, _member(g)] + [_member(w) for w in ws],
        out_specs=[pl.BlockSpec((tm, n_out[wi]), lambda i: (i, 0)) for wi, _, _ in out_cfg],
        out_shape=[jax.ShapeDtypeStruct((t, n_out[wi]), dt) for wi, dt, _ in out_cfg],
        compiler_params=_params(1),
        name="norm_proj",
    )(x, g.arr, *[w.arr for w in ws])


def _kv_streams_body(x_ref, g_ref, wk_ref, wv_ref, k32_ref, v32_ref, kn_ref, vn_ref, k4_ref, v4_ref,
                     k16_ref, v16_ref, scr):
    xn = _rms(x_ref[...], g_ref[...]).astype(BF16)
    per_r4 = TOKEN_TILE // 4
    for w_ref, o32, on, o4, o16 in ((wk_ref, k32_ref, kn_ref, k4_ref, k16_ref),
                                    (wv_ref, v32_ref, vn_ref, v4_ref, v16_ref)):
        y = _dot(xn, w_ref[...])
        o32[0] = y
        on[...] = y.astype(BF16)
        _slabs_put(scr, y)
        for r4 in range(4):
            o4[0, r4] = _slabs_gather(scr, N_SLABS, r4, per_r4, 4).astype(BF16)
        for r in range(R16):
            o16[0, r] = _slabs_gather(scr, N_SLABS, r, ROWS16, R16).astype(BF16)


def _kv_streams(x, g, wk, wv, b, s):
    t, d = x.shape
    b_tiles = s // TOKEN_TILE
    nat = pl.BlockSpec((TOKEN_TILE, d), lambda i: (i, 0))
    f32n = jax.ShapeDtypeStruct((b, s, d), F32)
    nat3 = pl.BlockSpec((1, TOKEN_TILE, d), lambda i: (i // b_tiles, i % b_tiles, 0))
    bfn = jax.ShapeDtypeStruct((t, d), BF16)
    bf4 = jax.ShapeDtypeStruct((b, 4, s // 4, d), BF16)
    bf16_ = jax.ShapeDtypeStruct((b, R16, s // R16, d), BF16)
    return pl.pallas_call(
        _kv_streams_body,
        grid=(t // TOKEN_TILE,),
        in_specs=[nat, _resident(g.shape), _resident(wk.shape), _resident(wv.shape)],
        out_specs=[nat3, nat3, nat, nat, _tile4_spec(b_tiles, d), _tile4_spec(b_tiles, d),
                   _tile16_spec(b_tiles, d), _tile16_spec(b_tiles, d)],
        out_shape=[f32n, f32n, bfn, bfn, bf4, bf4, bf16_, bf16_],
        scratch_shapes=[SLAB_SCRATCH],
        compiler_params=_params(1),
        name="kv_streams",
    )(x, g, wk, wv)


def _q_streams_body(x_ref, g_ref, w_ref, q0_ref, q1_ref, q2_ref, scr, *, g_row):
    d = D_MODEL
    xn = _rms(x_ref[0].reshape(TOKEN_TILE, d), g_ref[g_row:g_row + 1, :]).astype(BF16)
    scale = ATTN_SCALE * LOG2E
    y0 = _dot(xn, w_ref[:, 0:d]) * scale
    q0_ref[0] = _by16_to_natural(scr, y0).astype(BF16)
    y1 = _dot(xn, w_ref[:, d:2 * d]) * scale
    q1_ref[0] = _by16_to_by4(scr, y1).astype(BF16).reshape(4, TOKEN_TILE // 4, d)
    y2 = _dot(xn, w_ref[:, 2 * d:3 * d]) * scale
    q2_ref[0] = y2.astype(BF16).reshape(R16, ROWS16, d)


def _q_streams(h16, g, g_row, wq):
    b, _, n16, d = h16.shape
    s = n16 * R16
    b_tiles = s // TOKEN_TILE
    return pl.pallas_call(
        functools.partial(_q_streams_body, g_row=g_row),
        grid=(b * b_tiles,),
        in_specs=[_tile16_spec(b_tiles, d), _member(g), _member(wq)],
        out_specs=[pl.BlockSpec((1, TOKEN_TILE, d), lambda i: (i // b_tiles, i % b_tiles, 0)),
                   _tile4_spec(b_tiles, d), _tile16_spec(b_tiles, d)],
        out_shape=[jax.ShapeDtypeStruct((b, s, d), BF16), jax.ShapeDtypeStruct((b, 4, s // 4, d), BF16),
                   jax.ShapeDtypeStruct((b, R16, n16, d), BF16)],
        scratch_shapes=[SLAB_SCRATCH],
        compiler_params=_params(1),
        name="q_streams",
    )(h16, g.arr, wq.arr)


def _oproj_body(o_ref, h_ref, g_ref, w_ref, out_ref, *, g_row):
    y = _dot(o_ref[...], w_ref[...])
    out_ref[...] = h_ref[...] + _rms(y, g_ref[g_row:g_row + 1, :])


def _out_proj(o, h, g, g_row, w):
    t, d = h.shape
    tm = _token_tile(t)
    tile = pl.BlockSpec((tm, d), lambda i: (i, 0))
    return pl.pallas_call(
        functools.partial(_oproj_body, g_row=g_row),
        grid=(t // tm,),
        in_specs=[tile, tile, _member(g), _member(w)],
        out_specs=tile,
        out_shape=jax.ShapeDtypeStruct((t, d), F32),
        compiler_params=_params(1),
        name="out_proj",
    )(o, h, g.arr, w.arr)


def _head_stat_lane(h):
    return (h % 2) * HEAD_DIM + h // 2


def _stat_expander():
    col_head = jnp.arange(D_MODEL) // HEAD_DIM
    row = jnp.arange(LANES)
    row_head = jnp.where(row % HEAD_DIM < N_HEADS // 2, 2 * (row % HEAD_DIM) + row // HEAD_DIM, -1)
    e = (row_head[:, None] == col_head[None, :]).astype(BF16)
    return jnp.concatenate([e, e], axis=0)


def _merge_oproj_body(o0_ref, l0_ref, o1_ref, l1_ref, o2_ref, l2_ref, h_ref, g_ref, w_ref, e_ref, out_ref, scr, *,
                      g_row):
    d = D_MODEL
    o0 = jnp.concatenate(_natural_to_by16(scr, o0_ref[0]), axis=0)
    l0 = jnp.concatenate(_natural_to_by16(scr, l0_ref[0]), axis=0)
    o1 = _by4_to_by16(scr, o1_ref[0].reshape(TOKEN_TILE, d))
    l1 = _by4_to_by16(scr, l1_ref[0].reshape(TOKEN_TILE, LANES))
    o2 = o2_ref[0].reshape(TOKEN_TILE, d)
    l2 = l2_ref[0].reshape(TOKEN_TILE, LANES)
    mm = jnp.maximum(jnp.maximum(l0, l1), l2)
    es = [jnp.exp(l - mm) for l in (l0, l1, l2)]
    den = es[0] + es[1] + es[2]
    o = None
    for e, og in zip(es, (o0, o1, o2)):
        wgt = e / den
        hi = wgt.astype(BF16)
        lo = (wgt - hi.astype(F32)).astype(BF16)
        term = _dot(jnp.concatenate([hi, lo], axis=1), e_ref[...]) * og
        o = term if o is None else o + term
    y = _dot(o.astype(BF16), w_ref[...])
    out_ref[0] = (h_ref[0].reshape(TOKEN_TILE, d) + _rms(y, g_ref[g_row:g_row + 1, :])).reshape(R16, ROWS16, d)


def _merge_out_proj(groups, h16, g, g_row, w):
    b, _, n16, d = h16.shape
    s = n16 * R16
    b_tiles = s // TOKEN_TILE
    (o0, l0), (o1, l1), (o2, l2) = groups
    expander = _stat_expander()

    def specs(c):
        return (pl.BlockSpec((1, TOKEN_TILE, c), lambda i: (i // b_tiles, i % b_tiles, 0)),
                _tile4_spec(b_tiles, c), _tile16_spec(b_tiles, c))

    (nat, p4, p16), (nat_s, p4_s, p16_s) = specs(d), specs(LANES)
    return pl.pallas_call(
        functools.partial(_merge_oproj_body, g_row=g_row),
        grid=(b * b_tiles,),
        in_specs=[nat, nat_s, p4, p4_s, p16, p16_s, p16, _member(g), _member(w), _resident(expander.shape)],
        out_specs=p16,
        out_shape=jax.ShapeDtypeStruct(h16.shape, F32),
        scratch_shapes=[SLAB_SCRATCH],
        compiler_params=_params(1),
        name="merge_out_proj",
    )(o0, l0, o1, l1, o2, l2, h16, g.arr, w.arr, expander)


def _pool_body(h_ref, prev_ref, g_ref, win_ref, wgrp_ref, scale_ref, wout_ref,
               out_ref, rows_ref, ext_ref, sa_ref, sb_ref, *, g_row, pos0, bb, ts):
    d = D_MODEL
    j = pl.program_id(1)
    top, end = POOL_TOP, POOL_TOP + ts

    @pl.when(j == 0)
    def _():
        ext_ref[:, 0:top - POOL_HALO, :] = jnp.zeros((bb, top - POOL_HALO, d), F32)
        ext_ref[:, top - POOL_HALO:top, :] = prev_ref[...]

    @pl.when(j > 0)
    def _():
        ext_ref[:, 0:top, :] = ext_ref[:, ts:end, :]

    h = h_ref[...].reshape(bb * ts, d)
    hn = _rms(h, g_ref[g_row:g_row + 1, :]).astype(BF16)
    u = _dot(hn, win_ref[...])
    ext_ref[:, top:end, :] = u.reshape(bb, ts, d)
    rows_ref[...] = ext_ref[:, end - POOL_HALO:end, :]

    def doubled(ref, cs, lo, shift):
        return ref[:, lo:end, cs] + ref[:, lo - shift:end - shift, cs]

    every = slice(None)
    pos = pos0 + j * ts + lax.broadcasted_iota(jnp.int32, (1, ts, 1), 1)
    zs = []
    for gi, win in enumerate(POOL_WINDOWS):
        cs = slice(gi * POOL_GROUP_DIM, (gi + 1) * POOL_GROUP_DIM)
        n_stage = win.bit_length() - 1
        src, src_cs = ext_ref, cs
        stage_refs = (sa_ref, sb_ref)
        for k in range(n_stage - 1):
            lo = top - 8 * (n_stage - 1 - k)
            dst = stage_refs[k % 2]
            dst[:, lo:end, :] = doubled(src, src_cs, lo, 1 << k)
            src, src_cs = dst, every
        tot = doubled(src, src_cs, top, win // 2)
        cur = ext_ref[:, top:end, cs]
        cnt = jnp.minimum(pos + 1, win).astype(F32)
        z = (tot / cnt - cur).reshape(bb * ts, POOL_GROUP_DIM).astype(BF16)
        zs.append(_dot(z, wgrp_ref[gi]))
    zc = (jnp.concatenate(zs, axis=-1) * scale_ref[...]).astype(BF16)
    y = _dot(zc, wout_ref[...])
    out_ref[...] = (h + _rms(y, g_ref[g_row + 1:g_row + 2, :])).reshape(bb, ts, d)


def _pool_mixer(h, prev, g, g_row, w_in, w_grp, scale, w_out, pos0):
    b, s, d = h.shape
    if s >= TOKEN_TILE:
        bb, ts = 1, TOKEN_TILE
    else:
        bb, ts = b, s
    n_t = s // ts
    assert s % ts == 0 and ts % 8 == 0 and (n_t == 1 or ts >= POOL_TOP)
    tile = pl.BlockSpec((bb, ts, d), lambda i, j: (i, j, 0))
    halo = pl.BlockSpec((bb, POOL_HALO, d), lambda i, j: (i, 0, 0))
    stage = pltpu.VMEM((bb, POOL_TOP + ts, POOL_GROUP_DIM), F32)
    return pl.pallas_call(
        functools.partial(_pool_body, g_row=g_row, pos0=pos0, bb=bb, ts=ts),
        grid=(b // bb, n_t),
        in_specs=[tile, halo, _member(g), _member(w_in), _member(w_grp), _member(scale), _member(w_out)],
        out_specs=[tile, halo],
        out_shape=[jax.ShapeDtypeStruct((b, s, d), F32),
                   jax.ShapeDtypeStruct((b, POOL_HALO, d), F32)],
        scratch_shapes=[pltpu.VMEM((bb, POOL_TOP + ts, d), F32), stage, stage],
        compiler_params=_params(2),
        name="pool_mixer",
    )(h, prev, g.arr, w_in.arr, w_grp.arr, scale.arr, w_out.arr)


def _alibi_slopes(n):
    return 2.0 ** (-8.0 * jnp.arange(1, n + 1, dtype=F32) / n)


def _band_table(dil, n_steps):
    qi = jnp.arange(BAND)[:, None]
    ki = jnp.arange(2 * BAND)[None, :]
    step = qi + BAND - ki
    valid = (step >= 0) & (step <= n_steps)
    bias = -_alibi_slopes(N_HEADS)[:, None, None] * (step * dil).astype(F32)[None]
    return jnp.where(valid[None], bias, NEG_INF)


def _pair_tables(dil, n_steps):
    tab = _band_table(dil, n_steps) * LOG2E
    both = tab.reshape(N_HEADS // 2, 2, BAND, 2 * BAND)
    with_prev = jnp.concatenate([both[:, 0], both[:, 1]], axis=-1)
    cur_only = jnp.concatenate([both[:, 0, :, BAND:], both[:, 1, :, BAND:]], axis=-1)
    return with_prev, cur_only


def _step_key_positions(n_past, t_new, n_keys):
    j = jnp.arange(n_keys)
    pos = jnp.where(j < n_past, j + t_new, -1)
    return jnp.where((j >= n_past) & (j < n_past + t_new), j - n_past, pos)


def _step_table(n_past, t_new, n_keys):
    pos = _step_key_positions(n_past, t_new, n_keys)
    dist = n_past + jnp.arange(t_new)[:, None] - pos[None, :]
    held = (pos >= 0)[None, :]
    bias = -_alibi_slopes(N_HEADS)[:, None, None] * dist.astype(F32)[None]
    tabs = []
    for win, dil in DIL_GROUPS:
        valid = (dist >= 0) & (dist % dil == 0) & (dist <= win) & held
        tabs.append(jnp.where(valid[None], bias, NEG_INF))
    return jnp.stack(tabs, axis=1).reshape(N_HEADS * N_DIL * t_new, n_keys)


def _band_attn_body(*refs, chained):
    if chained:
        q_ref, kc_ref, vc_ref, kp_ref, vp_ref, tab2_ref, tab1_ref, o_ref, l_ref, s_ref, p_ref = refs
    else:
        q_ref, kc_ref, vc_ref, tab1_ref, o_ref, l_ref, s_ref, p_ref = refs
        kp_ref = vp_ref = tab2_ref = None
    n_pairs = N_HEADS // 2
    lead = (0,) * (len(q_ref.shape) - (2 if chained else 3))

    def rows(ref, sub):
        view = ref.at[lead]
        return view.at[pl.ds(sub * BAND, BAND)] if chained else view.at[sub]

    def attend(sub, k_prev, v_prev):
        q_v, kc_v, vc_v, o_v, l_v = (rows(r, sub) for r in (q_ref, kc_ref, vc_ref, o_ref, l_ref))
        s_v, p_v = s_ref.at[sub], p_ref.at[sub]
        with_prev = k_prev is not None
        nk = 2 * BAND if with_prev else BAND
        low = lax.broadcasted_iota(jnp.int32, (nk, LANES), 1) < HEAD_DIM
        row_low = lax.broadcasted_iota(jnp.int32, (2 * nk, LANES), 0) < nk
        lane_low = lax.broadcasted_iota(jnp.int32, (2 * nk, LANES), 1) < HEAD_DIM
        ones_st = jnp.where(row_low == lane_low, 1.0, 0.0).astype(BF16)
        lane = lax.broadcasted_iota(jnp.int32, (BAND, LANES), 1)
        q_low = lane < HEAD_DIM
        tab_ref = tab2_ref if with_prev else tab1_ref

        def stacked(c_v, p_v_, cs):
            x = c_v[:, cs]
            if with_prev:
                x = jnp.concatenate([p_v_[:, cs], x], axis=0)
            z = jnp.zeros_like(x)
            return jnp.concatenate([jnp.where(low, x, z), jnp.where(low, z, x)], axis=0)

        for j in range(n_pairs):
            cs = slice(j * LANES, (j + 1) * LANES)
            s_v[:, 2 * nk * j:2 * nk * (j + 1)] = _dot_nt(q_v[:, cs], stacked(kc_v, k_prev, cs)) + tab_ref[j]
        ms = []
        for h in range(N_HEADS):
            sh = s_v[:, nk * h:nk * (h + 1)]
            m = jnp.max(sh, axis=-1, keepdims=True)
            p_v[:, nk * h:nk * (h + 1)] = jnp.exp2(sh - m).astype(BF16)
            ms.append(m)
        stats = jnp.zeros((BAND, LANES), F32)
        for j in range(n_pairs):
            cs = slice(j * LANES, (j + 1) * LANES)
            aug = jnp.concatenate([stacked(vc_v, v_prev, cs), ones_st], axis=1)
            r = _dot(p_v[:, 2 * nk * j:2 * nk * (j + 1)], aug)
            l = r[:, LANES:]
            o_v[:, cs] = r[:, :LANES] / l
            m2 = jnp.where(q_low, jnp.broadcast_to(ms[2 * j], (BAND, LANES)),
                           jnp.broadcast_to(ms[2 * j + 1], (BAND, LANES)))
            stats = jnp.where((lane & (HEAD_DIM - 1)) == j, (m2 + jnp.log2(l)) * LN2, stats)
        l_v[...] = stats

    if chained:
        first = pl.program_id(2) == 0
        kp_v, vp_v = kp_ref.at[lead], vp_ref.at[lead]
        pl.when(first)(lambda: attend(0, None, None))
        pl.when(jnp.logical_not(first))(lambda: attend(0, kp_v, vp_v))
        for sub in range(1, ATTN_SUBS):
            attend(sub, rows(kc_ref, sub - 1), rows(vc_ref, sub - 1))
    else:
        for sub in range(ATTN_SUBS):
            attend(sub, None, None)


def _band_attn_group(q, kb, vb, gi):
    win, dil = DIL_GROUPS[gi]
    d = D_MODEL
    n = q.shape[-2]
    b = q.shape[0]
    chained = n > BAND
    step_rows = ATTN_SUBS * BAND

    def spec(c):
        if dil == 1:
            return (pl.BlockSpec((1, step_rows, c), lambda i, r, j: (i, j, 0)),
                    pl.BlockSpec((1, BAND, c), lambda i, r, j: (i, jnp.maximum(ATTN_SUBS * j - 1, 0), 0)))
        if chained:
            return (pl.BlockSpec((1, 1, step_rows, c), lambda i, r, j: (i, r, j, 0)),
                    pl.BlockSpec((1, 1, BAND, c), lambda i, r, j: (i, r, jnp.maximum(ATTN_SUBS * j - 1, 0), 0)))
        return pl.BlockSpec((1, ATTN_SUBS, BAND, c), lambda i, r, j: (i, r, 0, 0)), None

    if chained:
        assert n % step_rows == 0
        grid = (b, dil, n // step_rows)
    else:
        assert n == BAND and dil % ATTN_SUBS == 0
        grid = (b, dil // ATTN_SUBS, 1)
    cur, prv = spec(d)
    stat, _ = spec(LANES)
    tab2, tab1 = _pair_tables(dil, win // dil)
    in_specs = [cur, cur, cur]
    args = [q, kb, vb]
    if chained:
        in_specs += [prv, prv, _resident(tab2.shape)]
        args += [kb, vb, tab2]
    in_specs.append(_resident(tab1.shape))
    args.append(tab1)
    width = N_HEADS * 2 * BAND
    return pl.pallas_call(
        functools.partial(_band_attn_body, chained=chained),
        grid=grid,
        in_specs=in_specs,
        out_specs=[cur, stat],
        out_shape=[jax.ShapeDtypeStruct(q.shape, F32), jax.ShapeDtypeStruct(q.shape[:-1] + (LANES,), F32)],
        scratch_shapes=[pltpu.VMEM((ATTN_SUBS, BAND, width), F32), pltpu.VMEM((ATTN_SUBS, BAND, width), BF16)],
        compiler_params=_params(3),
        name=f"band_attn_g{gi}",
    )(*args)


STEP_HEADS = 4
STEP_COLS = STEP_HEADS * HEAD_DIM


def _step_attn_body(*refs, n_past, t_new, n_keys, append):
    q_refs = refs[:N_DIL]
    if append:
        wk_ref, wv_ref, fk_ref, fv_ref, kn_ref, vn_ref, tab_ref, o_ref, ok_ref, ov_ref, kb_ref, vb_ref = refs[N_DIL:]
        news, outs = (kn_ref, vn_ref), (ok_ref, ov_ref)
    else:
        wk_ref, wv_ref, fk_ref, fv_ref, tab_ref, o_ref, kb_ref, vb_ref = refs[N_DIL:]
        news, outs = (None, None), (None, None)
    rows_g = N_DIL * t_new
    rows = STEP_HEADS * rows_g
    pack = 16
    n_old = n_past - t_new

    for w_ref, f_ref, n_ref, out_ref, b_ref in zip((wk_ref, wv_ref), (fk_ref, fv_ref), news, outs, (kb_ref, vb_ref)):
        if append:
            whole = n_old // pack * pack
            b_ref[0:whole, :] = w_ref[0, 0:whole, :].astype(BF16)
            b_ref[whole:n_past, :] = jnp.concatenate([w_ref[0, whole:n_old, :], n_ref[0]], axis=0).astype(BF16)
            out_ref[0, 0:n_old, :] = w_ref[0]
            out_ref[0, n_old:n_past, :] = n_ref[0]
        else:
            b_ref[0:n_past, :] = w_ref[0].astype(BF16)
        tail = jnp.concatenate([f_ref[0], jnp.zeros((n_keys - n_past - t_new, STEP_COLS), F32)], axis=0)
        b_ref[n_past:n_keys, :] = tail.astype(BF16)

    q_rows = jnp.concatenate([r[0] for r in q_refs], axis=0)
    q_all = jnp.concatenate([q_rows] * STEP_HEADS, axis=0)
    row_head = lax.broadcasted_iota(jnp.int32, (rows, STEP_COLS), 0) // rows_g
    col_head = lax.broadcasted_iota(jnp.int32, (rows, STEP_COLS), 1) // HEAD_DIM
    own = row_head == col_head
    q_bd = jnp.where(own, q_all, jnp.zeros_like(q_all))
    s = _dot_nt(q_bd, kb_ref[...]) + tab_ref[...]
    m = jnp.max(s, axis=-1, keepdims=True)
    p = jnp.exp(s - m)
    l = jnp.sum(p, axis=-1, keepdims=True)
    o_full = _dot(p.astype(BF16), vb_ref[...]) / l
    lse = m + jnp.log(l)

    out = jnp.zeros((t_new, STEP_COLS), F32)
    col = lax.broadcasted_iota(jnp.int32, (t_new, STEP_COLS), 1) // HEAD_DIM
    for hh in range(STEP_HEADS):
        base = hh * rows_g
        lses = [lse[base + g * t_new:base + (g + 1) * t_new] for g in range(N_DIL)]
        mm = functools.reduce(jnp.maximum, lses)
        ws = [jnp.exp(x - mm) for x in lses]
        den = functools.reduce(lambda a, b: a + b, ws)
        acc = functools.reduce(
            lambda a, b: a + b,
            [ws[g] * o_full[base + g * t_new:base + (g + 1) * t_new] for g in range(N_DIL)])
        out = jnp.where(col == hh, acc / den, out)
    o_ref[0] = out.astype(o_ref.dtype)


def _step_attention(q, win_k, win_v, first_k, first_v, new=None):
    b, t_new, _ = q.shape
    append = new is not None
    n_past = win_k.shape[1] + (t_new if append else 0)
    d = D_MODEL
    assert t_new % 8 == 0 and n_past % 16 == 0
    n_keys = -(-(n_past + t_new) // LANES) * LANES
    rows = STEP_HEADS * N_DIL * t_new
    tab = _step_table(n_past, t_new, n_keys)
    n_hg = N_HEADS // STEP_HEADS

    def q_spec(g):
        return pl.BlockSpec((1, t_new, STEP_COLS), lambda i, j: (i, 0, g * n_hg + j))

    def rows_spec(n):
        return pl.BlockSpec((1, n, STEP_COLS), lambda i, j: (i, 0, j))

    in_specs = [q_spec(g) for g in range(N_DIL)] + [rows_spec(win_k.shape[1])] * 2 + [rows_spec(t_new)] * 2
    args = [q, q, q, win_k, win_v, first_k, first_v]
    out_specs = [rows_spec(t_new)]
    out_shape = [jax.ShapeDtypeStruct((b, t_new, d), BF16)]
    if append:
        in_specs += [rows_spec(t_new)] * 2
        args += list(new)
        out_specs += [rows_spec(n_past)] * 2
        out_shape += [jax.ShapeDtypeStruct((b, n_past, d), F32)] * 2
    in_specs.append(pl.BlockSpec((rows, n_keys), lambda i, j: (j, 0)))
    args.append(tab)
    return pl.pallas_call(
        functools.partial(_step_attn_body, n_past=n_past, t_new=t_new, n_keys=n_keys, append=append),
        grid=(b, n_hg),
        in_specs=in_specs,
        out_specs=out_specs,
        out_shape=out_shape,
        scratch_shapes=[pltpu.VMEM((n_keys, STEP_COLS), BF16)] * 2,
        compiler_params=_params(2),
        name="step_attn",
    )(*args)


G_FFN1, G_MIX, G_FFN2 = 0, 2, 4


def _ffn_w(w, layer, half):
    idx = (layer, half)
    return _Sel(w["ffn_w_gate"], idx), _Sel(w["ffn_w_up"], idx), _Sel(w["ffn_w_down"], idx)


def _pool_layers(h, b, s, pool_prev, pos0, w, n_a):
    d = D_MODEL
    pool_rows = []
    for layer in range(n_a):
        g = _Sel(w["norm_g"], (layer,))
        h = _ffn(h, g, G_FFN1, *_ffn_w(w, layer, 0))
        h3, rows = _pool_mixer(h.reshape(b, s, d), pool_prev[layer], g, G_MIX, _Sel(w["pool_w_in"], (layer,)),
                               _Sel(w["pool_w_grp"], (layer,)), _Sel(w["pool_scale"], (layer,)),
                               _Sel(w["pool_w_out"], (layer,)), pos0)
        h = h3.reshape(b * s, d)
        pool_rows.append(rows[:, POOL_HALO - POOL_BUF:])
        h = _ffn(h, g, G_FFN2, *_ffn_w(w, layer, 1))
    return h, jnp.stack(pool_rows)


def _prompt_trunk(x, w):
    b, s, d = x.shape
    t = b * s
    depth = w["norm_g"].shape[0]
    n_a = depth // 2
    assert s % (R16 * BAND) == 0 and s % TOKEN_TILE == 0
    empty = jnp.zeros((n_a, b, POOL_HALO, d), F32)
    h, pool_rows = _pool_layers(x.reshape(t, d), b, s, empty, 0, w, n_a)
    k32, v32, kn, vn, k4, v4, k16, v16 = _kv_streams(h, w["kv_norm"], w["w_k"], w["w_v"], b, s)
    kn, vn = kn.reshape(b, s, d), vn.reshape(b, s, d)
    for layer in range(n_a, depth):
        g = _Sel(w["norm_g"], (layer,))
        j = layer - n_a
        if layer == n_a:
            h = _ffn(h, g, G_FFN1, *_ffn_w(w, layer, 0), order_in="natural", order_out="by16", seq=s)
        else:
            h = _ffn(h, g, G_FFN1, *_ffn_w(w, layer, 0))
            h = h.reshape(b, R16, s // R16, d)
        q0, q1, q2 = _q_streams(h, g, G_MIX, _Sel(w["attn_w_q"], (j,)))
        groups = [_band_attn_group(q0, kn, vn, 0), _band_attn_group(q1, k4, v4, 1),
                  _band_attn_group(q2, k16, v16, 2)]
        h = _merge_out_proj(groups, h, g, G_MIX + 1, _Sel(w["attn_w_o"], (j,)))
        if layer == depth - 1:
            h = _ffn(h, g, G_FFN2, *_ffn_w(w, layer, 1), order_in="by16", order_out="natural", seq=s)
        else:
            h = _ffn(h.reshape(t, d), g, G_FFN2, *_ffn_w(w, layer, 1))
    buf = min(max(win for win, _ in DIL_GROUPS), s)
    k32, v32 = k32[:, s - buf:], v32[:, s - buf:]
    return h.reshape(b, s, d), pool_rows, k32, v32


def _sample_trunk(x, pool_prev, cache_k, cache_v, pos0, w):
    b, s, d = x.shape
    t = b * s
    depth = w["norm_g"].shape[0]
    n_a = depth // 2
    h, pool_rows = _pool_layers(x.reshape(t, d), b, s, pool_prev, pos0, w, n_a)
    k_new, v_new = _norm_proj(h, _Sel(w["kv_norm"]), 0, [_Sel(w["w_k"]), _Sel(w["w_v"])],
                              out_cfg=((0, F32, 1.0), (1, F32, 1.0)))
    k_new, v_new = k_new.reshape(b, s, d), v_new.reshape(b, s, d)
    win_k, win_v = lax.optimization_barrier((cache_k[:, s:], cache_v[:, s:]))
    first_k, first_v = cache_k[:, :s], cache_v[:, :s]
    for layer in range(n_a, depth):
        g = _Sel(w["norm_g"], (layer,))
        j = layer - n_a
        h = _ffn(h, g, G_FFN1, *_ffn_w(w, layer, 0))
        (q,) = _norm_proj(h, g, G_MIX, [_Sel(w["attn_w_q"], (j,))], out_cfg=((0, BF16, ATTN_SCALE),))
        res = _step_attention(q.reshape(b, s, N_DIL * d), win_k, win_v, first_k, first_v,
                              new=(k_new, v_new) if j == 0 else None)
        if j == 0:
            win_k, win_v = res[1], res[2]
        h = _out_proj(res[0].reshape(t, d), h, g, G_MIX + 1, _Sel(w["attn_w_o"], (j,)))
        h = _ffn(h, g, G_FFN2, *_ffn_w(w, layer, 1))
    return h.reshape(b, s, d), pool_rows, win_k, win_v


def kernel(x_prompt, x_sample, state_pool, cache_k, cache_v, norm_g, ffn_w_gate, ffn_w_up, ffn_w_down,
           pool_w_in, pool_w_grp, pool_scale, pool_w_out, kv_norm, w_k, w_v, attn_w_q, attn_w_o):
    d = D_MODEL
    w = dict(
        norm_g=norm_g,
        ffn_w_gate=ffn_w_gate.astype(BF16), ffn_w_up=ffn_w_up.astype(BF16), ffn_w_down=ffn_w_down.astype(BF16),
        pool_w_in=pool_w_in.astype(BF16), pool_w_grp=pool_w_grp.astype(BF16),
        pool_scale=pool_scale.reshape(-1, 1, d), pool_w_out=pool_w_out.astype(BF16),
        kv_norm=kv_norm.reshape(1, d), w_k=w_k.astype(BF16), w_v=w_v.astype(BF16),
        attn_w_q=attn_w_q.astype(BF16), attn_w_o=attn_w_o.astype(BF16),
    )
    bs, n_past = cache_k.shape[:2]
    y_p, pool_p, k_p, v_p = _prompt_trunk(x_prompt, w)

    prev_s = jnp.pad(state_pool, ((0, 0), (0, 0), (POOL_HALO - POOL_BUF, 0), (0, 0)))
    y_s, pool_s, k_s, v_s = _sample_trunk(x_sample, prev_s, cache_k.reshape(bs, n_past, d),
                                          cache_v.reshape(bs, n_past, d), PAST_LEN, w)
    hd = (N_HEADS, HEAD_DIM)
    return (y_p, y_s, pool_p,
            k_p.reshape(k_p.shape[:2] + hd), v_p.reshape(v_p.shape[:2] + hd),
            pool_s,
            k_s.reshape(k_s.shape[:2] + hd), v_s.reshape(v_s.shape[:2] + hd))
```

```python
import functools
import math
from typing import NamedTuple

import jax
import jax.numpy as jnp
from jax import lax
from jax.experimental import pallas as pl
from jax.experimental.pallas import tpu as pltpu

D_MODEL = 1024
D_FF = 2816
HEAD_DIM = 64
N_HEADS = D_MODEL // HEAD_DIM
POOL_WINDOWS = (2, 4, 8, 16)
POOL_GROUP_DIM = D_MODEL // len(POOL_WINDOWS)
POOL_BUF = max(POOL_WINDOWS) - 1
POOL_HALO = 16
POOL_TOP = 32
DIL_GROUPS = ((128, 1), (512, 4), (2048, 16))
N_DIL = len(DIL_GROUPS)
BAND = 128
ATTN_SUBS = 2
PAST_LEN = 8192
RMS_EPS = 1e-6
NEG_INF = -1e30
ATTN_SCALE = HEAD_DIM ** -0.5
LOG2E = 1.4426950408889634
LN2 = 0.6931471805599453
LANES = 128
N_SLABS = D_MODEL // LANES
TOKEN_TILE = 512
FFN_TILE = 512
FFN_SPLIT_ROWS = 256
R16 = 16
ROWS16 = TOKEN_TILE // R16
VMEM_LIMIT_BYTES = 56 * 1024 * 1024

F32 = jnp.float32
BF16 = jnp.bfloat16


def _params(n_axes):
    return pltpu.CompilerParams(dimension_semantics=("arbitrary",) * n_axes,
                                vmem_limit_bytes=VMEM_LIMIT_BYTES)


def _resident(shape):
    zeros = (0,) * len(shape)
    return pl.BlockSpec(shape, lambda *_: zeros, pipeline_mode=pl.Buffered(1))


class _Sel(NamedTuple):
    arr: jax.Array
    idx: tuple = ()


def _member(sel):
    lead = len(sel.idx)
    shape = sel.arr.shape[lead:]
    index = tuple(sel.idx) + (0,) * len(shape)
    return pl.BlockSpec((None,) * lead + shape, lambda *_: index, pipeline_mode=pl.Buffered(1))


def _rms(x, g):
    return x * lax.rsqrt(jnp.mean(x * x, axis=-1, keepdims=True) + RMS_EPS) * g


def _dot(a, b):
    return jnp.dot(a, b, preferred_element_type=F32)


def _dot_nt(a, b):
    return lax.dot_general(a, b, (((1,), (1,)), ((), ())), preferred_element_type=F32)


def _token_tile(t, largest=TOKEN_TILE):
    for tm in (largest, TOKEN_TILE, 256, 128, 64, 32, 16, 8):
        if t % tm == 0:
            return tm
    raise ValueError(f"token count {t} is not a multiple of 8")


SLAB_SCRATCH = pltpu.VMEM((N_SLABS, TOKEN_TILE, LANES), F32)


def _slabs_put(scr, val):
    for j in range(val.shape[-1] // LANES):
        scr[j, 0:val.shape[0], :] = val[:, j * LANES:(j + 1) * LANES]


def _slabs_get(scr, n_slabs, n_rows=TOKEN_TILE):
    return jnp.concatenate([scr[j, 0:n_rows, :] for j in range(n_slabs)], axis=-1)


def _slabs_gather(scr, n_slabs, start, n, stride):
    return jnp.concatenate([scr[j, pl.ds(start, n, stride=stride), :] for j in range(n_slabs)], axis=-1)


def _slabs_scatter(scr, start, n, stride, val):
    for j in range(val.shape[-1] // LANES):
        scr[j, pl.ds(start, n, stride=stride), :] = val[:, j * LANES:(j + 1) * LANES]


def _natural_to_by16(scr, val):
    n_slabs, n = val.shape[-1] // LANES, val.shape[0] // R16
    _slabs_put(scr, val)
    return [_slabs_gather(scr, n_slabs, r, n, R16) for r in range(R16)]


def _by16_to_natural(scr, val):
    n = val.shape[0] // R16
    for r in range(R16):
        _slabs_scatter(scr, r, n, R16, val[r * n:(r + 1) * n])
    return _slabs_get(scr, val.shape[-1] // LANES, val.shape[0])


def _by4_to_by16(scr, val):
    n_slabs = val.shape[-1] // LANES
    _slabs_put(scr, val)
    per_r4 = TOKEN_TILE // 4
    parts = [None] * R16
    for r4 in range(4):
        for c in range(4):
            parts[4 * c + r4] = _slabs_gather(scr, n_slabs, r4 * per_r4 + c, ROWS16, 4)
    return jnp.concatenate(parts, axis=0)


def _by16_to_by4(scr, val):
    per_r4 = TOKEN_TILE // 4
    for r4 in range(4):
        for c in range(4):
            r16 = 4 * c + r4
            _slabs_scatter(scr, r4 * per_r4 + c, ROWS16, 4, val[r16 * ROWS16:(r16 + 1) * ROWS16])
    return _slabs_get(scr, val.shape[-1] // LANES)


def _tile16_spec(b_tiles, c):
    return pl.BlockSpec((1, R16, ROWS16, c), lambda i: (i // b_tiles, 0, i % b_tiles, 0))


def _tile4_spec(b_tiles, c):
    return pl.BlockSpec((1, 4, TOKEN_TILE // 4, c), lambda i: (i // b_tiles, 0, i % b_tiles, 0))


def _ffn_body(x_ref, g_ref, wg_ref, wu_ref, wd_ref, o_ref, *scratch, g_row, order_in, order_out):
    d = D_MODEL

    def half_step(x):
        xn = _rms(x, g_ref[g_row:g_row + 1, :]).astype(BF16)
        hg = _dot(xn, wg_ref[...])
        hu = _dot(xn, wu_ref[...])
        a = (hg / (1.0 + jnp.exp(-hg)) * hu).astype(BF16)
        f = _dot(a, wd_ref[...])
        return x + 0.5 * _rms(f, g_ref[g_row + 1:g_row + 2, :])

    n = math.prod(x_ref.shape) // d
    rows = FFN_SPLIT_ROWS if n % FFN_SPLIT_ROWS == 0 else n
    per_res = rows // R16
    for k in range(n // rows):
        if order_in == "by16":
            x = x_ref[0, :, k * per_res:(k + 1) * per_res, :].reshape(rows, d)
        else:
            x = x_ref[k * rows:(k + 1) * rows, :]
        y = half_step(x)
        if order_out == "by16" and order_in == "by16":
            o_ref[0, :, k * per_res:(k + 1) * per_res, :] = y.reshape(R16, per_res, d)
        elif order_out == "by16":
            for r, part in enumerate(_natural_to_by16(scratch[0], y)):
                o_ref[0, r, k * per_res:(k + 1) * per_res, :] = part
        elif order_in == "by16":
            o_ref[k * rows:(k + 1) * rows, :] = _by16_to_natural(scratch[0], y)
        else:
            o_ref[k * rows:(k + 1) * rows, :] = y


def _ffn(x, g, g_row, wg, wu, wd, order_in="natural", order_out="natural", seq=None):
    d = D_MODEL
    regroup = order_in != order_out
    if regroup:
        b_tiles = seq // TOKEN_TILE
        t = x.size // d
        tm = TOKEN_TILE
        nat = pl.BlockSpec((tm, d), lambda i: (i, 0))
        p16 = _tile16_spec(b_tiles, d)
        in_spec, out_spec = (nat, p16) if order_out == "by16" else (p16, nat)
        out_shape = (t // seq, R16, seq // R16, d) if order_out == "by16" else (t, d)
        scratch = [SLAB_SCRATCH]
    else:
        t = x.shape[0]
        tm = _token_tile(t, FFN_TILE)
        in_spec = out_spec = pl.BlockSpec((tm, d), lambda i: (i, 0))
        out_shape = (t, d)
        scratch = []
    return pl.pallas_call(
        functools.partial(_ffn_body, g_row=g_row, order_in=order_in, order_out=order_out),
        grid=(t // tm,),
        in_specs=[in_spec, _member(g), _member(wg), _member(wu), _member(wd)],
        out_specs=out_spec,
        out_shape=jax.ShapeDtypeStruct(out_shape, F32),
        scratch_shapes=scratch,
        compiler_params=_params(1),
        name="ffn",
    )(x, g.arr, wg.arr, wu.arr, wd.arr)


def _proj_body(x_ref, g_ref, *refs, g_row, out_cfg):
    n_w = max(wi for wi, _, _ in out_cfg) + 1
    w_refs, o_refs = refs[:n_w], refs[n_w:]
    xn = _rms(x_ref[...], g_ref[g_row:g_row + 1, :]).astype(BF16)
    ys = [_dot(xn, w[...]) for w in w_refs]
    for o_ref, (wi, _, scale) in zip(o_refs, out_cfg):
        y = ys[wi] if scale == 1.0 else ys[wi] * scale
        o_ref[...] = y.astype(o_ref.dtype)


def _norm_proj(x, g, g_row, ws, out_cfg):
    t, d = x.shape
    tm = _token_tile(t)
    tile = pl.BlockSpec((tm, d), lambda i: (i, 0))
    n_out = [w.arr.shape[-1] for w in ws]
    return pl.pallas_call(
        functools.partial(_proj_body, g_row=g_row, out_cfg=out_cfg),
        grid=(t // tm,),
        in_specs=[tile, _member(g)] + [_member(w) for w in ws],
        out_specs=[pl.BlockSpec((tm, n_out[wi]), lambda i: (i, 0)) for wi, _, _ in out_cfg],
        out_shape=[jax.ShapeDtypeStruct((t, n_out[wi]), dt) for wi, dt, _ in out_cfg],
        compiler_params=_params(1),
        name="norm_proj",
    )(x, g.arr, *[w.arr for w in ws])


def _kv_streams_body(x_ref, g_ref, wk_ref, wv_ref, k32_ref, v32_ref, kn_ref, vn_ref, k4_ref, v4_ref,
                     k16_ref, v16_ref, scr):
    xn = _rms(x_ref[...], g_ref[...]).astype(BF16)
    per_r4 = TOKEN_TILE // 4
    for w_ref, o32, on, o4, o16 in ((wk_ref, k32_ref, kn_ref, k4_ref, k16_ref),
                                    (wv_ref, v32_ref, vn_ref, v4_ref, v16_ref)):
        y = _dot(xn, w_ref[...])
        o32[0] = y
        on[...] = y.astype(BF16)
        _slabs_put(scr, y)
        for r4 in range(4):
            o4[0, r4] = _slabs_gather(scr, N_SLABS, r4, per_r4, 4).astype(BF16)
        for r in range(R16):
            o16[0, r] = _slabs_gather(scr, N_SLABS, r, ROWS16, R16).astype(BF16)


def _kv_streams(x, g, wk, wv, b, s):
    t, d = x.shape
    b_tiles = s // TOKEN_TILE
    nat = pl.BlockSpec((TOKEN_TILE, d), lambda i: (i, 0))
    f32n = jax.ShapeDtypeStruct((b, s, d), F32)
    nat3 = pl.BlockSpec((1, TOKEN_TILE, d), lambda i: (i // b_tiles, i % b_tiles, 0))
    bfn = jax.ShapeDtypeStruct((t, d), BF16)
    bf4 = jax.ShapeDtypeStruct((b, 4, s // 4, d), BF16)
    bf16_ = jax.ShapeDtypeStruct((b, R16, s // R16, d), BF16)
    return pl.pallas_call(
        _kv_streams_body,
        grid=(t // TOKEN_TILE,),
        in_specs=[nat, _resident(g.shape), _resident(wk.shape), _resident(wv.shape)],
        out_specs=[nat3, nat3, nat, nat, _tile4_spec(b_tiles, d), _tile4_spec(b_tiles, d),
                   _tile16_spec(b_tiles, d), _tile16_spec(b_tiles, d)],
        out_shape=[f32n, f32n, bfn, bfn, bf4, bf4, bf16_, bf16_],
        scratch_shapes=[SLAB_SCRATCH],
        compiler_params=_params(1),
        name="kv_streams",
    )(x, g, wk, wv)


def _q_streams_body(x_ref, g_ref, w_ref, q0_ref, q1_ref, q2_ref, scr, *, g_row):
    d = D_MODEL
    xn = _rms(x_ref[0].reshape(TOKEN_TILE, d), g_ref[g_row:g_row + 1, :]).astype(BF16)
    scale = ATTN_SCALE * LOG2E
    y0 = _dot(xn, w_ref[:, 0:d]) * scale
    q0_ref[0] = _by16_to_natural(scr, y0).astype(BF16)
    y1 = _dot(xn, w_ref[:, d:2 * d]) * scale
    q1_ref[0] = _by16_to_by4(scr, y1).astype(BF16).reshape(4, TOKEN_TILE // 4, d)
    y2 = _dot(xn, w_ref[:, 2 * d:3 * d]) * scale
    q2_ref[0] = y2.astype(BF16).reshape(R16, ROWS16, d)


def _q_streams(h16, g, g_row, wq):
    b, _, n16, d = h16.shape
    s = n16 * R16
    b_tiles = s // TOKEN_TILE
    return pl.pallas_call(
        functools.partial(_q_streams_body, g_row=g_row),
        grid=(b * b_tiles,),
        in_specs=[_tile16_spec(b_tiles, d), _member(g), _member(wq)],
        out_specs=[pl.BlockSpec((1, TOKEN_TILE, d), lambda i: (i // b_tiles, i % b_tiles, 0)),
                   _tile4_spec(b_tiles, d), _tile16_spec(b_tiles, d)],
        out_shape=[jax.ShapeDtypeStruct((b, s, d), BF16), jax.ShapeDtypeStruct((b, 4, s // 4, d), BF16),
                   jax.ShapeDtypeStruct((b, R16, n16, d), BF16)],
        scratch_shapes=[SLAB_SCRATCH],
        compiler_params=_params(1),
        name="q_streams",
    )(h16, g.arr, wq.arr)


def _oproj_body(o_ref, h_ref, g_ref, w_ref, out_ref, *, g_row):
    y = _dot(o_ref[...], w_ref[...])
    out_ref[...] = h_ref[...] + _rms(y, g_ref[g_row:g_row + 1, :])


def _out_proj(o, h, g, g_row, w):
    t, d = h.shape
    tm = _token_tile(t)
    tile = pl.BlockSpec((tm, d), lambda i: (i, 0))
    return pl.pallas_call(
        functools.partial(_oproj_body, g_row=g_row),
        grid=(t // tm,),
        in_specs=[tile, tile, _member(g), _member(w)],
        out_specs=tile,
        out_shape=jax.ShapeDtypeStruct((t, d), F32),
        compiler_params=_params(1),
        name="out_proj",
    )(o, h, g.arr, w.arr)


def _head_stat_lane(h):
    return (h % 2) * HEAD_DIM + h // 2


def _stat_expander():
    col_head = jnp.arange(D_MODEL) // HEAD_DIM
    row = jnp.arange(LANES)
    row_head = jnp.where(row % HEAD_DIM < N_HEADS // 2, 2 * (row % HEAD_DIM) + row // HEAD_DIM, -1)
    e = (row_head[:, None] == col_head[None, :]).astype(BF16)
    return jnp.concatenate([e, e], axis=0)


def _merge_oproj_body(o0_ref, l0_ref, o1_ref, l1_ref, o2_ref, l2_ref, h_ref, g_ref, w_ref, e_ref, out_ref, scr, *,
                      g_row):
    d = D_MODEL
    o0 = jnp.concatenate(_natural_to_by16(scr, o0_ref[0]), axis=0)
    l0 = jnp.concatenate(_natural_to_by16(scr, l0_ref[0]), axis=0)
    o1 = _by4_to_by16(scr, o1_ref[0].reshape(TOKEN_TILE, d))
    l1 = _by4_to_by16(scr, l1_ref[0].reshape(TOKEN_TILE, LANES))
    o2 = o2_ref[0].reshape(TOKEN_TILE, d)
    l2 = l2_ref[0].reshape(TOKEN_TILE, LANES)
    mm = jnp.maximum(jnp.maximum(l0, l1), l2)
    es = [jnp.exp(l - mm) for l in (l0, l1, l2)]
    den = es[0] + es[1] + es[2]
    o = None
    for e, og in zip(es, (o0, o1, o2)):
        wgt = e / den
        hi = wgt.astype(BF16)
        lo = (wgt - hi.astype(F32)).astype(BF16)
        term = _dot(jnp.concatenate([hi, lo], axis=1), e_ref[...]) * og
        o = term if o is None else o + term
    y = _dot(o.astype(BF16), w_ref[...])
    out_ref[0] = (h_ref[0].reshape(TOKEN_TILE, d) + _rms(y, g_ref[g_row:g_row + 1, :])).reshape(R16, ROWS16, d)


def _merge_out_proj(groups, h16, g, g_row, w):
    b, _, n16, d = h16.shape
    s = n16 * R16
    b_tiles = s // TOKEN_TILE
    (o0, l0), (o1, l1), (o2, l2) = groups
    expander = _stat_expander()

    def specs(c):
        return (pl.BlockSpec((1, TOKEN_TILE, c), lambda i: (i // b_tiles, i % b_tiles, 0)),
                _tile4_spec(b_tiles, c), _tile16_spec(b_tiles, c))

    (nat, p4, p16), (nat_s, p4_s, p16_s) = specs(d), specs(LANES)
    return pl.pallas_call(
        functools.partial(_merge_oproj_body, g_row=g_row),
        grid=(b * b_tiles,),
        in_specs=[nat, nat_s, p4, p4_s, p16, p16_s, p16, _member(g), _member(w), _resident(expander.shape)],
        out_specs=p16,
        out_shape=jax.ShapeDtypeStruct(h16.shape, F32),
        scratch_shapes=[SLAB_SCRATCH],
        compiler_params=_params(1),
        name="merge_out_proj",
    )(o0, l0, o1, l1, o2, l2, h16, g.arr, w.arr, expander)


def _pool_body(h_ref, prev_ref, g_ref, win_ref, wgrp_ref, scale_ref, wout_ref,
               out_ref, rows_ref, ext_ref, sa_ref, sb_ref, *, g_row, pos0, bb, ts):
    d = D_MODEL
    j = pl.program_id(1)
    top, end = POOL_TOP, POOL_TOP + ts

    @pl.when(j == 0)
    def _():
        ext_ref[:, 0:top - POOL_HALO, :] = jnp.zeros((bb, top - POOL_HALO, d), F32)
        ext_ref[:, top - POOL_HALO:top, :] = prev_ref[...]

    @pl.when(j > 0)
    def _():
        ext_ref[:, 0:top, :] = ext_ref[:, ts:end, :]

    h = h_ref[...].reshape(bb * ts, d)
    hn = _rms(h, g_ref[g_row:g_row + 1, :]).astype(BF16)
    u = _dot(hn, win_ref[...])
    ext_ref[:, top:end, :] = u.reshape(bb, ts, d)
    rows_ref[...] = ext_ref[:, end - POOL_HALO:end, :]

    def doubled(ref, cs, lo, shift):
        return ref[:, lo:end, cs] + ref[:, lo - shift:end - shift, cs]

    every = slice(None)
    pos = pos0 + j * ts + lax.broadcasted_iota(jnp.int32, (1, ts, 1), 1)
    zs = []
    for gi, win in enumerate(POOL_WINDOWS):
        cs = slice(gi * POOL_GROUP_DIM, (gi + 1) * POOL_GROUP_DIM)
        n_stage = win.bit_length() - 1
        src, src_cs = ext_ref, cs
        stage_refs = (sa_ref, sb_ref)
        for k in range(n_stage - 1):
            lo = top - 8 * (n_stage - 1 - k)
            dst = stage_refs[k % 2]
            dst[:, lo:end, :] = doubled(src, src_cs, lo, 1 << k)
            src, src_cs = dst, every
        tot = doubled(src, src_cs, top, win // 2)
        cur = ext_ref[:, top:end, cs]
        cnt = jnp.minimum(pos + 1, win).astype(F32)
        z = (tot / cnt - cur).reshape(bb * ts, POOL_GROUP_DIM).astype(BF16)
        zs.append(_dot(z, wgrp_ref[gi]))
    zc = (jnp.concatenate(zs, axis=-1) * scale_ref[...]).astype(BF16)
    y = _dot(zc, wout_ref[...])
    out_ref[...] = (h + _rms(y, g_ref[g_row + 1:g_row + 2, :])).reshape(bb, ts, d)


def _pool_mixer(h, prev, g, g_row, w_in, w_grp, scale, w_out, pos0):
    b, s, d = h.shape
    if s >= TOKEN_TILE:
        bb, ts = 1, TOKEN_TILE
    else:
        bb, ts = b, s
    n_t = s // ts
    assert s % ts == 0 and ts % 8 == 0 and (n_t == 1 or ts >= POOL_TOP)
    tile = pl.BlockSpec((bb, ts, d), lambda i, j: (i, j, 0))
    halo = pl.BlockSpec((bb, POOL_HALO, d), lambda i, j: (i, 0, 0))
    stage = pltpu.VMEM((bb, POOL_TOP + ts, POOL_GROUP_DIM), F32)
    return pl.pallas_call(
        functools.partial(_pool_body, g_row=g_row, pos0=pos0, bb=bb, ts=ts),
        grid=(b // bb, n_t),
        in_specs=[tile, halo, _member(g), _member(w_in), _member(w_grp), _member(scale), _member(w_out)],
        out_specs=[tile, halo],
        out_shape=[jax.ShapeDtypeStruct((b, s, d), F32),
                   jax.ShapeDtypeStruct((b, POOL_HALO, d), F32)],
        scratch_shapes=[pltpu.VMEM((bb, POOL_TOP + ts, d), F32), stage, stage],
        compiler_params=_params(2),
        name="pool_mixer",
    )(h, prev, g.arr, w_in.arr, w_grp.arr, scale.arr, w_out.arr)


def _alibi_slopes(n):
    return 2.0 ** (-8.0 * jnp.arange(1, n + 1, dtype=F32) / n)


def _band_table(dil, n_steps):
    qi = jnp.arange(BAND)[:, None]
    ki = jnp.arange(2 * BAND)[None, :]
    step = qi + BAND - ki
    valid = (step >= 0) & (step <= n_steps)
    bias = -_alibi_slopes(N_HEADS)[:, None, None] * (step * dil).astype(F32)[None]
    return jnp.where(valid[None], bias, NEG_INF)


def _pair_tables(dil, n_steps):
    tab = _band_table(dil, n_steps) * LOG2E
    both = tab.reshape(N_HEADS // 2, 2, BAND, 2 * BAND)
    with_prev = jnp.concatenate([both[:, 0], both[:, 1]], axis=-1)
    cur_only = jnp.concatenate([both[:, 0, :, BAND:], both[:, 1, :, BAND:]], axis=-1)
    return with_prev, cur_only


def _step_key_positions(n_past, t_new, n_keys, old_window):
    j = jnp.arange(n_keys)
    if old_window:
        return jnp.where(j < n_past + t_new, j, -1)
    pos = jnp.where(j < n_past, j + t_new, -1)
    return jnp.where((j >= n_past) & (j < n_past + t_new), j - n_past, pos)


def _step_table(n_past, t_new, n_keys, old_window):
    pos = _step_key_positions(n_past, t_new, n_keys, old_window)
    dist = n_past + jnp.arange(t_new)[:, None] - pos[None, :]
    held = (pos >= 0)[None, :]
    bias = -_alibi_slopes(N_HEADS)[:, None, None] * dist.astype(F32)[None]
    tabs = []
    for win, dil in DIL_GROUPS:
        valid = (dist >= 0) & (dist % dil == 0) & (dist <= win) & held
        tabs.append(jnp.where(valid[None], bias, NEG_INF))
    return jnp.stack(tabs, axis=1).reshape(N_HEADS * N_DIL * t_new, n_keys)


def _band_attn_body(*refs, chained):
    if chained:
        q_ref, kc_ref, vc_ref, kp_ref, vp_ref, tab2_ref, tab1_ref, o_ref, l_ref, s_ref, p_ref = refs
    else:
        q_ref, kc_ref, vc_ref, tab1_ref, o_ref, l_ref, s_ref, p_ref = refs
        kp_ref = vp_ref = tab2_ref = None
    n_pairs = N_HEADS // 2
    lead = (0,) * (len(q_ref.shape) - (2 if chained else 3))

    def rows(ref, sub):
        view = ref.at[lead]
        return view.at[pl.ds(sub * BAND, BAND)] if chained else view.at[sub]

    def attend(sub, k_prev, v_prev):
        q_v, kc_v, vc_v, o_v, l_v = (rows(r, sub) for r in (q_ref, kc_ref, vc_ref, o_ref, l_ref))
        s_v, p_v = s_ref.at[sub], p_ref.at[sub]
        with_prev = k_prev is not None
        nk = 2 * BAND if with_prev else BAND
        low = lax.broadcasted_iota(jnp.int32, (nk, LANES), 1) < HEAD_DIM
        row_low = lax.broadcasted_iota(jnp.int32, (2 * nk, LANES), 0) < nk
        lane_low = lax.broadcasted_iota(jnp.int32, (2 * nk, LANES), 1) < HEAD_DIM
        ones_st = jnp.where(row_low == lane_low, 1.0, 0.0).astype(BF16)
        lane = lax.broadcasted_iota(jnp.int32, (BAND, LANES), 1)
        q_low = lane < HEAD_DIM
        tab_ref = tab2_ref if with_prev else tab1_ref

        def stacked(c_v, p_v_, cs):
            x = c_v[:, cs]
            if with_prev:
                x = jnp.concatenate([p_v_[:, cs], x], axis=0)
            z = jnp.zeros_like(x)
            return jnp.concatenate([jnp.where(low, x, z), jnp.where(low, z, x)], axis=0)

        for j in range(n_pairs):
            cs = slice(j * LANES, (j + 1) * LANES)
            s_v[:, 2 * nk * j:2 * nk * (j + 1)] = _dot_nt(q_v[:, cs], stacked(kc_v, k_prev, cs)) + tab_ref[j]
        ms = []
        for h in range(N_HEADS):
            sh = s_v[:, nk * h:nk * (h + 1)]
            m = jnp.max(sh, axis=-1, keepdims=True)
            p_v[:, nk * h:nk * (h + 1)] = jnp.exp2(sh - m).astype(BF16)
            ms.append(m)
        stats = jnp.zeros((BAND, LANES), F32)
        for j in range(n_pairs):
            cs = slice(j * LANES, (j + 1) * LANES)
            aug = jnp.concatenate([stacked(vc_v, v_prev, cs), ones_st], axis=1)
            r = _dot(p_v[:, 2 * nk * j:2 * nk * (j + 1)], aug)
            l = r[:, LANES:]
            o_v[:, cs] = r[:, :LANES] / l
            m2 = jnp.where(q_low, jnp.broadcast_to(ms[2 * j], (BAND, LANES)),
                           jnp.broadcast_to(ms[2 * j + 1], (BAND, LANES)))
            stats = jnp.where((lane & (HEAD_DIM - 1)) == j, (m2 + jnp.log2(l)) * LN2, stats)
        l_v[...] = stats

    if chained:
        first = pl.program_id(2) == 0
        kp_v, vp_v = kp_ref.at[lead], vp_ref.at[lead]
        pl.when(first)(lambda: attend(0, None, None))
        pl.when(jnp.logical_not(first))(lambda: attend(0, kp_v, vp_v))
        for sub in range(1, ATTN_SUBS):
            attend(sub, rows(kc_ref, sub - 1), rows(vc_ref, sub - 1))
    else:
        for sub in range(ATTN_SUBS):
            attend(sub, None, None)


def _band_attn_group(q, kb, vb, gi):
    win, dil = DIL_GROUPS[gi]
    d = D_MODEL
    n = q.shape[-2]
    b = q.shape[0]
    chained = n > BAND
    step_rows = ATTN_SUBS * BAND

    def spec(c):
        if dil == 1:
            return (pl.BlockSpec((1, step_rows, c), lambda i, r, j: (i, j, 0)),
                    pl.BlockSpec((1, BAND, c), lambda i, r, j: (i, jnp.maximum(ATTN_SUBS * j - 1, 0), 0)))
        if chained:
            return (pl.BlockSpec((1, 1, step_rows, c), lambda i, r, j: (i, r, j, 0)),
                    pl.BlockSpec((1, 1, BAND, c), lambda i, r, j: (i, r, jnp.maximum(ATTN_SUBS * j - 1, 0), 0)))
        return pl.BlockSpec((1, ATTN_SUBS, BAND, c), lambda i, r, j: (i, r, 0, 0)), None

    if chained:
        assert n % step_rows == 0
        grid = (b, dil, n // step_rows)
    else:
        assert n == BAND and dil % ATTN_SUBS == 0
        grid = (b, dil // ATTN_SUBS, 1)
    cur, prv = spec(d)
    stat, _ = spec(LANES)
    tab2, tab1 = _pair_tables(dil, win // dil)
    in_specs = [cur, cur, cur]
    args = [q, kb, vb]
    if chained:
        in_specs += [prv, prv, _resident(tab2.shape)]
        args += [kb, vb, tab2]
    in_specs.append(_resident(tab1.shape))
    args.append(tab1)
    width = N_HEADS * 2 * BAND
    return pl.pallas_call(
        functools.partial(_band_attn_body, chained=chained),
        grid=grid,
        in_specs=in_specs,
        out_specs=[cur, stat],
        out_shape=[jax.ShapeDtypeStruct(q.shape, F32), jax.ShapeDtypeStruct(q.shape[:-1] + (LANES,), F32)],
        scratch_shapes=[pltpu.VMEM((ATTN_SUBS, BAND, width), F32), pltpu.VMEM((ATTN_SUBS, BAND, width), BF16)],
        compiler_params=_params(3),
        name=f"band_attn_g{gi}",
    )(*args)


STEP_HEADS = 4
STEP_COLS = STEP_HEADS * HEAD_DIM


def _step_attn_body(*refs, n_past, t_new, append):
    q_refs = refs[:N_DIL]
    if append:
        wk_ref, wv_ref, xk_ref, xv_ref, tab_ref, o_ref, ok_ref, ov_ref, kb_ref, vb_ref = refs[N_DIL:]
        outs = (ok_ref, ov_ref)
    else:
        wk_ref, wv_ref, xk_ref, xv_ref, tab_ref, o_ref, kb_ref, vb_ref = refs[N_DIL:]
        outs = (None, None)
    rows_g = N_DIL * t_new
    rows = STEP_HEADS * rows_g

    for w_ref, x_ref, out_ref, b_ref in zip((wk_ref, wv_ref), (xk_ref, xv_ref), outs, (kb_ref, vb_ref)):
        win, extra = w_ref[0], x_ref[0]
        b_ref[:, 0:n_past] = win.astype(BF16)
        b_ref[:, n_past:n_past + LANES] = extra.astype(BF16)
        if append:
            out_ref[0] = jnp.concatenate([win, extra], axis=1)[:, t_new:t_new + n_past]

    q_rows = jnp.concatenate([r[0] for r in q_refs], axis=0)
    q_all = jnp.concatenate([q_rows] * STEP_HEADS, axis=0)
    row_head = lax.broadcasted_iota(jnp.int32, (rows, STEP_COLS), 0) // rows_g
    col_head = lax.broadcasted_iota(jnp.int32, (rows, STEP_COLS), 1) // HEAD_DIM
    own = row_head == col_head
    q_bd = jnp.where(own, q_all, jnp.zeros_like(q_all))
    s = _dot(q_bd, kb_ref[...]) + tab_ref[...]
    m = jnp.max(s, axis=-1, keepdims=True)
    p = jnp.exp(s - m)
    l = jnp.sum(p, axis=-1, keepdims=True)
    o_full = _dot_nt(p.astype(BF16), vb_ref[...]) / l
    lse = m + jnp.log(l)

    out = jnp.zeros((t_new, STEP_COLS), F32)
    col = lax.broadcasted_iota(jnp.int32, (t_new, STEP_COLS), 1) // HEAD_DIM
    for hh in range(STEP_HEADS):
        base = hh * rows_g
        lses = [lse[base + g * t_new:base + (g + 1) * t_new] for g in range(N_DIL)]
        mm = functools.reduce(jnp.maximum, lses)
        ws = [jnp.exp(x - mm) for x in lses]
        den = functools.reduce(lambda a, b: a + b, ws)
        acc = functools.reduce(
            lambda a, b: a + b,
            [ws[g] * o_full[base + g * t_new:base + (g + 1) * t_new] for g in range(N_DIL)])
        out = jnp.where(col == hh, acc / den, out)
    o_ref[0] = out.astype(o_ref.dtype)


def _step_attention(q, win_k, win_v, extra_k, extra_v, append):
    b, t_new, _ = q.shape
    n_past = win_k.shape[2]
    d = D_MODEL
    assert t_new % 8 == 0 and t_new <= LANES and n_past % LANES == 0
    n_keys = n_past + LANES
    rows = STEP_HEADS * N_DIL * t_new
    tab = _step_table(n_past, t_new, n_keys, append)
    n_hg = N_HEADS // STEP_HEADS

    def q_spec(g):
        return pl.BlockSpec((1, t_new, STEP_COLS), lambda i, j: (i, 0, g * n_hg + j))

    win_spec = pl.BlockSpec((1, STEP_COLS, n_past), lambda i, j: (i, j, 0))
    extra_spec = pl.BlockSpec((1, STEP_COLS, LANES), lambda i, j: (i, j, 0))
    out_specs = [pl.BlockSpec((1, t_new, STEP_COLS), lambda i, j: (i, 0, j))]
    out_shape = [jax.ShapeDtypeStruct((b, t_new, d), BF16)]
    if append:
        out_specs += [win_spec, win_spec]
        out_shape += [jax.ShapeDtypeStruct((b, d, n_past), F32)] * 2
    return pl.pallas_call(
        functools.partial(_step_attn_body, n_past=n_past, t_new=t_new, append=append),
        grid=(b, n_hg),
        in_specs=[q_spec(g) for g in range(N_DIL)] + [win_spec, win_spec, extra_spec, extra_spec,
                                                     pl.BlockSpec((rows, n_keys), lambda i, j: (j, 0))],
        out_specs=out_specs,
        out_shape=out_shape,
        scratch_shapes=[pltpu.VMEM((STEP_COLS, n_keys), BF16)] * 2,
        compiler_params=_params(2),
        name="step_attn",
    )(q, q, q, win_k, win_v, extra_k, extra_v, tab)


G_FFN1, G_MIX, G_FFN2 = 0, 2, 4


def _ffn_w(w, layer, half):
    idx = (layer, half)
    return _Sel(w["ffn_w_gate"], idx), _Sel(w["ffn_w_up"], idx), _Sel(w["ffn_w_down"], idx)


def _pool_layers(h, b, s, pool_prev, pos0, w, n_a):
    d = D_MODEL
    pool_rows = []
    for layer in range(n_a):
        g = _Sel(w["norm_g"], (layer,))
        h = _ffn(h, g, G_FFN1, *_ffn_w(w, layer, 0))
        h3, rows = _pool_mixer(h.reshape(b, s, d), pool_prev[layer], g, G_MIX, _Sel(w["pool_w_in"], (layer,)),
                               _Sel(w["pool_w_grp"], (layer,)), _Sel(w["pool_scale"], (layer,)),
                               _Sel(w["pool_w_out"], (layer,)), pos0)
        h = h3.reshape(b * s, d)
        pool_rows.append(rows[:, POOL_HALO - POOL_BUF:])
        h = _ffn(h, g, G_FFN2, *_ffn_w(w, layer, 1))
    return h, jnp.stack(pool_rows)


def _prompt_trunk(x, w):
    b, s, d = x.shape
    t = b * s
    depth = w["norm_g"].shape[0]
    n_a = depth // 2
    assert s % (R16 * BAND) == 0 and s % TOKEN_TILE == 0
    empty = jnp.zeros((n_a, b, POOL_HALO, d), F32)
    h, pool_rows = _pool_layers(x.reshape(t, d), b, s, empty, 0, w, n_a)
    k32, v32, kn, vn, k4, v4, k16, v16 = _kv_streams(h, w["kv_norm"], w["w_k"], w["w_v"], b, s)
    kn, vn = kn.reshape(b, s, d), vn.reshape(b, s, d)
    for layer in range(n_a, depth):
        g = _Sel(w["norm_g"], (layer,))
        j = layer - n_a
        if layer == n_a:
            h = _ffn(h, g, G_FFN1, *_ffn_w(w, layer, 0), order_in="natural", order_out="by16", seq=s)
        else:
            h = _ffn(h, g, G_FFN1, *_ffn_w(w, layer, 0))
            h = h.reshape(b, R16, s // R16, d)
        q0, q1, q2 = _q_streams(h, g, G_MIX, _Sel(w["attn_w_q"], (j,)))
        groups = [_band_attn_group(q0, kn, vn, 0), _band_attn_group(q1, k4, v4, 1),
                  _band_attn_group(q2, k16, v16, 2)]
        h = _merge_out_proj(groups, h, g, G_MIX + 1, _Sel(w["attn_w_o"], (j,)))
        if layer == depth - 1:
            h = _ffn(h, g, G_FFN2, *_ffn_w(w, layer, 1), order_in="by16", order_out="natural", seq=s)
        else:
            h = _ffn(h.reshape(t, d), g, G_FFN2, *_ffn_w(w, layer, 1))
    buf = min(max(win for win, _ in DIL_GROUPS), s)
    k32, v32 = k32[:, s - buf:], v32[:, s - buf:]
    return h.reshape(b, s, d), pool_rows, k32, v32


def _kv_new_body(x_ref, g_ref, wk_ref, wv_ref, k_ref, v_ref):
    t_new, d = x_ref.shape[1:]
    xn = _rms(x_ref[0], g_ref[...])
    xn = jnp.concatenate([xn, jnp.zeros((LANES - t_new, d), F32)], axis=0).astype(BF16)
    k_ref[0] = _dot_nt(wk_ref[...], xn)
    v_ref[0] = _dot_nt(wv_ref[...], xn)


def _kv_new(x, g, wk_t, wv_t):
    b, t_new, d = x.shape
    assert t_new % 8 == 0 and t_new <= LANES
    out = jax.ShapeDtypeStruct((b, d, LANES), F32)
    out_spec = pl.BlockSpec((1, d, LANES), lambda i: (i, 0, 0))
    return pl.pallas_call(
        _kv_new_body,
        grid=(b,),
        in_specs=[pl.BlockSpec((1, t_new, d), lambda i: (i, 0, 0)), _resident(g.shape), _resident(wk_t.shape),
                  _resident(wv_t.shape)],
        out_specs=[out_spec, out_spec],
        out_shape=[out, out],
        compiler_params=_params(1),
        name="kv_new",
    )(x, g, wk_t, wv_t)


def _sample_trunk(x, pool_prev, cache_kt, cache_vt, pos0, w):
    b, s, d = x.shape
    t = b * s
    depth = w["norm_g"].shape[0]
    n_a = depth // 2
    h, pool_rows = _pool_layers(x.reshape(t, d), b, s, pool_prev, pos0, w, n_a)
    new_kt, new_vt = _kv_new(h.reshape(b, s, d), w["kv_norm"], w["w_k"].T, w["w_v"].T)
    win_k, win_v = cache_kt, cache_vt
    for layer in range(n_a, depth):
        g = _Sel(w["norm_g"], (layer,))
        j = layer - n_a
        h = _ffn(h, g, G_FFN1, *_ffn_w(w, layer, 0))
        (q,) = _norm_proj(h, g, G_MIX, [_Sel(w["attn_w_q"], (j,))], out_cfg=((0, BF16, ATTN_SCALE),))
        q = q.reshape(b, s, N_DIL * d)
        if j == 0:
            o, win_k, win_v = _step_attention(q, win_k, win_v, new_kt, new_vt, append=True)
        else:
            (o,) = _step_attention(q, win_k, win_v, cache_kt, cache_vt, append=False)
        h = _out_proj(o.reshape(t, d), h, g, G_MIX + 1, _Sel(w["attn_w_o"], (j,)))
        h = _ffn(h, g, G_FFN2, *_ffn_w(w, layer, 1))
    return h.reshape(b, s, d), pool_rows, win_k, win_v


def kernel(x_prompt, x_sample, state_pool, cache_k, cache_v, norm_g, ffn_w_gate, ffn_w_up, ffn_w_down,
           pool_w_in, pool_w_grp, pool_scale, pool_w_out, kv_norm, w_k, w_v, attn_w_q, attn_w_o):
    d = D_MODEL
    w = dict(
        norm_g=norm_g,
        ffn_w_gate=ffn_w_gate.astype(BF16), ffn_w_up=ffn_w_up.astype(BF16), ffn_w_down=ffn_w_down.astype(BF16),
        pool_w_in=pool_w_in.astype(BF16), pool_w_grp=pool_w_grp.astype(BF16),
        pool_scale=pool_scale.reshape(-1, 1, d), pool_w_out=pool_w_out.astype(BF16),
        kv_norm=kv_norm.reshape(1, d), w_k=w_k.astype(BF16), w_v=w_v.astype(BF16),
        attn_w_q=attn_w_q.astype(BF16), attn_w_o=attn_w_o.astype(BF16),
    )
    bs, n_past = cache_k.shape[:2]
    y_p, pool_p, k_p, v_p = _prompt_trunk(x_prompt, w)

    prev_s = jnp.pad(state_pool, ((0, 0), (0, 0), (POOL_HALO - POOL_BUF, 0), (0, 0)))
    def feature_major(c):
        return jnp.transpose(c.reshape(bs, n_past, d), (0, 2, 1))

    y_s, pool_s, k_s, v_s = _sample_trunk(x_sample, prev_s, feature_major(cache_k), feature_major(cache_v),
                                          PAST_LEN, w)
    k_s, v_s = jnp.transpose(k_s, (0, 2, 1)), jnp.transpose(v_s, (0, 2, 1))
    hd = (N_HEADS, HEAD_DIM)
    return (y_p, y_s, pool_p,
            k_p.reshape(k_p.shape[:2] + hd), v_p.reshape(v_p.shape[:2] + hd),
            pool_s,
            k_s.reshape(k_s.shape[:2] + hd), v_s.reshape(v_s.shape[:2] + hd))
```

```python
import functools
import math
from typing import NamedTuple

import jax
import jax.numpy as jnp
from jax import lax
from jax.experimental import pallas as pl
from jax.experimental.pallas import tpu as pltpu

D_MODEL = 1024
D_FF = 2816
HEAD_DIM = 64
N_HEADS = D_MODEL // HEAD_DIM
POOL_WINDOWS = (2, 4, 8, 16)
POOL_GROUP_DIM = D_MODEL // len(POOL_WINDOWS)
POOL_BUF = max(POOL_WINDOWS) - 1
POOL_HALO = 16
POOL_TOP = 32
DIL_GROUPS = ((128, 1), (512, 4), (2048, 16))
N_DIL = len(DIL_GROUPS)
BAND = 128
ATTN_SUBS = 4
PAST_LEN = 8192
RMS_EPS = 1e-6
NEG_INF = -1e30
ATTN_SCALE = HEAD_DIM ** -0.5
LOG2E = 1.4426950408889634
LN2 = 0.6931471805599453
LANES = 128
N_SLABS = D_MODEL // LANES
TOKEN_TILE = 512
FFN_TILE = 512
FFN_SPLIT_ROWS = 256
R16 = 16
ROWS16 = TOKEN_TILE // R16
VMEM_LIMIT_BYTES = 56 * 1024 * 1024

F32 = jnp.float32
BF16 = jnp.bfloat16


def _params(n_axes):
    return pltpu.CompilerParams(dimension_semantics=("arbitrary",) * n_axes,
                                vmem_limit_bytes=VMEM_LIMIT_BYTES)


def _resident(shape):
    zeros = (0,) * len(shape)
    return pl.BlockSpec(shape, lambda *_: zeros, pipeline_mode=pl.Buffered(1))


class _Sel(NamedTuple):
    arr: jax.Array
    idx: tuple = ()


def _member(sel):
    lead = len(sel.idx)
    shape = sel.arr.shape[lead:]
    index = tuple(sel.idx) + (0,) * len(shape)
    return pl.BlockSpec((None,) * lead + shape, lambda *_: index, pipeline_mode=pl.Buffered(1))


def _rms(x, g):
    return x * lax.rsqrt(jnp.mean(x * x, axis=-1, keepdims=True) + RMS_EPS) * g


def _dot(a, b):
    return jnp.dot(a, b, preferred_element_type=F32)


def _dot_nt(a, b):
    return lax.dot_general(a, b, (((1,), (1,)), ((), ())), preferred_element_type=F32)


def _token_tile(t, largest=TOKEN_TILE):
    for tm in (largest, TOKEN_TILE, 256, 128, 64, 32, 16, 8):
        if t % tm == 0:
            return tm
    raise ValueError(f"token count {t} is not a multiple of 8")


SLAB_SCRATCH = pltpu.VMEM((N_SLABS, TOKEN_TILE, LANES), F32)


def _slabs_put(scr, val):
    for j in range(val.shape[-1] // LANES):
        scr[j, 0:val.shape[0], :] = val[:, j * LANES:(j + 1) * LANES]


def _slabs_get(scr, n_slabs, n_rows=TOKEN_TILE):
    return jnp.concatenate([scr[j, 0:n_rows, :] for j in range(n_slabs)], axis=-1)


def _slabs_gather(scr, n_slabs, start, n, stride):
    return jnp.concatenate([scr[j, pl.ds(start, n, stride=stride), :] for j in range(n_slabs)], axis=-1)


def _slabs_scatter(scr, start, n, stride, val):
    for j in range(val.shape[-1] // LANES):
        scr[j, pl.ds(start, n, stride=stride), :] = val[:, j * LANES:(j + 1) * LANES]


def _natural_to_by4(scr, val):
    n_slabs, m = val.shape[-1] // LANES, val.shape[0] // 4
    _slabs_put(scr, val)
    return jnp.concatenate([_slabs_gather(scr, n_slabs, r4, m, 4) for r4 in range(4)], axis=0)


def _by4_to_natural(scr, val):
    m = val.shape[0] // 4
    for r4 in range(4):
        _slabs_scatter(scr, r4, m, 4, val[r4 * m:(r4 + 1) * m])
    return _slabs_get(scr, val.shape[-1] // LANES, val.shape[0])


def _by4_to_by16(scr, val):
    n_slabs, n = val.shape[-1] // LANES, val.shape[0] // R16
    _slabs_put(scr, val)
    parts = [None] * R16
    for r4 in range(4):
        for c in range(4):
            parts[4 * c + r4] = _slabs_gather(scr, n_slabs, r4 * 4 * n + c, n, 4)
    return jnp.concatenate(parts, axis=0)


def _by16_to_by4(scr, val):
    n = val.shape[0] // R16
    for r4 in range(4):
        for c in range(4):
            r16 = 4 * c + r4
            _slabs_scatter(scr, r4 * 4 * n + c, n, 4, val[r16 * n:(r16 + 1) * n])
    return _slabs_get(scr, val.shape[-1] // LANES, val.shape[0])


def _natural_to_by16(scr, val):
    n = val.shape[0] // R16
    by16 = _by4_to_by16(scr, _natural_to_by4(scr, val))
    return [by16[r * n:(r + 1) * n] for r in range(R16)]


def _by16_to_natural(scr, val):
    return _by4_to_natural(scr, _by16_to_by4(scr, val))


def _tile16_spec(b_tiles, c):
    return pl.BlockSpec((1, R16, ROWS16, c), lambda i: (i // b_tiles, 0, i % b_tiles, 0))


def _tile4_spec(b_tiles, c):
    return pl.BlockSpec((1, 4, TOKEN_TILE // 4, c), lambda i: (i // b_tiles, 0, i % b_tiles, 0))


def _ffn_body(x_ref, g_ref, wg_ref, wu_ref, wd_ref, o_ref, *scratch, g_row, order_in, order_out):
    d = D_MODEL

    def half_step(x):
        xn = _rms(x, g_ref[g_row:g_row + 1, :]).astype(BF16)
        hg = _dot(xn, wg_ref[...])
        hu = _dot(xn, wu_ref[...])
        a = (hg / (1.0 + jnp.exp(-hg)) * hu).astype(BF16)
        f = _dot(a, wd_ref[...])
        return x + 0.5 * _rms(f, g_ref[g_row + 1:g_row + 2, :])

    n = math.prod(x_ref.shape) // d
    rows = FFN_SPLIT_ROWS if n % FFN_SPLIT_ROWS == 0 else n
    per_res = rows // R16
    for k in range(n // rows):
        if order_in == "by16":
            x = x_ref[0, :, k * per_res:(k + 1) * per_res, :].reshape(rows, d)
        else:
            x = x_ref[k * rows:(k + 1) * rows, :]
        y = half_step(x)
        if order_out == "by16" and order_in == "by16":
            o_ref[0, :, k * per_res:(k + 1) * per_res, :] = y.reshape(R16, per_res, d)
        elif order_out == "by16":
            for r, part in enumerate(_natural_to_by16(scratch[0], y)):
                o_ref[0, r, k * per_res:(k + 1) * per_res, :] = part
        elif order_in == "by16":
            o_ref[k * rows:(k + 1) * rows, :] = _by16_to_natural(scratch[0], y)
        else:
            o_ref[k * rows:(k + 1) * rows, :] = y


def _ffn(x, g, g_row, wg, wu, wd, order_in="natural", order_out="natural", seq=None):
    d = D_MODEL
    regroup = order_in != order_out
    if regroup:
        b_tiles = seq // TOKEN_TILE
        t = x.size // d
        tm = TOKEN_TILE
        nat = pl.BlockSpec((tm, d), lambda i: (i, 0))
        p16 = _tile16_spec(b_tiles, d)
        in_spec, out_spec = (nat, p16) if order_out == "by16" else (p16, nat)
        out_shape = (t // seq, R16, seq // R16, d) if order_out == "by16" else (t, d)
        scratch = [SLAB_SCRATCH]
    else:
        t = x.shape[0]
        tm = _token_tile(t, FFN_TILE)
        in_spec = out_spec = pl.BlockSpec((tm, d), lambda i: (i, 0))
        out_shape = (t, d)
        scratch = []
    return pl.pallas_call(
        functools.partial(_ffn_body, g_row=g_row, order_in=order_in, order_out=order_out),
        grid=(t // tm,),
        in_specs=[in_spec, _member(g), _member(wg), _member(wu), _member(wd)],
        out_specs=out_spec,
        out_shape=jax.ShapeDtypeStruct(out_shape, F32),
        scratch_shapes=scratch,
        compiler_params=_params(1),
        name="ffn",
    )(x, g.arr, wg.arr, wu.arr, wd.arr)


def _proj_body(x_ref, g_ref, *refs, g_row, out_cfg):
    n_w = max(wi for wi, _, _ in out_cfg) + 1
    w_refs, o_refs = refs[:n_w], refs[n_w:]
    xn = _rms(x_ref[...], g_ref[g_row:g_row + 1, :]).astype(BF16)
    ys = [_dot(xn, w[...]) for w in w_refs]
    for o_ref, (wi, _, scale) in zip(o_refs, out_cfg):
        y = ys[wi] if scale == 1.0 else ys[wi] * scale
        o_ref[...] = y.astype(o_ref.dtype)


def _norm_proj(x, g, g_row, ws, out_cfg):
    t, d = x.shape
    tm = _token_tile(t)
    tile = pl.BlockSpec((tm, d), lambda i: (i, 0))
    n_out = [w.arr.shape[-1] for w in ws]
    return pl.pallas_call(
        functools.partial(_proj_body, g_row=g_row, out_cfg=out_cfg),
        grid=(t // tm,),
        in_specs=[tile, _member(g)] + [_member(w) for w in ws],
        out_specs=[pl.BlockSpec((tm, n_out[wi]), lambda i: (i, 0)) for wi, _, _ in out_cfg],
        out_shape=[jax.ShapeDtypeStruct((t, n_out[wi]), dt) for wi, dt, _ in out_cfg],
        compiler_params=_params(1),
        name="norm_proj",
    )(x, g.arr, *[w.arr for w in ws])


def _kv_streams_body(x_ref, g_ref, wk_ref, wv_ref, k32_ref, v32_ref, kn_ref, vn_ref, k4_ref, v4_ref,
                     k16_ref, v16_ref, scr):
    xn = _rms(x_ref[...], g_ref[...]).astype(BF16)
    per_r4 = TOKEN_TILE // 4
    for w_ref, o32, on, o4, o16 in ((wk_ref, k32_ref, kn_ref, k4_ref, k16_ref),
                                    (wv_ref, v32_ref, vn_ref, v4_ref, v16_ref)):
        y = _dot(xn, w_ref[...])
        o32[0] = y
        on[...] = y.astype(BF16)
        y4 = _natural_to_by4(scr, y)
        o4[0] = y4.astype(BF16).reshape(4, per_r4, D_MODEL)
        o16[0] = _by4_to_by16(scr, y4).astype(BF16).reshape(R16, ROWS16, D_MODEL)


def _kv_streams(x, g, wk, wv, b, s):
    t, d = x.shape
    b_tiles = s // TOKEN_TILE
    nat = pl.BlockSpec((TOKEN_TILE, d), lambda i: (i, 0))
    f32n = jax.ShapeDtypeStruct((b, s, d), F32)
    nat3 = pl.BlockSpec((1, TOKEN_TILE, d), lambda i: (i // b_tiles, i % b_tiles, 0))
    bfn = jax.ShapeDtypeStruct((t, d), BF16)
    bf4 = jax.ShapeDtypeStruct((b, 4, s // 4, d), BF16)
    bf16_ = jax.ShapeDtypeStruct((b, R16, s // R16, d), BF16)
    return pl.pallas_call(
        _kv_streams_body,
        grid=(t // TOKEN_TILE,),
        in_specs=[nat, _resident(g.shape), _resident(wk.shape), _resident(wv.shape)],
        out_specs=[nat3, nat3, nat, nat, _tile4_spec(b_tiles, d), _tile4_spec(b_tiles, d),
                   _tile16_spec(b_tiles, d), _tile16_spec(b_tiles, d)],
        out_shape=[f32n, f32n, bfn, bfn, bf4, bf4, bf16_, bf16_],
        scratch_shapes=[SLAB_SCRATCH],
        compiler_params=_params(1),
        name="kv_streams",
    )(x, g, wk, wv)


def _q_streams_body(x_ref, g_ref, w_ref, q0_ref, q1_ref, q2_ref, scr, *, g_row):
    d = D_MODEL
    xn = _rms(x_ref[0].reshape(TOKEN_TILE, d), g_ref[g_row:g_row + 1, :]).astype(BF16)
    scale = ATTN_SCALE * LOG2E
    y0 = _dot(xn, w_ref[:, 0:d]) * scale
    q0_ref[0] = _by16_to_natural(scr, y0).astype(BF16)
    y1 = _dot(xn, w_ref[:, d:2 * d]) * scale
    q1_ref[0] = _by16_to_by4(scr, y1).astype(BF16).reshape(4, TOKEN_TILE // 4, d)
    y2 = _dot(xn, w_ref[:, 2 * d:3 * d]) * scale
    q2_ref[0] = y2.astype(BF16).reshape(R16, ROWS16, d)


def _q_streams(h16, g, g_row, wq):
    b, _, n16, d = h16.shape
    s = n16 * R16
    b_tiles = s // TOKEN_TILE
    return pl.pallas_call(
        functools.partial(_q_streams_body, g_row=g_row),
        grid=(b * b_tiles,),
        in_specs=[_tile16_spec(b_tiles, d), _member(g), _member(wq)],
        out_specs=[pl.BlockSpec((1, TOKEN_TILE, d), lambda i: (i // b_tiles, i % b_tiles, 0)),
                   _tile4_spec(b_tiles, d), _tile16_spec(b_tiles, d)],
        out_shape=[jax.ShapeDtypeStruct((b, s, d), BF16), jax.ShapeDtypeStruct((b, 4, s // 4, d), BF16),
                   jax.ShapeDtypeStruct((b, R16, n16, d), BF16)],
        scratch_shapes=[SLAB_SCRATCH],
        compiler_params=_params(1),
        name="q_streams",
    )(h16, g.arr, wq.arr)


def _oproj_body(o_ref, h_ref, g_ref, w_ref, out_ref, *, g_row):
    y = _dot(o_ref[...], w_ref[...])
    out_ref[...] = h_ref[...] + _rms(y, g_ref[g_row:g_row + 1, :])


def _out_proj(o, h, g, g_row, w):
    t, d = h.shape
    tm = _token_tile(t)
    tile = pl.BlockSpec((tm, d), lambda i: (i, 0))
    return pl.pallas_call(
        functools.partial(_oproj_body, g_row=g_row),
        grid=(t // tm,),
        in_specs=[tile, tile, _member(g), _member(w)],
        out_specs=tile,
        out_shape=jax.ShapeDtypeStruct((t, d), F32),
        compiler_params=_params(1),
        name="out_proj",
    )(o, h, g.arr, w.arr)


def _head_stat_lane(h):
    return (h % 2) * HEAD_DIM + h // 2


def _stat_expander():
    col_head = jnp.arange(D_MODEL) // HEAD_DIM
    row = jnp.arange(LANES)
    row_head = jnp.where(row % HEAD_DIM < N_HEADS // 2, 2 * (row % HEAD_DIM) + row // HEAD_DIM, -1)
    e = (row_head[:, None] == col_head[None, :]).astype(BF16)
    return jnp.concatenate([e, e], axis=0)


def _merge_oproj_body(o0_ref, l0_ref, o1_ref, l1_ref, o2_ref, l2_ref, h_ref, g_ref, w_ref, e_ref, out_ref, scr, *,
                      g_row):
    d = D_MODEL
    o0 = jnp.concatenate(_natural_to_by16(scr, o0_ref[0]), axis=0)
    l0 = jnp.concatenate(_natural_to_by16(scr, l0_ref[0]), axis=0)
    o1 = _by4_to_by16(scr, o1_ref[0].reshape(TOKEN_TILE, d))
    l1 = _by4_to_by16(scr, l1_ref[0].reshape(TOKEN_TILE, LANES))
    o2 = o2_ref[0].reshape(TOKEN_TILE, d)
    l2 = l2_ref[0].reshape(TOKEN_TILE, LANES)
    mm = jnp.maximum(jnp.maximum(l0, l1), l2)
    es = [jnp.exp(l - mm) for l in (l0, l1, l2)]
    den = es[0] + es[1] + es[2]
    o = None
    for e, og in zip(es, (o0, o1, o2)):
        wgt = e / den
        hi = wgt.astype(BF16)
        lo = (wgt - hi.astype(F32)).astype(BF16)
        term = _dot(jnp.concatenate([hi, lo], axis=1), e_ref[...]) * og
        o = term if o is None else o + term
    y = _dot(o.astype(BF16), w_ref[...])
    out_ref[0] = (h_ref[0].reshape(TOKEN_TILE, d) + _rms(y, g_ref[g_row:g_row + 1, :])).reshape(R16, ROWS16, d)


def _merge_out_proj(groups, h16, g, g_row, w):
    b, _, n16, d = h16.shape
    s = n16 * R16
    b_tiles = s // TOKEN_TILE
    (o0, l0), (o1, l1), (o2, l2) = groups
    expander = _stat_expander()

    def specs(c):
        return (pl.BlockSpec((1, TOKEN_TILE, c), lambda i: (i // b_tiles, i % b_tiles, 0)),
                _tile4_spec(b_tiles, c), _tile16_spec(b_tiles, c))

    (nat, p4, p16), (nat_s, p4_s, p16_s) = specs(d), specs(LANES)
    return pl.pallas_call(
        functools.partial(_merge_oproj_body, g_row=g_row),
        grid=(b * b_tiles,),
        in_specs=[nat, nat_s, p4, p4_s, p16, p16_s, p16, _member(g), _member(w), _resident(expander.shape)],
        out_specs=p16,
        out_shape=jax.ShapeDtypeStruct(h16.shape, F32),
        scratch_shapes=[SLAB_SCRATCH],
        compiler_params=_params(1),
        name="merge_out_proj",
    )(o0, l0, o1, l1, o2, l2, h16, g.arr, w.arr, expander)


def _pool_body(h_ref, prev_ref, g_ref, win_ref, wgrp_ref, scale_ref, wout_ref,
               out_ref, rows_ref, ext_ref, sa_ref, sb_ref, *, g_row, pos0, bb, ts):
    d = D_MODEL
    j = pl.program_id(1)
    top, end = POOL_TOP, POOL_TOP + ts

    @pl.when(j == 0)
    def _():
        ext_ref[:, 0:top - POOL_HALO, :] = jnp.zeros((bb, top - POOL_HALO, d), F32)
        ext_ref[:, top - POOL_HALO:top, :] = prev_ref[...]

    @pl.when(j > 0)
    def _():
        ext_ref[:, 0:top, :] = ext_ref[:, ts:end, :]

    h = h_ref[...].reshape(bb * ts, d)
    hn = _rms(h, g_ref[g_row:g_row + 1, :]).astype(BF16)
    u = _dot(hn, win_ref[...])
    ext_ref[:, top:end, :] = u.reshape(bb, ts, d)
    rows_ref[...] = ext_ref[:, end - POOL_HALO:end, :]

    def doubled(ref, cs, lo, shift):
        return ref[:, lo:end, cs] + ref[:, lo - shift:end - shift, cs]

    every = slice(None)
    pos = pos0 + j * ts + lax.broadcasted_iota(jnp.int32, (1, ts, 1), 1)
    zs = []
    for gi, win in enumerate(POOL_WINDOWS):
        cs = slice(gi * POOL_GROUP_DIM, (gi + 1) * POOL_GROUP_DIM)
        n_stage = win.bit_length() - 1
        src, src_cs = ext_ref, cs
        stage_refs = (sa_ref, sb_ref)
        for k in range(n_stage - 1):
            lo = top - 8 * (n_stage - 1 - k)
            dst = stage_refs[k % 2]
            dst[:, lo:end, :] = doubled(src, src_cs, lo, 1 << k)
            src, src_cs = dst, every
        tot = doubled(src, src_cs, top, win // 2)
        cur = ext_ref[:, top:end, cs]
        cnt = jnp.minimum(pos + 1, win).astype(F32)
        z = (tot / cnt - cur).reshape(bb * ts, POOL_GROUP_DIM).astype(BF16)
        zs.append(_dot(z, wgrp_ref[gi]))
    zc = (jnp.concatenate(zs, axis=-1) * scale_ref[...]).astype(BF16)
    y = _dot(zc, wout_ref[...])
    out_ref[...] = (h + _rms(y, g_ref[g_row + 1:g_row + 2, :])).reshape(bb, ts, d)


def _pool_mixer(h, prev, g, g_row, w_in, w_grp, scale, w_out, pos0):
    b, s, d = h.shape
    if s >= TOKEN_TILE:
        bb, ts = 1, TOKEN_TILE
    else:
        bb, ts = b, s
    n_t = s // ts
    assert s % ts == 0 and ts % 8 == 0 and (n_t == 1 or ts >= POOL_TOP)
    tile = pl.BlockSpec((bb, ts, d), lambda i, j: (i, j, 0))
    halo = pl.BlockSpec((bb, POOL_HALO, d), lambda i, j: (i, 0, 0))
    stage = pltpu.VMEM((bb, POOL_TOP + ts, POOL_GROUP_DIM), F32)
    return pl.pallas_call(
        functools.partial(_pool_body, g_row=g_row, pos0=pos0, bb=bb, ts=ts),
        grid=(b // bb, n_t),
        in_specs=[tile, halo, _member(g), _member(w_in), _member(w_grp), _member(scale), _member(w_out)],
        out_specs=[tile, halo],
        out_shape=[jax.ShapeDtypeStruct((b, s, d), F32),
                   jax.ShapeDtypeStruct((b, POOL_HALO, d), F32)],
        scratch_shapes=[pltpu.VMEM((bb, POOL_TOP + ts, d), F32), stage, stage],
        compiler_params=_params(2),
        name="pool_mixer",
    )(h, prev, g.arr, w_in.arr, w_grp.arr, scale.arr, w_out.arr)


def _alibi_slopes(n):
    return 2.0 ** (-8.0 * jnp.arange(1, n + 1, dtype=F32) / n)


def _band_table(dil, n_steps):
    qi = jnp.arange(BAND)[:, None]
    ki = jnp.arange(2 * BAND)[None, :]
    step = qi + BAND - ki
    valid = (step >= 0) & (step <= n_steps)
    bias = -_alibi_slopes(N_HEADS)[:, None, None] * (step * dil).astype(F32)[None]
    return jnp.where(valid[None], bias, NEG_INF)


def _pair_tables(dil, n_steps):
    tab = _band_table(dil, n_steps) * LOG2E
    both = tab.reshape(N_HEADS // 2, 2, BAND, 2 * BAND)
    with_prev = jnp.concatenate([both[:, 0], both[:, 1]], axis=-1)
    cur_only = jnp.concatenate([both[:, 0, :, BAND:], both[:, 1, :, BAND:]], axis=-1)
    return with_prev, cur_only


def _step_key_positions(n_past, t_new, n_keys, old_window):
    j = jnp.arange(n_keys)
    if old_window:
        return jnp.where(j < n_past + t_new, j, -1)
    pos = jnp.where(j < n_past, j + t_new, -1)
    return jnp.where((j >= n_past) & (j < n_past + t_new), j - n_past, pos)


def _step_table(n_past, t_new, n_keys, old_window):
    pos = _step_key_positions(n_past, t_new, n_keys, old_window)
    dist = n_past + jnp.arange(t_new)[:, None] - pos[None, :]
    held = (pos >= 0)[None, :]
    bias = -_alibi_slopes(N_HEADS)[:, None, None] * dist.astype(F32)[None]
    tabs = []
    for win, dil in DIL_GROUPS:
        valid = (dist >= 0) & (dist % dil == 0) & (dist <= win) & held
        tabs.append(jnp.where(valid[None], bias, NEG_INF))
    return jnp.stack(tabs, axis=1).reshape(N_HEADS * N_DIL * t_new, n_keys)


def _band_attn_body(*refs, chained):
    if chained:
        q_ref, kc_ref, vc_ref, kp_ref, vp_ref, tab2_ref, tab1_ref, o_ref, l_ref, s_ref, p_ref = refs
    else:
        q_ref, kc_ref, vc_ref, tab1_ref, o_ref, l_ref, s_ref, p_ref = refs
        kp_ref = vp_ref = tab2_ref = None
    n_pairs = N_HEADS // 2
    lead = (0,) * (len(q_ref.shape) - (2 if chained else 3))

    def rows(ref, sub):
        view = ref.at[lead]
        return view.at[pl.ds(sub * BAND, BAND)] if chained else view.at[sub]

    def attend(sub, k_prev, v_prev):
        q_v, kc_v, vc_v, o_v, l_v = (rows(r, sub) for r in (q_ref, kc_ref, vc_ref, o_ref, l_ref))
        s_v, p_v = s_ref.at[sub], p_ref.at[sub]
        with_prev = k_prev is not None
        nk = 2 * BAND if with_prev else BAND
        low = lax.broadcasted_iota(jnp.int32, (nk, LANES), 1) < HEAD_DIM
        row_low = lax.broadcasted_iota(jnp.int32, (2 * nk, LANES), 0) < nk
        lane_low = lax.broadcasted_iota(jnp.int32, (2 * nk, LANES), 1) < HEAD_DIM
        ones_st = jnp.where(row_low == lane_low, 1.0, 0.0).astype(BF16)
        lane = lax.broadcasted_iota(jnp.int32, (BAND, LANES), 1)
        q_low = lane < HEAD_DIM
        tab_ref = tab2_ref if with_prev else tab1_ref

        def stacked(c_v, p_v_, cs):
            x = c_v[:, cs]
            if with_prev:
                x = jnp.concatenate([p_v_[:, cs], x], axis=0)
            z = jnp.zeros_like(x)
            return jnp.concatenate([jnp.where(low, x, z), jnp.where(low, z, x)], axis=0)

        for j in range(n_pairs):
            cs = slice(j * LANES, (j + 1) * LANES)
            s_v[:, 2 * nk * j:2 * nk * (j + 1)] = _dot_nt(q_v[:, cs], stacked(kc_v, k_prev, cs)) + tab_ref[j]
        ms = []
        for h in range(N_HEADS):
            sh = s_v[:, nk * h:nk * (h + 1)]
            m = jnp.max(sh, axis=-1, keepdims=True)
            p_v[:, nk * h:nk * (h + 1)] = jnp.exp2(sh - m).astype(BF16)
            ms.append(m)
        stats = jnp.zeros((BAND, LANES), F32)
        for j in range(n_pairs):
            cs = slice(j * LANES, (j + 1) * LANES)
            aug = jnp.concatenate([stacked(vc_v, v_prev, cs), ones_st], axis=1)
            r = _dot(p_v[:, 2 * nk * j:2 * nk * (j + 1)], aug)
            l = r[:, LANES:]
            o_v[:, cs] = r[:, :LANES] / l
            m2 = jnp.where(q_low, jnp.broadcast_to(ms[2 * j], (BAND, LANES)),
                           jnp.broadcast_to(ms[2 * j + 1], (BAND, LANES)))
            stats = jnp.where((lane & (HEAD_DIM - 1)) == j, (m2 + jnp.log2(l)) * LN2, stats)
        l_v[...] = stats

    if chained:
        first = pl.program_id(2) == 0
        kp_v, vp_v = kp_ref.at[lead], vp_ref.at[lead]
        pl.when(first)(lambda: attend(0, None, None))
        pl.when(jnp.logical_not(first))(lambda: attend(0, kp_v, vp_v))
        for sub in range(1, ATTN_SUBS):
            attend(sub, rows(kc_ref, sub - 1), rows(vc_ref, sub - 1))
    else:
        for sub in range(ATTN_SUBS):
            attend(sub, None, None)


def _band_attn_group(q, kb, vb, gi):
    win, dil = DIL_GROUPS[gi]
    d = D_MODEL
    n = q.shape[-2]
    b = q.shape[0]
    chained = n > BAND
    step_rows = ATTN_SUBS * BAND

    def spec(c):
        if dil == 1:
            return (pl.BlockSpec((1, step_rows, c), lambda i, r, j: (i, j, 0)),
                    pl.BlockSpec((1, BAND, c), lambda i, r, j: (i, jnp.maximum(ATTN_SUBS * j - 1, 0), 0)))
        if chained:
            return (pl.BlockSpec((1, 1, step_rows, c), lambda i, r, j: (i, r, j, 0)),
                    pl.BlockSpec((1, 1, BAND, c), lambda i, r, j: (i, r, jnp.maximum(ATTN_SUBS * j - 1, 0), 0)))
        return pl.BlockSpec((1, ATTN_SUBS, BAND, c), lambda i, r, j: (i, r, 0, 0)), None

    if chained:
        assert n % step_rows == 0
        grid = (b, dil, n // step_rows)
    else:
        assert n == BAND and dil % ATTN_SUBS == 0
        grid = (b, dil // ATTN_SUBS, 1)
    cur, prv = spec(d)
    stat, _ = spec(LANES)
    tab2, tab1 = _pair_tables(dil, win // dil)
    in_specs = [cur, cur, cur]
    args = [q, kb, vb]
    if chained:
        in_specs += [prv, prv, _resident(tab2.shape)]
        args += [kb, vb, tab2]
    in_specs.append(_resident(tab1.shape))
    args.append(tab1)
    width = N_HEADS * 2 * BAND
    return pl.pallas_call(
        functools.partial(_band_attn_body, chained=chained),
        grid=grid,
        in_specs=in_specs,
        out_specs=[cur, stat],
        out_shape=[jax.ShapeDtypeStruct(q.shape, F32), jax.ShapeDtypeStruct(q.shape[:-1] + (LANES,), F32)],
        scratch_shapes=[pltpu.VMEM((ATTN_SUBS, BAND, width), F32), pltpu.VMEM((ATTN_SUBS, BAND, width), BF16)],
        compiler_params=_params(3),
        name=f"band_attn_g{gi}",
    )(*args)


STEP_HEADS = 4
STEP_COLS = STEP_HEADS * HEAD_DIM


def _step_attn_body(*refs, n_past, t_new, append):
    q_refs = refs[:N_DIL]
    if append:
        wk_ref, wv_ref, xk_ref, xv_ref, tab_ref, o_ref, ok_ref, ov_ref, kb_ref, vb_ref = refs[N_DIL:]
        outs = (ok_ref, ov_ref)
    else:
        wk_ref, wv_ref, xk_ref, xv_ref, tab_ref, o_ref, kb_ref, vb_ref = refs[N_DIL:]
        outs = (None, None)
    rows_g = N_DIL * t_new
    rows = STEP_HEADS * rows_g

    for w_ref, x_ref, out_ref, b_ref in zip((wk_ref, wv_ref), (xk_ref, xv_ref), outs, (kb_ref, vb_ref)):
        win, extra = w_ref[0], x_ref[0]
        b_ref[:, 0:n_past] = win.astype(BF16)
        b_ref[:, n_past:n_past + LANES] = extra.astype(BF16)
        if append:
            out_ref[0] = jnp.concatenate([win, extra], axis=1)[:, t_new:t_new + n_past]

    q_rows = jnp.concatenate([r[0] for r in q_refs], axis=0)
    q_all = jnp.concatenate([q_rows] * STEP_HEADS, axis=0)
    row_head = lax.broadcasted_iota(jnp.int32, (rows, STEP_COLS), 0) // rows_g
    col_head = lax.broadcasted_iota(jnp.int32, (rows, STEP_COLS), 1) // HEAD_DIM
    own = row_head == col_head
    q_bd = jnp.where(own, q_all, jnp.zeros_like(q_all))
    s = _dot(q_bd, kb_ref[...]) + tab_ref[...]
    m = jnp.max(s, axis=-1, keepdims=True)
    p = jnp.exp(s - m)
    l = jnp.sum(p, axis=-1, keepdims=True)
    o_full = _dot_nt(p.astype(BF16), vb_ref[...]) / l
    lse = m + jnp.log(l)

    out = jnp.zeros((t_new, STEP_COLS), F32)
    col = lax.broadcasted_iota(jnp.int32, (t_new, STEP_COLS), 1) // HEAD_DIM
    for hh in range(STEP_HEADS):
        base = hh * rows_g
        lses = [lse[base + g * t_new:base + (g + 1) * t_new] for g in range(N_DIL)]
        mm = functools.reduce(jnp.maximum, lses)
        ws = [jnp.exp(x - mm) for x in lses]
        den = functools.reduce(lambda a, b: a + b, ws)
        acc = functools.reduce(
            lambda a, b: a + b,
            [ws[g] * o_full[base + g * t_new:base + (g + 1) * t_new] for g in range(N_DIL)])
        out = jnp.where(col == hh, acc / den, out)
    o_ref[0] = out.astype(o_ref.dtype)


def _step_attention(q, win_k, win_v, extra_k, extra_v, append):
    b, t_new, _ = q.shape
    n_past = win_k.shape[2]
    d = D_MODEL
    assert t_new % 8 == 0 and t_new <= LANES and n_past % LANES == 0
    n_keys = n_past + LANES
    rows = STEP_HEADS * N_DIL * t_new
    tab = _step_table(n_past, t_new, n_keys, append)
    n_hg = N_HEADS // STEP_HEADS

    def q_spec(g):
        return pl.BlockSpec((1, t_new, STEP_COLS), lambda i, j: (i, 0, g * n_hg + j))

    win_spec = pl.BlockSpec((1, STEP_COLS, n_past), lambda i, j: (i, j, 0))
    extra_spec = pl.BlockSpec((1, STEP_COLS, LANES), lambda i, j: (i, j, 0))
    out_specs = [pl.BlockSpec((1, t_new, STEP_COLS), lambda i, j: (i, 0, j))]
    out_shape = [jax.ShapeDtypeStruct((b, t_new, d), BF16)]
    if append:
        out_specs += [win_spec, win_spec]
        out_shape += [jax.ShapeDtypeStruct((b, d, n_past), F32)] * 2
    return pl.pallas_call(
        functools.partial(_step_attn_body, n_past=n_past, t_new=t_new, append=append),
        grid=(b, n_hg),
        in_specs=[q_spec(g) for g in range(N_DIL)] + [win_spec, win_spec, extra_spec, extra_spec,
                                                     pl.BlockSpec((rows, n_keys), lambda i, j: (j, 0))],
        out_specs=out_specs,
        out_shape=out_shape,
        scratch_shapes=[pltpu.VMEM((STEP_COLS, n_keys), BF16)] * 2,
        compiler_params=_params(2),
        name="step_attn",
    )(q, q, q, win_k, win_v, extra_k, extra_v, tab)


G_FFN1, G_MIX, G_FFN2 = 0, 2, 4


def _ffn_w(w, layer, half):
    idx = (layer, half)
    return _Sel(w["ffn_w_gate"], idx), _Sel(w["ffn_w_up"], idx), _Sel(w["ffn_w_down"], idx)


def _pool_layers(h, b, s, pool_prev, pos0, w, n_a):
    d = D_MODEL
    pool_rows = []
    for layer in range(n_a):
        g = _Sel(w["norm_g"], (layer,))
        h = _ffn(h, g, G_FFN1, *_ffn_w(w, layer, 0))
        h3, rows = _pool_mixer(h.reshape(b, s, d), pool_prev[layer], g, G_MIX, _Sel(w["pool_w_in"], (layer,)),
                               _Sel(w["pool_w_grp"], (layer,)), _Sel(w["pool_scale"], (layer,)),
                               _Sel(w["pool_w_out"], (layer,)), pos0)
        h = h3.reshape(b * s, d)
        pool_rows.append(rows[:, POOL_HALO - POOL_BUF:])
        h = _ffn(h, g, G_FFN2, *_ffn_w(w, layer, 1))
    return h, jnp.stack(pool_rows)


def _prompt_trunk(x, w):
    b, s, d = x.shape
    t = b * s
    depth = w["norm_g"].shape[0]
    n_a = depth // 2
    assert s % (R16 * BAND) == 0 and s % TOKEN_TILE == 0
    empty = jnp.zeros((n_a, b, POOL_HALO, d), F32)
    h, pool_rows = _pool_layers(x.reshape(t, d), b, s, empty, 0, w, n_a)
    k32, v32, kn, vn, k4, v4, k16, v16 = _kv_streams(h, w["kv_norm"], w["w_k"], w["w_v"], b, s)
    kn, vn = kn.reshape(b, s, d), vn.reshape(b, s, d)
    for layer in range(n_a, depth):
        g = _Sel(w["norm_g"], (layer,))
        j = layer - n_a
        if layer == n_a:
            h = _ffn(h, g, G_FFN1, *_ffn_w(w, layer, 0), order_in="natural", order_out="by16", seq=s)
        else:
            h = _ffn(h, g, G_FFN1, *_ffn_w(w, layer, 0))
            h = h.reshape(b, R16, s // R16, d)
        q0, q1, q2 = _q_streams(h, g, G_MIX, _Sel(w["attn_w_q"], (j,)))
        groups = [_band_attn_group(q0, kn, vn, 0), _band_attn_group(q1, k4, v4, 1),
                  _band_attn_group(q2, k16, v16, 2)]
        h = _merge_out_proj(groups, h, g, G_MIX + 1, _Sel(w["attn_w_o"], (j,)))
        if layer == depth - 1:
            h = _ffn(h, g, G_FFN2, *_ffn_w(w, layer, 1), order_in="by16", order_out="natural", seq=s)
        else:
            h = _ffn(h.reshape(t, d), g, G_FFN2, *_ffn_w(w, layer, 1))
    buf = min(max(win for win, _ in DIL_GROUPS), s)
    k32, v32 = k32[:, s - buf:], v32[:, s - buf:]
    return h.reshape(b, s, d), pool_rows, k32, v32


def _kv_new_body(x_ref, g_ref, wk_ref, wv_ref, k_ref, v_ref):
    t_new, d = x_ref.shape[1:]
    xn = _rms(x_ref[0], g_ref[...])
    xn = jnp.concatenate([xn, jnp.zeros((LANES - t_new, d), F32)], axis=0).astype(BF16)
    k_ref[0] = _dot_nt(wk_ref[...], xn)
    v_ref[0] = _dot_nt(wv_ref[...], xn)


def _kv_new(x, g, wk_t, wv_t):
    b, t_new, d = x.shape
    assert t_new % 8 == 0 and t_new <= LANES
    out = jax.ShapeDtypeStruct((b, d, LANES), F32)
    out_spec = pl.BlockSpec((1, d, LANES), lambda i: (i, 0, 0))
    return pl.pallas_call(
        _kv_new_body,
        grid=(b,),
        in_specs=[pl.BlockSpec((1, t_new, d), lambda i: (i, 0, 0)), _resident(g.shape), _resident(wk_t.shape),
                  _resident(wv_t.shape)],
        out_specs=[out_spec, out_spec],
        out_shape=[out, out],
        compiler_params=_params(1),
        name="kv_new",
    )(x, g, wk_t, wv_t)


def _sample_trunk(x, pool_prev, cache_kt, cache_vt, pos0, w):
    b, s, d = x.shape
    t = b * s
    depth = w["norm_g"].shape[0]
    n_a = depth // 2
    h, pool_rows = _pool_layers(x.reshape(t, d), b, s, pool_prev, pos0, w, n_a)
    new_kt, new_vt = _kv_new(h.reshape(b, s, d), w["kv_norm"], w["w_k"].T, w["w_v"].T)
    win_k, win_v = cache_kt, cache_vt
    for layer in range(n_a, depth):
        g = _Sel(w["norm_g"], (layer,))
        j = layer - n_a
        h = _ffn(h, g, G_FFN1, *_ffn_w(w, layer, 0))
        (q,) = _norm_proj(h, g, G_MIX, [_Sel(w["attn_w_q"], (j,))], out_cfg=((0, BF16, ATTN_SCALE),))
        q = q.reshape(b, s, N_DIL * d)
        if j == 0:
            o, win_k, win_v = _step_attention(q, win_k, win_v, new_kt, new_vt, append=True)
        else:
            (o,) = _step_attention(q, win_k, win_v, cache_kt, cache_vt, append=False)
        h = _out_proj(o.reshape(t, d), h, g, G_MIX + 1, _Sel(w["attn_w_o"], (j,)))
        h = _ffn(h, g, G_FFN2, *_ffn_w(w, layer, 1))
    return h.reshape(b, s, d), pool_rows, win_k, win_v


def kernel(x_prompt, x_sample, state_pool, cache_k, cache_v, norm_g, ffn_w_gate, ffn_w_up, ffn_w_down,
           pool_w_in, pool_w_grp, pool_scale, pool_w_out, kv_norm, w_k, w_v, attn_w_q, attn_w_o):
    d = D_MODEL
    w = dict(
        norm_g=norm_g,
        ffn_w_gate=ffn_w_gate.astype(BF16), ffn_w_up=ffn_w_up.astype(BF16), ffn_w_down=ffn_w_down.astype(BF16),
        pool_w_in=pool_w_in.astype(BF16), pool_w_grp=pool_w_grp.astype(BF16),
        pool_scale=pool_scale.reshape(-1, 1, d), pool_w_out=pool_w_out.astype(BF16),
        kv_norm=kv_norm.reshape(1, d), w_k=w_k.astype(BF16), w_v=w_v.astype(BF16),
        attn_w_q=attn_w_q.astype(BF16), attn_w_o=attn_w_o.astype(BF16),
    )
    bs, n_past = cache_k.shape[:2]
    y_p, pool_p, k_p, v_p = _prompt_trunk(x_prompt, w)

    prev_s = jnp.pad(state_pool, ((0, 0), (0, 0), (POOL_HALO - POOL_BUF, 0), (0, 0)))
    def feature_major(c):
        return jnp.transpose(c.reshape(bs, n_past, d), (0, 2, 1))

    y_s, pool_s, k_s, v_s = _sample_trunk(x_sample, prev_s, feature_major(cache_k), feature_major(cache_v),
                                          PAST_LEN, w)
    k_s, v_s = jnp.transpose(k_s, (0, 2, 1)), jnp.transpose(v_s, (0, 2, 1))
    hd = (N_HEADS, HEAD_DIM)
    return (y_p, y_s, pool_p,
            k_p.reshape(k_p.shape[:2] + hd), v_p.reshape(v_p.shape[:2] + hd),
            pool_s,
            k_s.reshape(k_s.shape[:2] + hd), v_s.reshape(v_s.shape[:2] + hd))
```

```python
import functools
import math
from typing import NamedTuple

import jax
import jax.numpy as jnp
from jax import lax
from jax.experimental import pallas as pl
from jax.experimental.pallas import tpu as pltpu

D_MODEL = 1024
D_FF = 2816
HEAD_DIM = 64
N_HEADS = D_MODEL // HEAD_DIM
POOL_WINDOWS = (2, 4, 8, 16)
POOL_GROUP_DIM = D_MODEL // len(POOL_WINDOWS)
POOL_BUF = max(POOL_WINDOWS) - 1
POOL_HALO = 16
POOL_TOP = 32
DIL_GROUPS = ((128, 1), (512, 4), (2048, 16))
N_DIL = len(DIL_GROUPS)
BAND = 128
ATTN_SUBS = 4
PAST_LEN = 8192
RMS_EPS = 1e-6
NEG_INF = -1e30
ATTN_SCALE = HEAD_DIM ** -0.5
LOG2E = 1.4426950408889634
LN2 = 0.6931471805599453
LANES = 128
N_SLABS = D_MODEL // LANES
TOKEN_TILE = 512
FFN_TILE = 512
FFN_SPLIT_ROWS = 256
R16 = 16
ROWS16 = TOKEN_TILE // R16
VMEM_LIMIT_BYTES = 56 * 1024 * 1024

F32 = jnp.float32
BF16 = jnp.bfloat16


def _params(n_axes):
    return pltpu.CompilerParams(dimension_semantics=("arbitrary",) * n_axes,
                                vmem_limit_bytes=VMEM_LIMIT_BYTES)


def _resident(shape):
    zeros = (0,) * len(shape)
    return pl.BlockSpec(shape, lambda *_: zeros, pipeline_mode=pl.Buffered(1))


class _Sel(NamedTuple):
    arr: jax.Array
    idx: tuple = ()


def _member(sel):
    lead = len(sel.idx)
    shape = sel.arr.shape[lead:]
    index = tuple(sel.idx) + (0,) * len(shape)
    return pl.BlockSpec((None,) * lead + shape, lambda *_: index, pipeline_mode=pl.Buffered(1))


def _rms(x, g):
    return x * lax.rsqrt(jnp.mean(x * x, axis=-1, keepdims=True) + RMS_EPS) * g


def _dot(a, b):
    return jnp.dot(a, b, preferred_element_type=F32)


def _dot_nt(a, b):
    return lax.dot_general(a, b, (((1,), (1,)), ((), ())), preferred_element_type=F32)


def _token_tile(t, largest=TOKEN_TILE):
    for tm in (largest, TOKEN_TILE, 256, 128, 64, 32, 16, 8):
        if t % tm == 0:
            return tm
    raise ValueError(f"token count {t} is not a multiple of 8")


SLAB_SCRATCH = pltpu.VMEM((N_SLABS, TOKEN_TILE, LANES), F32)


def _slabs_put(scr, val):
    for j in range(val.shape[-1] // LANES):
        scr[j, 0:val.shape[0], :] = val[:, j * LANES:(j + 1) * LANES]


def _slabs_get(scr, n_slabs, n_rows=TOKEN_TILE):
    return jnp.concatenate([scr[j, 0:n_rows, :] for j in range(n_slabs)], axis=-1)


def _slabs_gather(scr, n_slabs, start, n, stride):
    return jnp.concatenate([scr[j, pl.ds(start, n, stride=stride), :] for j in range(n_slabs)], axis=-1)


def _slabs_scatter(scr, start, n, stride, val):
    for j in range(val.shape[-1] // LANES):
        scr[j, pl.ds(start, n, stride=stride), :] = val[:, j * LANES:(j + 1) * LANES]


def _natural_to_by4(scr, val):
    n_slabs, m = val.shape[-1] // LANES, val.shape[0] // 4
    _slabs_put(scr, val)
    return jnp.concatenate([_slabs_gather(scr, n_slabs, r4, m, 4) for r4 in range(4)], axis=0)


def _by4_to_natural(scr, val):
    m = val.shape[0] // 4
    for r4 in range(4):
        _slabs_scatter(scr, r4, m, 4, val[r4 * m:(r4 + 1) * m])
    return _slabs_get(scr, val.shape[-1] // LANES, val.shape[0])


def _by4_to_by16(scr, val):
    n_slabs, n = val.shape[-1] // LANES, val.shape[0] // R16
    _slabs_put(scr, val)
    parts = [None] * R16
    for r4 in range(4):
        for c in range(4):
            parts[4 * c + r4] = _slabs_gather(scr, n_slabs, r4 * 4 * n + c, n, 4)
    return jnp.concatenate(parts, axis=0)


def _by16_to_by4(scr, val):
    n = val.shape[0] // R16
    for r4 in range(4):
        for c in range(4):
            r16 = 4 * c + r4
            _slabs_scatter(scr, r4 * 4 * n + c, n, 4, val[r16 * n:(r16 + 1) * n])
    return _slabs_get(scr, val.shape[-1] // LANES, val.shape[0])


def _natural_to_by16(scr, val):
    n = val.shape[0] // R16
    by16 = _by4_to_by16(scr, _natural_to_by4(scr, val))
    return [by16[r * n:(r + 1) * n] for r in range(R16)]


def _by16_to_natural(scr, val):
    return _by4_to_natural(scr, _by16_to_by4(scr, val))


def _tile16_spec(b_tiles, c):
    return pl.BlockSpec((1, R16, ROWS16, c), lambda i: (i // b_tiles, 0, i % b_tiles, 0))


def _tile4_spec(b_tiles, c):
    return pl.BlockSpec((1, 4, TOKEN_TILE // 4, c), lambda i: (i // b_tiles, 0, i % b_tiles, 0))


def _ffn_body(x_ref, g_ref, wg_ref, wu_ref, wd_ref, o_ref, *scratch, g_row, order_in, order_out):
    d = D_MODEL

    def half_step(x):
        xn = _rms(x, g_ref[g_row:g_row + 1, :]).astype(BF16)
        hg = _dot(xn, wg_ref[...])
        hu = _dot(xn, wu_ref[...])
        a = (hg / (1.0 + jnp.exp(-hg)) * hu).astype(BF16)
        f = _dot(a, wd_ref[...])
        return x + 0.5 * _rms(f, g_ref[g_row + 1:g_row + 2, :])

    n = math.prod(x_ref.shape) // d
    rows = FFN_SPLIT_ROWS if n % FFN_SPLIT_ROWS == 0 else n
    per_res = rows // R16
    for k in range(n // rows):
        if order_in == "by16":
            x = x_ref[0, :, k * per_res:(k + 1) * per_res, :].reshape(rows, d)
        else:
            x = x_ref[k * rows:(k + 1) * rows, :]
        y = half_step(x)
        if order_out == "by16" and order_in == "by16":
            o_ref[0, :, k * per_res:(k + 1) * per_res, :] = y.reshape(R16, per_res, d)
        elif order_out == "by16":
            for r, part in enumerate(_natural_to_by16(scratch[0], y)):
                o_ref[0, r, k * per_res:(k + 1) * per_res, :] = part
        elif order_in == "by16":
            o_ref[k * rows:(k + 1) * rows, :] = _by16_to_natural(scratch[0], y)
        else:
            o_ref[k * rows:(k + 1) * rows, :] = y


def _ffn(x, g, g_row, wg, wu, wd, order_in="natural", order_out="natural", seq=None):
    d = D_MODEL
    regroup = order_in != order_out
    if regroup:
        b_tiles = seq // TOKEN_TILE
        t = x.size // d
        tm = TOKEN_TILE
        nat = pl.BlockSpec((tm, d), lambda i: (i, 0))
        p16 = _tile16_spec(b_tiles, d)
        in_spec, out_spec = (nat, p16) if order_out == "by16" else (p16, nat)
        out_shape = (t // seq, R16, seq // R16, d) if order_out == "by16" else (t, d)
        scratch = [SLAB_SCRATCH]
    else:
        t = x.shape[0]
        tm = _token_tile(t, FFN_TILE)
        in_spec = out_spec = pl.BlockSpec((tm, d), lambda i: (i, 0))
        out_shape = (t, d)
        scratch = []
    return pl.pallas_call(
        functools.partial(_ffn_body, g_row=g_row, order_in=order_in, order_out=order_out),
        grid=(t // tm,),
        in_specs=[in_spec, _member(g), _member(wg), _member(wu), _member(wd)],
        out_specs=out_spec,
        out_shape=jax.ShapeDtypeStruct(out_shape, F32),
        scratch_shapes=scratch,
        compiler_params=_params(1),
        name="ffn",
    )(x, g.arr, wg.arr, wu.arr, wd.arr)


def _proj_body(x_ref, g_ref, *refs, g_row, out_cfg):
    n_w = max(wi for wi, _, _ in out_cfg) + 1
    w_refs, o_refs = refs[:n_w], refs[n_w:]
    xn = _rms(x_ref[...], g_ref[g_row:g_row + 1, :]).astype(BF16)
    ys = [_dot(xn, w[...]) for w in w_refs]
    for o_ref, (wi, _, scale) in zip(o_refs, out_cfg):
        y = ys[wi] if scale == 1.0 else ys[wi] * scale
        o_ref[...] = y.astype(o_ref.dtype)


def _norm_proj(x, g, g_row, ws, out_cfg):
    t, d = x.shape
    tm = _token_tile(t)
    tile = pl.BlockSpec((tm, d), lambda i: (i, 0))
    n_out = [w.arr.shape[-1] for w in ws]
    return pl.pallas_call(
        functools.partial(_proj_body, g_row=g_row, out_cfg=out_cfg),
        grid=(t // tm,),
        in_specs=[tile, _member(g)] + [_member(w) for w in ws],
        out_specs=[pl.BlockSpec((tm, n_out[wi]), lambda i: (i, 0)) for wi, _, _ in out_cfg],
        out_shape=[jax.ShapeDtypeStruct((t, n_out[wi]), dt) for wi, dt, _ in out_cfg],
        compiler_params=_params(1),
        name="norm_proj",
    )(x, g.arr, *[w.arr for w in ws])


def _kv_streams_body(x_ref, g_ref, wk_ref, wv_ref, k32_ref, v32_ref, kn_ref, vn_ref, k4_ref, v4_ref,
                     k16_ref, v16_ref, scr):
    xn = _rms(x_ref[...], g_ref[...]).astype(BF16)
    per_r4 = TOKEN_TILE // 4
    for w_ref, o32, on, o4, o16 in ((wk_ref, k32_ref, kn_ref, k4_ref, k16_ref),
                                    (wv_ref, v32_ref, vn_ref, v4_ref, v16_ref)):
        y = _dot(xn, w_ref[...])
        o32[0] = y
        on[...] = y.astype(BF16)
        y4 = _natural_to_by4(scr, y)
        o4[0] = y4.astype(BF16).reshape(4, per_r4, D_MODEL)
        o16[0] = _by4_to_by16(scr, y4).astype(BF16).reshape(R16, ROWS16, D_MODEL)


def _kv_streams(x, g, wk, wv, b, s):
    t, d = x.shape
    b_tiles = s // TOKEN_TILE
    nat = pl.BlockSpec((TOKEN_TILE, d), lambda i: (i, 0))
    f32n = jax.ShapeDtypeStruct((b, s, d), F32)
    nat3 = pl.BlockSpec((1, TOKEN_TILE, d), lambda i: (i // b_tiles, i % b_tiles, 0))
    bfn = jax.ShapeDtypeStruct((t, d), BF16)
    bf4 = jax.ShapeDtypeStruct((b, 4, s // 4, d), BF16)
    bf16_ = jax.ShapeDtypeStruct((b, R16, s // R16, d), BF16)
    return pl.pallas_call(
        _kv_streams_body,
        grid=(t // TOKEN_TILE,),
        in_specs=[nat, _resident(g.shape), _resident(wk.shape), _resident(wv.shape)],
        out_specs=[nat3, nat3, nat, nat, _tile4_spec(b_tiles, d), _tile4_spec(b_tiles, d),
                   _tile16_spec(b_tiles, d), _tile16_spec(b_tiles, d)],
        out_shape=[f32n, f32n, bfn, bfn, bf4, bf4, bf16_, bf16_],
        scratch_shapes=[SLAB_SCRATCH],
        compiler_params=_params(1),
        name="kv_streams",
    )(x, g, wk, wv)


def _q_streams_body(x_ref, g_ref, w_ref, q0_ref, q1_ref, q2_ref, scr, *, g_row):
    d = D_MODEL
    xn = _rms(x_ref[0].reshape(TOKEN_TILE, d), g_ref[g_row:g_row + 1, :]).astype(BF16)
    scale = ATTN_SCALE * LOG2E
    y0 = _dot(xn, w_ref[:, 0:d]) * scale
    q0_ref[0] = _by16_to_natural(scr, y0).astype(BF16)
    y1 = _dot(xn, w_ref[:, d:2 * d]) * scale
    q1_ref[0] = _by16_to_by4(scr, y1).astype(BF16).reshape(4, TOKEN_TILE // 4, d)
    y2 = _dot(xn, w_ref[:, 2 * d:3 * d]) * scale
    q2_ref[0] = y2.astype(BF16).reshape(R16, ROWS16, d)


def _q_streams(h16, g, g_row, wq):
    b, _, n16, d = h16.shape
    s = n16 * R16
    b_tiles = s // TOKEN_TILE
    return pl.pallas_call(
        functools.partial(_q_streams_body, g_row=g_row),
        grid=(b * b_tiles,),
        in_specs=[_tile16_spec(b_tiles, d), _member(g), _member(wq)],
        out_specs=[pl.BlockSpec((1, TOKEN_TILE, d), lambda i: (i // b_tiles, i % b_tiles, 0)),
                   _tile4_spec(b_tiles, d), _tile16_spec(b_tiles, d)],
        out_shape=[jax.ShapeDtypeStruct((b, s, d), BF16), jax.ShapeDtypeStruct((b, 4, s // 4, d), BF16),
                   jax.ShapeDtypeStruct((b, R16, n16, d), BF16)],
        scratch_shapes=[SLAB_SCRATCH],
        compiler_params=_params(1),
        name="q_streams",
    )(h16, g.arr, wq.arr)


def _oproj_body(o_ref, h_ref, g_ref, w_ref, out_ref, *, g_row):
    y = _dot(o_ref[...], w_ref[...])
    out_ref[...] = h_ref[...] + _rms(y, g_ref[g_row:g_row + 1, :])


def _out_proj(o, h, g, g_row, w):
    t, d = h.shape
    tm = _token_tile(t)
    tile = pl.BlockSpec((tm, d), lambda i: (i, 0))
    return pl.pallas_call(
        functools.partial(_oproj_body, g_row=g_row),
        grid=(t // tm,),
        in_specs=[tile, tile, _member(g), _member(w)],
        out_specs=tile,
        out_shape=jax.ShapeDtypeStruct((t, d), F32),
        compiler_params=_params(1),
        name="out_proj",
    )(o, h, g.arr, w.arr)


def _head_stat_lane(h):
    return (h % 2) * HEAD_DIM + h // 2


def _stat_expander():
    col_head = jnp.arange(D_MODEL) // HEAD_DIM
    row = jnp.arange(LANES)
    row_head = jnp.where(row % HEAD_DIM < N_HEADS // 2, 2 * (row % HEAD_DIM) + row // HEAD_DIM, -1)
    e = (row_head[:, None] == col_head[None, :]).astype(BF16)
    return jnp.concatenate([e, e], axis=0)


def _merge_oproj_body(o0_ref, l0_ref, o1_ref, l1_ref, o2_ref, l2_ref, h_ref, g_ref, w_ref, e_ref, out_ref, scr, *,
                      g_row):
    d = D_MODEL
    o0 = jnp.concatenate(_natural_to_by16(scr, o0_ref[0]), axis=0)
    l0 = jnp.concatenate(_natural_to_by16(scr, l0_ref[0]), axis=0)
    o1 = _by4_to_by16(scr, o1_ref[0].reshape(TOKEN_TILE, d))
    l1 = _by4_to_by16(scr, l1_ref[0].reshape(TOKEN_TILE, LANES))
    o2 = o2_ref[0].reshape(TOKEN_TILE, d)
    l2 = l2_ref[0].reshape(TOKEN_TILE, LANES)
    mm = jnp.maximum(jnp.maximum(l0, l1), l2)
    es = [jnp.exp(l - mm) for l in (l0, l1, l2)]
    den = es[0] + es[1] + es[2]
    o = None
    for e, og in zip(es, (o0, o1, o2)):
        wgt = e / den
        hi = wgt.astype(BF16)
        lo = (wgt - hi.astype(F32)).astype(BF16)
        term = _dot(jnp.concatenate([hi, lo], axis=1), e_ref[...]) * og
        o = term if o is None else o + term
    y = _dot(o.astype(BF16), w_ref[...])
    out_ref[0] = (h_ref[0].reshape(TOKEN_TILE, d) + _rms(y, g_ref[g_row:g_row + 1, :])).reshape(R16, ROWS16, d)


def _merge_out_proj(groups, h16, g, g_row, w):
    b, _, n16, d = h16.shape
    s = n16 * R16
    b_tiles = s // TOKEN_TILE
    (o0, l0), (o1, l1), (o2, l2) = groups
    expander = _stat_expander()

    def specs(c):
        return (pl.BlockSpec((1, TOKEN_TILE, c), lambda i: (i // b_tiles, i % b_tiles, 0)),
                _tile4_spec(b_tiles, c), _tile16_spec(b_tiles, c))

    (nat, p4, p16), (nat_s, p4_s, p16_s) = specs(d), specs(LANES)
    return pl.pallas_call(
        functools.partial(_merge_oproj_body, g_row=g_row),
        grid=(b * b_tiles,),
        in_specs=[nat, nat_s, p4, p4_s, p16, p16_s, p16, _member(g), _member(w), _resident(expander.shape)],
        out_specs=p16,
        out_shape=jax.ShapeDtypeStruct(h16.shape, F32),
        scratch_shapes=[SLAB_SCRATCH],
        compiler_params=_params(1),
        name="merge_out_proj",
    )(o0, l0, o1, l1, o2, l2, h16, g.arr, w.arr, expander)


def _pool_body(h_ref, prev_ref, g_ref, win_ref, wgrp_ref, scale_ref, wout_ref,
               out_ref, rows_ref, ext_ref, sa_ref, sb_ref, *, g_row, pos0, bb, ts):
    d = D_MODEL
    j = pl.program_id(1)
    top, end = POOL_TOP, POOL_TOP + ts

    @pl.when(j == 0)
    def _():
        ext_ref[:, 0:top - POOL_HALO, :] = jnp.zeros((bb, top - POOL_HALO, d), F32)
        ext_ref[:, top - POOL_HALO:top, :] = prev_ref[...]

    @pl.when(j > 0)
    def _():
        ext_ref[:, 0:top, :] = ext_ref[:, ts:end, :]

    h = h_ref[...].reshape(bb * ts, d)
    hn = _rms(h, g_ref[g_row:g_row + 1, :]).astype(BF16)
    u = _dot(hn, win_ref[...])
    ext_ref[:, top:end, :] = u.reshape(bb, ts, d)
    rows_ref[...] = ext_ref[:, end - POOL_HALO:end, :]

    def doubled(ref, cs, lo, shift):
        return ref[:, lo:end, cs] + ref[:, lo - shift:end - shift, cs]

    every = slice(None)
    pos = pos0 + j * ts + lax.broadcasted_iota(jnp.int32, (1, ts, 1), 1)
    zs = []
    for gi, win in enumerate(POOL_WINDOWS):
        cs = slice(gi * POOL_GROUP_DIM, (gi + 1) * POOL_GROUP_DIM)
        n_stage = win.bit_length() - 1
        src, src_cs = ext_ref, cs
        stage_refs = (sa_ref, sb_ref)
        for k in range(n_stage - 1):
            lo = top - 8 * (n_stage - 1 - k)
            dst = stage_refs[k % 2]
            dst[:, lo:end, :] = doubled(src, src_cs, lo, 1 << k)
            src, src_cs = dst, every
        tot = doubled(src, src_cs, top, win // 2)
        cur = ext_ref[:, top:end, cs]
        cnt = jnp.minimum(pos + 1, win).astype(F32)
        z = (tot / cnt - cur).reshape(bb * ts, POOL_GROUP_DIM).astype(BF16)
        zs.append(_dot(z, wgrp_ref[gi]))
    zc = (jnp.concatenate(zs, axis=-1) * scale_ref[...]).astype(BF16)
    y = _dot(zc, wout_ref[...])
    out_ref[...] = (h + _rms(y, g_ref[g_row + 1:g_row + 2, :])).reshape(bb, ts, d)


def _pool_mixer(h, prev, g, g_row, w_in, w_grp, scale, w_out, pos0):
    b, s, d = h.shape
    if s >= TOKEN_TILE:
        bb, ts = 1, TOKEN_TILE
    else:
        bb, ts = b, s
    n_t = s // ts
    assert s % ts == 0 and ts % 8 == 0 and (n_t == 1 or ts >= POOL_TOP)
    tile = pl.BlockSpec((bb, ts, d), lambda i, j: (i, j, 0))
    halo = pl.BlockSpec((bb, POOL_HALO, d), lambda i, j: (i, 0, 0))
    stage = pltpu.VMEM((bb, POOL_TOP + ts, POOL_GROUP_DIM), F32)
    return pl.pallas_call(
        functools.partial(_pool_body, g_row=g_row, pos0=pos0, bb=bb, ts=ts),
        grid=(b // bb, n_t),
        in_specs=[tile, halo, _member(g), _member(w_in), _member(w_grp), _member(scale), _member(w_out)],
        out_specs=[tile, halo],
        out_shape=[jax.ShapeDtypeStruct((b, s, d), F32),
                   jax.ShapeDtypeStruct((b, POOL_HALO, d), F32)],
        scratch_shapes=[pltpu.VMEM((bb, POOL_TOP + ts, d), F32), stage, stage],
        compiler_params=_params(2),
        name="pool_mixer",
    )(h, prev, g.arr, w_in.arr, w_grp.arr, scale.arr, w_out.arr)


def _alibi_slopes(n):
    return 2.0 ** (-8.0 * jnp.arange(1, n + 1, dtype=F32) / n)


def _band_table(dil, n_steps):
    qi = jnp.arange(BAND)[:, None]
    ki = jnp.arange(2 * BAND)[None, :]
    step = qi + BAND - ki
    valid = (step >= 0) & (step <= n_steps)
    bias = -_alibi_slopes(N_HEADS)[:, None, None] * (step * dil).astype(F32)[None]
    return jnp.where(valid[None], bias, NEG_INF)


def _pair_tables(dil, n_steps):
    tab = _band_table(dil, n_steps) * LOG2E
    both = tab.reshape(N_HEADS // 2, 2, BAND, 2 * BAND)
    with_prev = jnp.concatenate([both[:, 0], both[:, 1]], axis=-1)
    cur_only = jnp.concatenate([both[:, 0, :, BAND:], both[:, 1, :, BAND:]], axis=-1)
    return with_prev, cur_only


def _step_key_positions(n_past, t_new, n_keys, old_window):
    j = jnp.arange(n_keys)
    if old_window:
        return jnp.where(j < n_past + t_new, j, -1)
    pos = jnp.where(j < n_past, j + t_new, -1)
    return jnp.where((j >= n_past) & (j < n_past + t_new), j - n_past, pos)


def _step_table(n_past, t_new, n_keys, old_window):
    pos = _step_key_positions(n_past, t_new, n_keys, old_window)
    dist = n_past + jnp.arange(t_new)[:, None] - pos[None, :]
    held = (pos >= 0)[None, :]
    bias = -_alibi_slopes(N_HEADS)[:, None, None] * dist.astype(F32)[None]
    tabs = []
    for win, dil in DIL_GROUPS:
        valid = (dist >= 0) & (dist % dil == 0) & (dist <= win) & held
        tabs.append(jnp.where(valid[None], bias, NEG_INF))
    return jnp.stack(tabs, axis=1).reshape(N_HEADS * N_DIL * t_new, n_keys)


def _band_attn_body(*refs, chained):
    if chained:
        q_ref, kc_ref, vc_ref, kp_ref, vp_ref, tab2_ref, tab1_ref, o_ref, l_ref, s_ref, p_ref = refs
    else:
        q_ref, kc_ref, vc_ref, tab1_ref, o_ref, l_ref, s_ref, p_ref = refs
        kp_ref = vp_ref = tab2_ref = None
    n_pairs = N_HEADS // 2
    lead = (0,) * (len(q_ref.shape) - (2 if chained else 3))

    def rows(ref, sub):
        view = ref.at[lead]
        return view.at[pl.ds(sub * BAND, BAND)] if chained else view.at[sub]

    def attend(sub, k_prev, v_prev):
        q_v, kc_v, vc_v, o_v, l_v = (rows(r, sub) for r in (q_ref, kc_ref, vc_ref, o_ref, l_ref))
        s_v, p_v = s_ref.at[sub], p_ref.at[sub]
        with_prev = k_prev is not None
        nk = 2 * BAND if with_prev else BAND
        low = lax.broadcasted_iota(jnp.int32, (nk, LANES), 1) < HEAD_DIM
        row_low = lax.broadcasted_iota(jnp.int32, (2 * nk, LANES), 0) < nk
        lane_low = lax.broadcasted_iota(jnp.int32, (2 * nk, LANES), 1) < HEAD_DIM
        ones_st = jnp.where(row_low == lane_low, 1.0, 0.0).astype(BF16)
        lane = lax.broadcasted_iota(jnp.int32, (BAND, LANES), 1)
        q_low = lane < HEAD_DIM
        tab_ref = tab2_ref if with_prev else tab1_ref

        def stacked(c_v, p_v_, cs):
            x = c_v[:, cs]
            if with_prev:
                x = jnp.concatenate([p_v_[:, cs], x], axis=0)
            z = jnp.zeros_like(x)
            return jnp.concatenate([jnp.where(low, x, z), jnp.where(low, z, x)], axis=0)

        for j in range(n_pairs):
            cs = slice(j * LANES, (j + 1) * LANES)
            s_v[:, 2 * nk * j:2 * nk * (j + 1)] = _dot_nt(q_v[:, cs], stacked(kc_v, k_prev, cs)) + tab_ref[j]
        ms = []
        for h in range(N_HEADS):
            sh = s_v[:, nk * h:nk * (h + 1)]
            m = jnp.max(sh, axis=-1, keepdims=True)
            p_v[:, nk * h:nk * (h + 1)] = jnp.exp2(sh - m).astype(BF16)
            ms.append(m)
        stats = jnp.zeros((BAND, LANES), F32)
        for j in range(n_pairs):
            cs = slice(j * LANES, (j + 1) * LANES)
            aug = jnp.concatenate([stacked(vc_v, v_prev, cs), ones_st], axis=1)
            r = _dot(p_v[:, 2 * nk * j:2 * nk * (j + 1)], aug)
            l = r[:, LANES:]
            o_v[:, cs] = r[:, :LANES] / l
            m2 = jnp.where(q_low, jnp.broadcast_to(ms[2 * j], (BAND, LANES)),
                           jnp.broadcast_to(ms[2 * j + 1], (BAND, LANES)))
            stats = jnp.where((lane & (HEAD_DIM - 1)) == j, (m2 + jnp.log2(l)) * LN2, stats)
        l_v[...] = stats

    if chained:
        first = pl.program_id(2) == 0
        kp_v, vp_v = kp_ref.at[lead], vp_ref.at[lead]
        pl.when(first)(lambda: attend(0, None, None))
        pl.when(jnp.logical_not(first))(lambda: attend(0, kp_v, vp_v))
        for sub in range(1, ATTN_SUBS):
            attend(sub, rows(kc_ref, sub - 1), rows(vc_ref, sub - 1))
    else:
        for sub in range(ATTN_SUBS):
            attend(sub, None, None)


def _band_attn_group(q, kb, vb, gi):
    win, dil = DIL_GROUPS[gi]
    d = D_MODEL
    n = q.shape[-2]
    b = q.shape[0]
    chained = n > BAND
    step_rows = ATTN_SUBS * BAND

    def spec(c):
        if dil == 1:
            return (pl.BlockSpec((1, step_rows, c), lambda i, r, j: (i, j, 0)),
                    pl.BlockSpec((1, BAND, c), lambda i, r, j: (i, jnp.maximum(ATTN_SUBS * j - 1, 0), 0)))
        if chained:
            return (pl.BlockSpec((1, 1, step_rows, c), lambda i, r, j: (i, r, j, 0)),
                    pl.BlockSpec((1, 1, BAND, c), lambda i, r, j: (i, r, jnp.maximum(ATTN_SUBS * j - 1, 0), 0)))
        return pl.BlockSpec((1, ATTN_SUBS, BAND, c), lambda i, r, j: (i, r, 0, 0)), None

    if chained:
        assert n % step_rows == 0
        grid = (b, dil, n // step_rows)
    else:
        assert n == BAND and dil % ATTN_SUBS == 0
        grid = (b, dil // ATTN_SUBS, 1)
    cur, prv = spec(d)
    stat, _ = spec(LANES)
    tab2, tab1 = _pair_tables(dil, win // dil)
    in_specs = [cur, cur, cur]
    args = [q, kb, vb]
    if chained:
        in_specs += [prv, prv, _resident(tab2.shape)]
        args += [kb, vb, tab2]
    in_specs.append(_resident(tab1.shape))
    args.append(tab1)
    width = N_HEADS * 2 * BAND
    return pl.pallas_call(
        functools.partial(_band_attn_body, chained=chained),
        grid=grid,
        in_specs=in_specs,
        out_specs=[cur, stat],
        out_shape=[jax.ShapeDtypeStruct(q.shape, F32), jax.ShapeDtypeStruct(q.shape[:-1] + (LANES,), F32)],
        scratch_shapes=[pltpu.VMEM((ATTN_SUBS, BAND, width), F32), pltpu.VMEM((ATTN_SUBS, BAND, width), BF16)],
        compiler_params=_params(3),
        name=f"band_attn_g{gi}",
    )(*args)


STEP_HEADS = 4
STEP_COLS = STEP_HEADS * HEAD_DIM


def _step_attn_body(*refs, n_past, t_new, append):
    q_refs = refs[:N_DIL]
    if append:
        wk_ref, wv_ref, xk_ref, xv_ref, tab_ref, o_ref, ok_ref, ov_ref, kb_ref, vb_ref = refs[N_DIL:]
        outs = (ok_ref, ov_ref)
    else:
        wk_ref, wv_ref, xk_ref, xv_ref, tab_ref, o_ref, kb_ref, vb_ref = refs[N_DIL:]
        outs = (None, None)
    rows_g = N_DIL * t_new
    rows = STEP_HEADS * rows_g

    for w_ref, x_ref, out_ref, b_ref in zip((wk_ref, wv_ref), (xk_ref, xv_ref), outs, (kb_ref, vb_ref)):
        win, extra = w_ref[0], x_ref[0]
        b_ref[:, 0:n_past] = win.astype(BF16)
        b_ref[:, n_past:n_past + LANES] = extra.astype(BF16)
        if append:
            out_ref[0] = jnp.concatenate([win, extra], axis=1)[:, t_new:t_new + n_past]

    q_rows = jnp.concatenate([r[0] for r in q_refs], axis=0)
    q_all = jnp.concatenate([q_rows] * STEP_HEADS, axis=0)
    row_head = lax.broadcasted_iota(jnp.int32, (rows, STEP_COLS), 0) // rows_g
    col_head = lax.broadcasted_iota(jnp.int32, (rows, STEP_COLS), 1) // HEAD_DIM
    own = row_head == col_head
    q_bd = jnp.where(own, q_all, jnp.zeros_like(q_all))
    s = _dot(q_bd, kb_ref[...]) + tab_ref[...]
    m = jnp.max(s, axis=-1, keepdims=True)
    p = jnp.exp(s - m)
    l = jnp.sum(p, axis=-1, keepdims=True)
    o_full = _dot_nt(p.astype(BF16), vb_ref[...]) / l
    lse = m + jnp.log(l)

    out = jnp.zeros((t_new, STEP_COLS), F32)
    col = lax.broadcasted_iota(jnp.int32, (t_new, STEP_COLS), 1) // HEAD_DIM
    for hh in range(STEP_HEADS):
        base = hh * rows_g
        lses = [lse[base + g * t_new:base + (g + 1) * t_new] for g in range(N_DIL)]
        mm = functools.reduce(jnp.maximum, lses)
        ws = [jnp.exp(x - mm) for x in lses]
        den = functools.reduce(lambda a, b: a + b, ws)
        acc = functools.reduce(
            lambda a, b: a + b,
            [ws[g] * o_full[base + g * t_new:base + (g + 1) * t_new] for g in range(N_DIL)])
        out = jnp.where(col == hh, acc / den, out)
    o_ref[0] = out.astype(o_ref.dtype)


def _step_attention(q, win_k, win_v, extra_k, extra_v, append):
    b, t_new, _ = q.shape
    n_past = win_k.shape[2]
    d = D_MODEL
    assert t_new % 8 == 0 and t_new <= LANES and n_past % LANES == 0
    n_keys = n_past + LANES
    rows = STEP_HEADS * N_DIL * t_new
    tab = _step_table(n_past, t_new, n_keys, append)
    n_hg = N_HEADS // STEP_HEADS

    def q_spec(g):
        return pl.BlockSpec((1, t_new, STEP_COLS), lambda j, i: (i, 0, g * n_hg + j))

    win_spec = pl.BlockSpec((1, STEP_COLS, n_past), lambda j, i: (i, j, 0))
    extra_spec = pl.BlockSpec((1, STEP_COLS, LANES), lambda j, i: (i, j, 0))
    out_specs = [pl.BlockSpec((1, t_new, STEP_COLS), lambda j, i: (i, 0, j))]
    out_shape = [jax.ShapeDtypeStruct((b, t_new, d), BF16)]
    if append:
        out_specs += [win_spec, win_spec]
        out_shape += [jax.ShapeDtypeStruct((b, d, n_past), F32)] * 2
    return pl.pallas_call(
        functools.partial(_step_attn_body, n_past=n_past, t_new=t_new, append=append),
        grid=(n_hg, b),
        in_specs=[q_spec(g) for g in range(N_DIL)] + [win_spec, win_spec, extra_spec, extra_spec,
                                                     pl.BlockSpec((rows, n_keys), lambda j, i: (j, 0))],
        out_specs=out_specs,
        out_shape=out_shape,
        scratch_shapes=[pltpu.VMEM((STEP_COLS, n_keys), BF16)] * 2,
        compiler_params=_params(2),
        name="step_attn",
    )(q, q, q, win_k, win_v, extra_k, extra_v, tab)


G_FFN1, G_MIX, G_FFN2 = 0, 2, 4


def _ffn_w(w, layer, half):
    idx = (layer, half)
    return _Sel(w["ffn_w_gate"], idx), _Sel(w["ffn_w_up"], idx), _Sel(w["ffn_w_down"], idx)


def _pool_layers(h, b, s, pool_prev, pos0, w, n_a):
    d = D_MODEL
    pool_rows = []
    for layer in range(n_a):
        g = _Sel(w["norm_g"], (layer,))
        h = _ffn(h, g, G_FFN1, *_ffn_w(w, layer, 0))
        h3, rows = _pool_mixer(h.reshape(b, s, d), pool_prev[layer], g, G_MIX, _Sel(w["pool_w_in"], (layer,)),
                               _Sel(w["pool_w_grp"], (layer,)), _Sel(w["pool_scale"], (layer,)),
                               _Sel(w["pool_w_out"], (layer,)), pos0)
        h = h3.reshape(b * s, d)
        pool_rows.append(rows[:, POOL_HALO - POOL_BUF:])
        h = _ffn(h, g, G_FFN2, *_ffn_w(w, layer, 1))
    return h, jnp.stack(pool_rows)


def _prompt_trunk(x, w):
    b, s, d = x.shape
    t = b * s
    depth = w["norm_g"].shape[0]
    n_a = depth // 2
    assert s % (R16 * BAND) == 0 and s % TOKEN_TILE == 0
    empty = jnp.zeros((n_a, b, POOL_HALO, d), F32)
    h, pool_rows = _pool_layers(x.reshape(t, d), b, s, empty, 0, w, n_a)
    k32, v32, kn, vn, k4, v4, k16, v16 = _kv_streams(h, w["kv_norm"], w["w_k"], w["w_v"], b, s)
    kn, vn = kn.reshape(b, s, d), vn.reshape(b, s, d)
    for layer in range(n_a, depth):
        g = _Sel(w["norm_g"], (layer,))
        j = layer - n_a
        if layer == n_a:
            h = _ffn(h, g, G_FFN1, *_ffn_w(w, layer, 0), order_in="natural", order_out="by16", seq=s)
        else:
            h = _ffn(h, g, G_FFN1, *_ffn_w(w, layer, 0))
            h = h.reshape(b, R16, s // R16, d)
        q0, q1, q2 = _q_streams(h, g, G_MIX, _Sel(w["attn_w_q"], (j,)))
        groups = [_band_attn_group(q0, kn, vn, 0), _band_attn_group(q1, k4, v4, 1),
                  _band_attn_group(q2, k16, v16, 2)]
        h = _merge_out_proj(groups, h, g, G_MIX + 1, _Sel(w["attn_w_o"], (j,)))
        if layer == depth - 1:
            h = _ffn(h, g, G_FFN2, *_ffn_w(w, layer, 1), order_in="by16", order_out="natural", seq=s)
        else:
            h = _ffn(h.reshape(t, d), g, G_FFN2, *_ffn_w(w, layer, 1))
    buf = min(max(win for win, _ in DIL_GROUPS), s)
    k32, v32 = k32[:, s - buf:], v32[:, s - buf:]
    return h.reshape(b, s, d), pool_rows, k32, v32


def _kv_new_body(x_ref, g_ref, wk_ref, wv_ref, k_ref, v_ref):
    bb, t_new, d = x_ref.shape
    for i in range(bb):
        xn = _rms(x_ref[i], g_ref[...])
        xn = jnp.concatenate([xn, jnp.zeros((LANES - t_new, d), F32)], axis=0).astype(BF16)
        k_ref[i] = _dot_nt(wk_ref[...], xn)
        v_ref[i] = _dot_nt(wv_ref[...], xn)


def _kv_new(x, g, wk_t, wv_t):
    b, t_new, d = x.shape
    assert t_new % 8 == 0 and t_new <= LANES
    bb = 8 if b % 8 == 0 else 1
    out = jax.ShapeDtypeStruct((b, d, LANES), F32)
    out_spec = pl.BlockSpec((bb, d, LANES), lambda i: (i, 0, 0))
    return pl.pallas_call(
        _kv_new_body,
        grid=(b // bb,),
        in_specs=[pl.BlockSpec((bb, t_new, d), lambda i: (i, 0, 0)), _resident(g.shape), _resident(wk_t.shape),
                  _resident(wv_t.shape)],
        out_specs=[out_spec, out_spec],
        out_shape=[out, out],
        compiler_params=_params(1),
        name="kv_new",
    )(x, g, wk_t, wv_t)


def _sample_trunk(x, pool_prev, cache_kt, cache_vt, pos0, w):
    b, s, d = x.shape
    t = b * s
    depth = w["norm_g"].shape[0]
    n_a = depth // 2
    h, pool_rows = _pool_layers(x.reshape(t, d), b, s, pool_prev, pos0, w, n_a)
    new_kt, new_vt = _kv_new(h.reshape(b, s, d), w["kv_norm"], w["w_k"].T, w["w_v"].T)
    win_k, win_v = cache_kt, cache_vt
    for layer in range(n_a, depth):
        g = _Sel(w["norm_g"], (layer,))
        j = layer - n_a
        h = _ffn(h, g, G_FFN1, *_ffn_w(w, layer, 0))
        (q,) = _norm_proj(h, g, G_MIX, [_Sel(w["attn_w_q"], (j,))], out_cfg=((0, BF16, ATTN_SCALE),))
        q = q.reshape(b, s, N_DIL * d)
        if j == 0:
            o, win_k, win_v = _step_attention(q, win_k, win_v, new_kt, new_vt, append=True)
        else:
            (o,) = _step_attention(q, win_k, win_v, cache_kt, cache_vt, append=False)
        h = _out_proj(o.reshape(t, d), h, g, G_MIX + 1, _Sel(w["attn_w_o"], (j,)))
        h = _ffn(h, g, G_FFN2, *_ffn_w(w, layer, 1))
    return h.reshape(b, s, d), pool_rows, win_k, win_v


def kernel(x_prompt, x_sample, state_pool, cache_k, cache_v, norm_g, ffn_w_gate, ffn_w_up, ffn_w_down,
           pool_w_in, pool_w_grp, pool_scale, pool_w_out, kv_norm, w_k, w_v, attn_w_q, attn_w_o):
    d = D_MODEL
    w = dict(
        norm_g=norm_g,
        ffn_w_gate=ffn_w_gate.astype(BF16), ffn_w_up=ffn_w_up.astype(BF16), ffn_w_down=ffn_w_down.astype(BF16),
        pool_w_in=pool_w_in.astype(BF16), pool_w_grp=pool_w_grp.astype(BF16),
        pool_scale=pool_scale.reshape(-1, 1, d), pool_w_out=pool_w_out.astype(BF16),
        kv_norm=kv_norm.reshape(1, d), w_k=w_k.astype(BF16), w_v=w_v.astype(BF16),
        attn_w_q=attn_w_q.astype(BF16), attn_w_o=attn_w_o.astype(BF16),
    )
    bs, n_past = cache_k.shape[:2]
    y_p, pool_p, k_p, v_p = _prompt_trunk(x_prompt, w)

    prev_s = jnp.pad(state_pool, ((0, 0), (0, 0), (POOL_HALO - POOL_BUF, 0), (0, 0)))
    def feature_major(c):
        return jnp.transpose(c.reshape(bs, n_past, d), (0, 2, 1))

    y_s, pool_s, k_s, v_s = _sample_trunk(x_sample, prev_s, feature_major(cache_k), feature_major(cache_v),
                                          PAST_LEN, w)
    k_s, v_s = jnp.transpose(k_s, (0, 2, 1)), jnp.transpose(v_s, (0, 2, 1))
    hd = (N_HEADS, HEAD_DIM)
    return (y_p, y_s, pool_p,
            k_p.reshape(k_p.shape[:2] + hd), v_p.reshape(v_p.shape[:2] + hd),
            pool_s,
            k_s.reshape(k_s.shape[:2] + hd), v_s.reshape(v_s.shape[:2] + hd))
```

```python
import functools
import math
from typing import NamedTuple

import jax
import jax.numpy as jnp
from jax import lax
from jax.experimental import pallas as pl
from jax.experimental.pallas import tpu as pltpu

D_MODEL = 1024
D_FF = 2816
HEAD_DIM = 64
N_HEADS = D_MODEL // HEAD_DIM
POOL_WINDOWS = (2, 4, 8, 16)
POOL_GROUP_DIM = D_MODEL // len(POOL_WINDOWS)
POOL_BUF = max(POOL_WINDOWS) - 1
POOL_HALO = 16
POOL_TOP = 32
DIL_GROUPS = ((128, 1), (512, 4), (2048, 16))
N_DIL = len(DIL_GROUPS)
BAND = 128
ATTN_SUBS = 4
PAST_LEN = 8192
RMS_EPS = 1e-6
NEG_INF = -1e30
ATTN_SCALE = HEAD_DIM ** -0.5
LOG2E = 1.4426950408889634
LN2 = 0.6931471805599453
LANES = 128
N_SLABS = D_MODEL // LANES
TOKEN_TILE = 512
FFN_TILE = 512
FFN_SPLIT_ROWS = 256
R16 = 16
ROWS16 = TOKEN_TILE // R16
VMEM_LIMIT_BYTES = 56 * 1024 * 1024

F32 = jnp.float32
BF16 = jnp.bfloat16


def _params(n_axes):
    return pltpu.CompilerParams(dimension_semantics=("arbitrary",) * n_axes,
                                vmem_limit_bytes=VMEM_LIMIT_BYTES)


def _resident(shape):
    zeros = (0,) * len(shape)
    return pl.BlockSpec(shape, lambda *_: zeros, pipeline_mode=pl.Buffered(1))


class _Sel(NamedTuple):
    arr: jax.Array
    idx: tuple = ()


def _member(sel):
    lead = len(sel.idx)
    shape = sel.arr.shape[lead:]
    index = tuple(sel.idx) + (0,) * len(shape)
    return pl.BlockSpec((None,) * lead + shape, lambda *_: index, pipeline_mode=pl.Buffered(1))


def _rms(x, g):
    return x * lax.rsqrt(jnp.mean(x * x, axis=-1, keepdims=True) + RMS_EPS) * g


def _dot(a, b):
    return jnp.dot(a, b, preferred_element_type=F32)


def _dot_nt(a, b):
    return lax.dot_general(a, b, (((1,), (1,)), ((), ())), preferred_element_type=F32)


def _token_tile(t, largest=TOKEN_TILE):
    for tm in (largest, TOKEN_TILE, 256, 128, 64, 32, 16, 8):
        if t % tm == 0:
            return tm
    raise ValueError(f"token count {t} is not a multiple of 8")


SLAB_SCRATCH = pltpu.VMEM((N_SLABS, TOKEN_TILE, LANES), F32)


def _slabs_put(scr, val):
    for j in range(val.shape[-1] // LANES):
        scr[j, 0:val.shape[0], :] = val[:, j * LANES:(j + 1) * LANES]


def _slabs_get(scr, n_slabs, n_rows=TOKEN_TILE):
    return jnp.concatenate([scr[j, 0:n_rows, :] for j in range(n_slabs)], axis=-1)


def _slabs_gather(scr, n_slabs, start, n, stride):
    return jnp.concatenate([scr[j, pl.ds(start, n, stride=stride), :] for j in range(n_slabs)], axis=-1)


def _slabs_scatter(scr, start, n, stride, val):
    for j in range(val.shape[-1] // LANES):
        scr[j, pl.ds(start, n, stride=stride), :] = val[:, j * LANES:(j + 1) * LANES]


def _natural_to_by4(scr, val):
    n_slabs, m = val.shape[-1] // LANES, val.shape[0] // 4
    _slabs_put(scr, val)
    return jnp.concatenate([_slabs_gather(scr, n_slabs, r4, m, 4) for r4 in range(4)], axis=0)


def _by4_to_natural(scr, val):
    m = val.shape[0] // 4
    for r4 in range(4):
        _slabs_scatter(scr, r4, m, 4, val[r4 * m:(r4 + 1) * m])
    return _slabs_get(scr, val.shape[-1] // LANES, val.shape[0])


def _by4_to_by16(scr, val):
    n_slabs, n = val.shape[-1] // LANES, val.shape[0] // R16
    _slabs_put(scr, val)
    parts = [None] * R16
    for r4 in range(4):
        for c in range(4):
            parts[4 * c + r4] = _slabs_gather(scr, n_slabs, r4 * 4 * n + c, n, 4)
    return jnp.concatenate(parts, axis=0)


def _by16_to_by4(scr, val):
    n = val.shape[0] // R16
    for r4 in range(4):
        for c in range(4):
            r16 = 4 * c + r4
            _slabs_scatter(scr, r4 * 4 * n + c, n, 4, val[r16 * n:(r16 + 1) * n])
    return _slabs_get(scr, val.shape[-1] // LANES, val.shape[0])


def _natural_to_by16(scr, val):
    n = val.shape[0] // R16
    by16 = _by4_to_by16(scr, _natural_to_by4(scr, val))
    return [by16[r * n:(r + 1) * n] for r in range(R16)]


def _by16_to_natural(scr, val):
    return _by4_to_natural(scr, _by16_to_by4(scr, val))


def _tile16_spec(b_tiles, c):
    return pl.BlockSpec((1, R16, ROWS16, c), lambda i: (i // b_tiles, 0, i % b_tiles, 0))


def _tile4_spec(b_tiles, c):
    return pl.BlockSpec((1, 4, TOKEN_TILE // 4, c), lambda i: (i // b_tiles, 0, i % b_tiles, 0))


def _ffn_body(x_ref, g_ref, wg_ref, wu_ref, wd_ref, o_ref, *scratch, g_row, order_in, order_out):
    d = D_MODEL

    def half_step(x):
        xn = _rms(x, g_ref[g_row:g_row + 1, :]).astype(BF16)
        hg = _dot(xn, wg_ref[...])
        hu = _dot(xn, wu_ref[...])
        a = (hg / (1.0 + jnp.exp(-hg)) * hu).astype(BF16)
        f = _dot(a, wd_ref[...])
        return x + 0.5 * _rms(f, g_ref[g_row + 1:g_row + 2, :])

    n = math.prod(x_ref.shape) // d
    rows = FFN_SPLIT_ROWS if n % FFN_SPLIT_ROWS == 0 else n
    per_res = rows // R16
    for k in range(n // rows):
        if order_in == "by16":
            x = x_ref[0, :, k * per_res:(k + 1) * per_res, :].reshape(rows, d)
        else:
            x = x_ref[k * rows:(k + 1) * rows, :]
        y = half_step(x)
        if order_out == "by16" and order_in == "by16":
            o_ref[0, :, k * per_res:(k + 1) * per_res, :] = y.reshape(R16, per_res, d)
        elif order_out == "by16":
            for r, part in enumerate(_natural_to_by16(scratch[0], y)):
                o_ref[0, r, k * per_res:(k + 1) * per_res, :] = part
        elif order_in == "by16":
            o_ref[k * rows:(k + 1) * rows, :] = _by16_to_natural(scratch[0], y)
        else:
            o_ref[k * rows:(k + 1) * rows, :] = y


def _ffn(x, g, g_row, wg, wu, wd, order_in="natural", order_out="natural", seq=None):
    d = D_MODEL
    regroup = order_in != order_out
    if regroup:
        b_tiles = seq // TOKEN_TILE
        t = x.size // d
        tm = TOKEN_TILE
        nat = pl.BlockSpec((tm, d), lambda i: (i, 0))
        p16 = _tile16_spec(b_tiles, d)
        in_spec, out_spec = (nat, p16) if order_out == "by16" else (p16, nat)
        out_shape = (t // seq, R16, seq // R16, d) if order_out == "by16" else (t, d)
        scratch = [SLAB_SCRATCH]
    else:
        t = x.shape[0]
        tm = _token_tile(t, FFN_TILE)
        in_spec = out_spec = pl.BlockSpec((tm, d), lambda i: (i, 0))
        out_shape = (t, d)
        scratch = []
    return pl.pallas_call(
        functools.partial(_ffn_body, g_row=g_row, order_in=order_in, order_out=order_out),
        grid=(t // tm,),
        in_specs=[in_spec, _member(g), _member(wg), _member(wu), _member(wd)],
        out_specs=out_spec,
        out_shape=jax.ShapeDtypeStruct(out_shape, F32),
        scratch_shapes=scratch,
        compiler_params=_params(1),
        name="ffn",
    )(x, g.arr, wg.arr, wu.arr, wd.arr)


def _proj_body(x_ref, g_ref, *refs, g_row, out_cfg):
    n_w = max(wi for wi, _, _ in out_cfg) + 1
    w_refs, o_refs = refs[:n_w], refs[n_w:]
    xn = _rms(x_ref[...], g_ref[g_row:g_row + 1, :]).astype(BF16)
    ys = [_dot(xn, w[...]) for w in w_refs]
    for o_ref, (wi, _, scale) in zip(o_refs, out_cfg):
        y = ys[wi] if scale == 1.0 else ys[wi] * scale
        o_ref[...] = y.astype(o_ref.dtype)


def _norm_proj(x, g, g_row, ws, out_cfg):
    t, d = x.shape
    tm = _token_tile(t)
    tile = pl.BlockSpec((tm, d), lambda i: (i, 0))
    n_out = [w.arr.shape[-1] for w in ws]
    return pl.pallas_call(
        functools.partial(_proj_body, g_row=g_row, out_cfg=out_cfg),
        grid=(t // tm,),
        in_specs=[tile, _member(g)] + [_member(w) for w in ws],
        out_specs=[pl.BlockSpec((tm, n_out[wi]), lambda i: (i, 0)) for wi, _, _ in out_cfg],
        out_shape=[jax.ShapeDtypeStruct((t, n_out[wi]), dt) for wi, dt, _ in out_cfg],
        compiler_params=_params(1),
        name="norm_proj",
    )(x, g.arr, *[w.arr for w in ws])


def _kv_streams_body(x_ref, g_ref, wk_ref, wv_ref, k32_ref, v32_ref, kn_ref, vn_ref, k4_ref, v4_ref,
                     k16_ref, v16_ref, scr):
    xn = _rms(x_ref[...], g_ref[...]).astype(BF16)
    per_r4 = TOKEN_TILE // 4
    for w_ref, o32, on, o4, o16 in ((wk_ref, k32_ref, kn_ref, k4_ref, k16_ref),
                                    (wv_ref, v32_ref, vn_ref, v4_ref, v16_ref)):
        y = _dot(xn, w_ref[...])
        o32[0] = y
        on[...] = y.astype(BF16)
        y4 = _natural_to_by4(scr, y)
        o4[0] = y4.astype(BF16).reshape(4, per_r4, D_MODEL)
        o16[0] = _by4_to_by16(scr, y4).astype(BF16).reshape(R16, ROWS16, D_MODEL)


def _kv_streams(x, g, wk, wv, b, s):
    t, d = x.shape
    b_tiles = s // TOKEN_TILE
    nat = pl.BlockSpec((TOKEN_TILE, d), lambda i: (i, 0))
    f32n = jax.ShapeDtypeStruct((b, s, d), F32)
    nat3 = pl.BlockSpec((1, TOKEN_TILE, d), lambda i: (i // b_tiles, i % b_tiles, 0))
    bfn = jax.ShapeDtypeStruct((t, d), BF16)
    bf4 = jax.ShapeDtypeStruct((b, 4, s // 4, d), BF16)
    bf16_ = jax.ShapeDtypeStruct((b, R16, s // R16, d), BF16)
    return pl.pallas_call(
        _kv_streams_body,
        grid=(t // TOKEN_TILE,),
        in_specs=[nat, _resident(g.shape), _resident(wk.shape), _resident(wv.shape)],
        out_specs=[nat3, nat3, nat, nat, _tile4_spec(b_tiles, d), _tile4_spec(b_tiles, d),
                   _tile16_spec(b_tiles, d), _tile16_spec(b_tiles, d)],
        out_shape=[f32n, f32n, bfn, bfn, bf4, bf4, bf16_, bf16_],
        scratch_shapes=[SLAB_SCRATCH],
        compiler_params=_params(1),
        name="kv_streams",
    )(x, g, wk, wv)


def _q_streams_body(x_ref, g_ref, w_ref, q0_ref, q1_ref, q2_ref, scr, *, g_row):
    d = D_MODEL
    xn = _rms(x_ref[0].reshape(TOKEN_TILE, d), g_ref[g_row:g_row + 1, :]).astype(BF16)
    scale = ATTN_SCALE * LOG2E
    y0 = _dot(xn, w_ref[:, 0:d]) * scale
    q0_ref[0] = _by16_to_natural(scr, y0).astype(BF16)
    y1 = _dot(xn, w_ref[:, d:2 * d]) * scale
    q1_ref[0] = _by16_to_by4(scr, y1).astype(BF16).reshape(4, TOKEN_TILE // 4, d)
    y2 = _dot(xn, w_ref[:, 2 * d:3 * d]) * scale
    q2_ref[0] = y2.astype(BF16).reshape(R16, ROWS16, d)


def _q_streams(h16, g, g_row, wq):
    b, _, n16, d = h16.shape
    s = n16 * R16
    b_tiles = s // TOKEN_TILE
    return pl.pallas_call(
        functools.partial(_q_streams_body, g_row=g_row),
        grid=(b * b_tiles,),
        in_specs=[_tile16_spec(b_tiles, d), _member(g), _member(wq)],
        out_specs=[pl.BlockSpec((1, TOKEN_TILE, d), lambda i: (i // b_tiles, i % b_tiles, 0)),
                   _tile4_spec(b_tiles, d), _tile16_spec(b_tiles, d)],
        out_shape=[jax.ShapeDtypeStruct((b, s, d), BF16), jax.ShapeDtypeStruct((b, 4, s // 4, d), BF16),
                   jax.ShapeDtypeStruct((b, R16, n16, d), BF16)],
        scratch_shapes=[SLAB_SCRATCH],
        compiler_params=_params(1),
        name="q_streams",
    )(h16, g.arr, wq.arr)


def _oproj_body(o_ref, h_ref, g_ref, w_ref, out_ref, *, g_row):
    y = _dot(o_ref[...], w_ref[...])
    out_ref[...] = h_ref[...] + _rms(y, g_ref[g_row:g_row + 1, :])


def _out_proj(o, h, g, g_row, w):
    t, d = h.shape
    tm = _token_tile(t)
    tile = pl.BlockSpec((tm, d), lambda i: (i, 0))
    return pl.pallas_call(
        functools.partial(_oproj_body, g_row=g_row),
        grid=(t // tm,),
        in_specs=[tile, tile, _member(g), _member(w)],
        out_specs=tile,
        out_shape=jax.ShapeDtypeStruct((t, d), F32),
        compiler_params=_params(1),
        name="out_proj",
    )(o, h, g.arr, w.arr)


def _head_stat_lane(h):
    return (h % 2) * HEAD_DIM + h // 2


def _stat_expander():
    col_head = jnp.arange(D_MODEL) // HEAD_DIM
    row = jnp.arange(LANES)
    row_head = jnp.where(row % HEAD_DIM < N_HEADS // 2, 2 * (row % HEAD_DIM) + row // HEAD_DIM, -1)
    e = (row_head[:, None] == col_head[None, :]).astype(BF16)
    return jnp.concatenate([e, e], axis=0)


def _merge_oproj_body(o0_ref, l0_ref, o1_ref, l1_ref, o2_ref, l2_ref, h_ref, g_ref, w_ref, e_ref, out_ref, scr, *,
                      g_row):
    d = D_MODEL
    o0 = jnp.concatenate(_natural_to_by16(scr, o0_ref[0]), axis=0)
    l0 = jnp.concatenate(_natural_to_by16(scr, l0_ref[0]), axis=0)
    o1 = _by4_to_by16(scr, o1_ref[0].reshape(TOKEN_TILE, d))
    l1 = _by4_to_by16(scr, l1_ref[0].reshape(TOKEN_TILE, LANES))
    o2 = o2_ref[0].reshape(TOKEN_TILE, d)
    l2 = l2_ref[0].reshape(TOKEN_TILE, LANES)
    mm = jnp.maximum(jnp.maximum(l0, l1), l2)
    es = [jnp.exp(l - mm) for l in (l0, l1, l2)]
    den = es[0] + es[1] + es[2]
    o = None
    for e, og in zip(es, (o0, o1, o2)):
        wgt = e / den
        hi = wgt.astype(BF16)
        lo = (wgt - hi.astype(F32)).astype(BF16)
        term = _dot(jnp.concatenate([hi, lo], axis=1), e_ref[...]) * og
        o = term if o is None else o + term
    y = _dot(o.astype(BF16), w_ref[...])
    out_ref[0] = (h_ref[0].reshape(TOKEN_TILE, d) + _rms(y, g_ref[g_row:g_row + 1, :])).reshape(R16, ROWS16, d)


def _merge_out_proj(groups, h16, g, g_row, w):
    b, _, n16, d = h16.shape
    s = n16 * R16
    b_tiles = s // TOKEN_TILE
    (o0, l0), (o1, l1), (o2, l2) = groups
    expander = _stat_expander()

    def specs(c):
        return (pl.BlockSpec((1, TOKEN_TILE, c), lambda i: (i // b_tiles, i % b_tiles, 0)),
                _tile4_spec(b_tiles, c), _tile16_spec(b_tiles, c))

    (nat, p4, p16), (nat_s, p4_s, p16_s) = specs(d), specs(LANES)
    return pl.pallas_call(
        functools.partial(_merge_oproj_body, g_row=g_row),
        grid=(b * b_tiles,),
        in_specs=[nat, nat_s, p4, p4_s, p16, p16_s, p16, _member(g), _member(w), _resident(expander.shape)],
        out_specs=p16,
        out_shape=jax.ShapeDtypeStruct(h16.shape, F32),
        scratch_shapes=[SLAB_SCRATCH],
        compiler_params=_params(1),
        name="merge_out_proj",
    )(o0, l0, o1, l1, o2, l2, h16, g.arr, w.arr, expander)


def _pool_body(h_ref, prev_ref, g_ref, win_ref, wgrp_ref, scale_ref, wout_ref,
               out_ref, rows_ref, ext_ref, sa_ref, sb_ref, *, g_row, pos0, bb, ts):
    d = D_MODEL
    j = pl.program_id(1)
    top, end = POOL_TOP, POOL_TOP + ts

    @pl.when(j == 0)
    def _():
        ext_ref[:, 0:top - POOL_HALO, :] = jnp.zeros((bb, top - POOL_HALO, d), F32)
        ext_ref[:, top - POOL_HALO:top, :] = prev_ref[...]

    @pl.when(j > 0)
    def _():
        ext_ref[:, 0:top, :] = ext_ref[:, ts:end, :]

    h = h_ref[...].reshape(bb * ts, d)
    hn = _rms(h, g_ref[g_row:g_row + 1, :]).astype(BF16)
    u = _dot(hn, win_ref[...])
    ext_ref[:, top:end, :] = u.reshape(bb, ts, d)
    rows_ref[...] = ext_ref[:, end - POOL_HALO:end, :]

    def doubled(ref, cs, lo, shift):
        return ref[:, lo:end, cs] + ref[:, lo - shift:end - shift, cs]

    every = slice(None)
    pos = pos0 + j * ts + lax.broadcasted_iota(jnp.int32, (1, ts, 1), 1)
    zs = []
    for gi, win in enumerate(POOL_WINDOWS):
        cs = slice(gi * POOL_GROUP_DIM, (gi + 1) * POOL_GROUP_DIM)
        n_stage = win.bit_length() - 1
        src, src_cs = ext_ref, cs
        stage_refs = (sa_ref, sb_ref)
        for k in range(n_stage - 1):
            lo = top - 8 * (n_stage - 1 - k)
            dst = stage_refs[k % 2]
            dst[:, lo:end, :] = doubled(src, src_cs, lo, 1 << k)
            src, src_cs = dst, every
        tot = doubled(src, src_cs, top, win // 2)
        cur = ext_ref[:, top:end, cs]
        cnt = jnp.minimum(pos + 1, win).astype(F32)
        z = (tot / cnt - cur).reshape(bb * ts, POOL_GROUP_DIM).astype(BF16)
        zs.append(_dot(z, wgrp_ref[gi]))
    zc = (jnp.concatenate(zs, axis=-1) * scale_ref[...]).astype(BF16)
    y = _dot(zc, wout_ref[...])
    out_ref[...] = (h + _rms(y, g_ref[g_row + 1:g_row + 2, :])).reshape(bb, ts, d)


def _pool_mixer(h, prev, g, g_row, w_in, w_grp, scale, w_out, pos0):
    b, s, d = h.shape
    if s >= TOKEN_TILE:
        bb, ts = 1, TOKEN_TILE
    else:
        bb, ts = b, s
    n_t = s // ts
    assert s % ts == 0 and ts % 8 == 0 and (n_t == 1 or ts >= POOL_TOP)
    tile = pl.BlockSpec((bb, ts, d), lambda i, j: (i, j, 0))
    halo = pl.BlockSpec((bb, POOL_HALO, d), lambda i, j: (i, 0, 0))
    stage = pltpu.VMEM((bb, POOL_TOP + ts, POOL_GROUP_DIM), F32)
    return pl.pallas_call(
        functools.partial(_pool_body, g_row=g_row, pos0=pos0, bb=bb, ts=ts),
        grid=(b // bb, n_t),
        in_specs=[tile, halo, _member(g), _member(w_in), _member(w_grp), _member(scale), _member(w_out)],
        out_specs=[tile, halo],
        out_shape=[jax.ShapeDtypeStruct((b, s, d), F32),
                   jax.ShapeDtypeStruct((b, POOL_HALO, d), F32)],
        scratch_shapes=[pltpu.VMEM((bb, POOL_TOP + ts, d), F32), stage, stage],
        compiler_params=_params(2),
        name="pool_mixer",
    )(h, prev, g.arr, w_in.arr, w_grp.arr, scale.arr, w_out.arr)


def _alibi_slopes(n):
    return 2.0 ** (-8.0 * jnp.arange(1, n + 1, dtype=F32) / n)


def _band_table(dil, n_steps):
    qi = jnp.arange(BAND)[:, None]
    ki = jnp.arange(2 * BAND)[None, :]
    step = qi + BAND - ki
    valid = (step >= 0) & (step <= n_steps)
    bias = -_alibi_slopes(N_HEADS)[:, None, None] * (step * dil).astype(F32)[None]
    return jnp.where(valid[None], bias, NEG_INF)


def _pair_tables(dil, n_steps):
    tab = _band_table(dil, n_steps) * LOG2E
    both = tab.reshape(N_HEADS // 2, 2, BAND, 2 * BAND)
    with_prev = jnp.concatenate([both[:, 0], both[:, 1]], axis=-1)
    cur_only = jnp.concatenate([both[:, 0, :, BAND:], both[:, 1, :, BAND:]], axis=-1)
    return with_prev, cur_only


def _step_key_positions(n_past, t_new, n_keys, old_window):
    j = jnp.arange(n_keys)
    if old_window:
        return jnp.where(j < n_past + t_new, j, -1)
    pos = jnp.where(j < n_past, j + t_new, -1)
    return jnp.where((j >= n_past) & (j < n_past + t_new), j - n_past, pos)


def _step_table(n_past, t_new, n_keys, old_window):
    pos = _step_key_positions(n_past, t_new, n_keys, old_window)
    dist = n_past + jnp.arange(t_new)[:, None] - pos[None, :]
    held = (pos >= 0)[None, :]
    bias = -_alibi_slopes(N_HEADS)[:, None, None] * dist.astype(F32)[None]
    tabs = []
    for win, dil in DIL_GROUPS:
        valid = (dist >= 0) & (dist % dil == 0) & (dist <= win) & held
        tabs.append(jnp.where(valid[None], bias, NEG_INF))
    return jnp.stack(tabs, axis=1).reshape(N_HEADS * N_DIL * t_new, n_keys)


def _band_attn_body(*refs, chained):
    if chained:
        q_ref, kc_ref, vc_ref, kp_ref, vp_ref, tab2_ref, tab1_ref, o_ref, l_ref, s_ref, p_ref = refs
    else:
        q_ref, kc_ref, vc_ref, tab1_ref, o_ref, l_ref, s_ref, p_ref = refs
        kp_ref = vp_ref = tab2_ref = None
    n_pairs = N_HEADS // 2
    lead = (0,) * (len(q_ref.shape) - (2 if chained else 3))

    def rows(ref, sub):
        view = ref.at[lead]
        return view.at[pl.ds(sub * BAND, BAND)] if chained else view.at[sub]

    def attend(sub, k_prev, v_prev):
        q_v, kc_v, vc_v, o_v, l_v = (rows(r, sub) for r in (q_ref, kc_ref, vc_ref, o_ref, l_ref))
        s_v, p_v = s_ref.at[sub], p_ref.at[sub]
        with_prev = k_prev is not None
        nk = 2 * BAND if with_prev else BAND
        low = lax.broadcasted_iota(jnp.int32, (nk, LANES), 1) < HEAD_DIM
        row_low = lax.broadcasted_iota(jnp.int32, (2 * nk, LANES), 0) < nk
        lane_low = lax.broadcasted_iota(jnp.int32, (2 * nk, LANES), 1) < HEAD_DIM
        ones_st = jnp.where(row_low == lane_low, 1.0, 0.0).astype(BF16)
        lane = lax.broadcasted_iota(jnp.int32, (BAND, LANES), 1)
        q_low = lane < HEAD_DIM
        tab_ref = tab2_ref if with_prev else tab1_ref

        def stacked(c_v, p_v_, cs):
            x = c_v[:, cs]
            if with_prev:
                x = jnp.concatenate([p_v_[:, cs], x], axis=0)
            z = jnp.zeros_like(x)
            return jnp.concatenate([jnp.where(low, x, z), jnp.where(low, z, x)], axis=0)

        for j in range(n_pairs):
            cs = slice(j * LANES, (j + 1) * LANES)
            s_v[:, 2 * nk * j:2 * nk * (j + 1)] = _dot_nt(q_v[:, cs], stacked(kc_v, k_prev, cs)) + tab_ref[j]
        ms = []
        for h in range(N_HEADS):
            sh = s_v[:, nk * h:nk * (h + 1)]
            m = jnp.max(sh, axis=-1, keepdims=True)
            p_v[:, nk * h:nk * (h + 1)] = jnp.exp2(sh - m).astype(BF16)
            ms.append(m)
        stats = jnp.zeros((BAND, LANES), F32)
        for j in range(n_pairs):
            cs = slice(j * LANES, (j + 1) * LANES)
            aug = jnp.concatenate([stacked(vc_v, v_prev, cs), ones_st], axis=1)
            r = _dot(p_v[:, 2 * nk * j:2 * nk * (j + 1)], aug)
            l = r[:, LANES:]
            o_v[:, cs] = r[:, :LANES] / l
            m2 = jnp.where(q_low, jnp.broadcast_to(ms[2 * j], (BAND, LANES)),
                           jnp.broadcast_to(ms[2 * j + 1], (BAND, LANES)))
            stats = jnp.where((lane & (HEAD_DIM - 1)) == j, (m2 + jnp.log2(l)) * LN2, stats)
        l_v[...] = stats

    if chained:
        first = pl.program_id(2) == 0
        kp_v, vp_v = kp_ref.at[lead], vp_ref.at[lead]
        pl.when(first)(lambda: attend(0, None, None))
        pl.when(jnp.logical_not(first))(lambda: attend(0, kp_v, vp_v))
        for sub in range(1, ATTN_SUBS):
            attend(sub, rows(kc_ref, sub - 1), rows(vc_ref, sub - 1))
    else:
        for sub in range(ATTN_SUBS):
            attend(sub, None, None)


def _band_attn_group(q, kb, vb, gi):
    win, dil = DIL_GROUPS[gi]
    d = D_MODEL
    n = q.shape[-2]
    b = q.shape[0]
    chained = n > BAND
    step_rows = ATTN_SUBS * BAND

    def spec(c):
        if dil == 1:
            return (pl.BlockSpec((1, step_rows, c), lambda i, r, j: (i, j, 0)),
                    pl.BlockSpec((1, BAND, c), lambda i, r, j: (i, jnp.maximum(ATTN_SUBS * j - 1, 0), 0)))
        if chained:
            return (pl.BlockSpec((1, 1, step_rows, c), lambda i, r, j: (i, r, j, 0)),
                    pl.BlockSpec((1, 1, BAND, c), lambda i, r, j: (i, r, jnp.maximum(ATTN_SUBS * j - 1, 0), 0)))
        return pl.BlockSpec((1, ATTN_SUBS, BAND, c), lambda i, r, j: (i, r, 0, 0)), None

    if chained:
        assert n % step_rows == 0
        grid = (b, dil, n // step_rows)
    else:
        assert n == BAND and dil % ATTN_SUBS == 0
        grid = (b, dil // ATTN_SUBS, 1)
    cur, prv = spec(d)
    stat, _ = spec(LANES)
    tab2, tab1 = _pair_tables(dil, win // dil)
    in_specs = [cur, cur, cur]
    args = [q, kb, vb]
    if chained:
        in_specs += [prv, prv, _resident(tab2.shape)]
        args += [kb, vb, tab2]
    in_specs.append(_resident(tab1.shape))
    args.append(tab1)
    width = N_HEADS * 2 * BAND
    return pl.pallas_call(
        functools.partial(_band_attn_body, chained=chained),
        grid=grid,
        in_specs=in_specs,
        out_specs=[cur, stat],
        out_shape=[jax.ShapeDtypeStruct(q.shape, F32), jax.ShapeDtypeStruct(q.shape[:-1] + (LANES,), F32)],
        scratch_shapes=[pltpu.VMEM((ATTN_SUBS, BAND, width), F32), pltpu.VMEM((ATTN_SUBS, BAND, width), BF16)],
        compiler_params=_params(3),
        name=f"band_attn_g{gi}",
    )(*args)


STEP_HEADS = 4
STEP_COLS = STEP_HEADS * HEAD_DIM


def _step_attn_body(*refs, n_past, t_new, append):
    q_refs = refs[:N_DIL]
    if append:
        wk_ref, wv_ref, xk_ref, xv_ref, tab_ref, o_ref, ok_ref, ov_ref, kb_ref, vb_ref = refs[N_DIL:]
        outs = (ok_ref, ov_ref)
    else:
        wk_ref, wv_ref, xk_ref, xv_ref, tab_ref, o_ref, kb_ref, vb_ref = refs[N_DIL:]
        outs = (None, None)
    rows_g = N_DIL * t_new
    rows = STEP_HEADS * rows_g

    for w_ref, x_ref, out_ref, b_ref in zip((wk_ref, wv_ref), (xk_ref, xv_ref), outs, (kb_ref, vb_ref)):
        win, extra = w_ref[0], x_ref[0]
        b_ref[:, 0:n_past] = win.astype(BF16)
        b_ref[:, n_past:n_past + LANES] = extra.astype(BF16)
        if append:
            out_ref[0] = jnp.concatenate([win, extra], axis=1)[:, t_new:t_new + n_past]

    q_rows = jnp.concatenate([r[0] for r in q_refs], axis=0)
    q_all = jnp.concatenate([q_rows] * STEP_HEADS, axis=0)
    row_head = lax.broadcasted_iota(jnp.int32, (rows, STEP_COLS), 0) // rows_g
    col_head = lax.broadcasted_iota(jnp.int32, (rows, STEP_COLS), 1) // HEAD_DIM
    own = row_head == col_head
    q_bd = jnp.where(own, q_all, jnp.zeros_like(q_all))
    s = _dot(q_bd, kb_ref[...]) + tab_ref[...]
    m = jnp.max(s, axis=-1, keepdims=True)
    p = jnp.exp(s - m)
    l = jnp.sum(p, axis=-1, keepdims=True)
    o_full = _dot_nt(p.astype(BF16), vb_ref[...]) / l
    lse = m + jnp.log(l)

    out = jnp.zeros((t_new, STEP_COLS), F32)
    col = lax.broadcasted_iota(jnp.int32, (t_new, STEP_COLS), 1) // HEAD_DIM
    for hh in range(STEP_HEADS):
        base = hh * rows_g
        lses = [lse[base + g * t_new:base + (g + 1) * t_new] for g in range(N_DIL)]
        mm = functools.reduce(jnp.maximum, lses)
        ws = [jnp.exp(x - mm) for x in lses]
        den = functools.reduce(lambda a, b: a + b, ws)
        acc = functools.reduce(
            lambda a, b: a + b,
            [ws[g] * o_full[base + g * t_new:base + (g + 1) * t_new] for g in range(N_DIL)])
        out = jnp.where(col == hh, acc / den, out)
    o_ref[0] = out.astype(o_ref.dtype)


def _step_attention(q, win_k, win_v, extra_k, extra_v, append):
    b, t_new, _ = q.shape
    n_past = win_k.shape[2]
    d = D_MODEL
    assert t_new % 8 == 0 and t_new <= LANES and n_past % LANES == 0
    n_keys = n_past + LANES
    rows = STEP_HEADS * N_DIL * t_new
    tab = _step_table(n_past, t_new, n_keys, append)
    n_hg = N_HEADS // STEP_HEADS

    def q_spec(g):
        return pl.BlockSpec((1, t_new, STEP_COLS), lambda j, i: (i, 0, g * n_hg + j))

    win_spec = pl.BlockSpec((1, STEP_COLS, n_past), lambda j, i: (i, j, 0))
    extra_spec = pl.BlockSpec((1, STEP_COLS, LANES), lambda j, i: (i, j, 0))
    out_specs = [pl.BlockSpec((1, t_new, STEP_COLS), lambda j, i: (i, 0, j))]
    out_shape = [jax.ShapeDtypeStruct((b, t_new, d), BF16)]
    if append:
        out_specs += [win_spec, win_spec]
        out_shape += [jax.ShapeDtypeStruct((b, d, n_past), F32)] * 2
    return pl.pallas_call(
        functools.partial(_step_attn_body, n_past=n_past, t_new=t_new, append=append),
        grid=(n_hg, b),
        in_specs=[q_spec(g) for g in range(N_DIL)] + [win_spec, win_spec, extra_spec, extra_spec,
                                                     pl.BlockSpec((rows, n_keys), lambda j, i: (j, 0))],
        out_specs=out_specs,
        out_shape=out_shape,
        scratch_shapes=[pltpu.VMEM((STEP_COLS, n_keys), BF16)] * 2,
        compiler_params=_params(2),
        name="step_attn",
    )(q, q, q, win_k, win_v, extra_k, extra_v, tab)


G_FFN1, G_MIX, G_FFN2 = 0, 2, 4


def _ffn_w(w, layer, half):
    idx = (layer, half)
    return _Sel(w["ffn_w_gate"], idx), _Sel(w["ffn_w_up"], idx), _Sel(w["ffn_w_down"], idx)


def _pool_layers(h, b, s, pool_prev, pos0, w, n_a):
    d = D_MODEL
    pool_rows = []
    for layer in range(n_a):
        g = _Sel(w["norm_g"], (layer,))
        h = _ffn(h, g, G_FFN1, *_ffn_w(w, layer, 0))
        h3, rows = _pool_mixer(h.reshape(b, s, d), pool_prev[layer], g, G_MIX, _Sel(w["pool_w_in"], (layer,)),
                               _Sel(w["pool_w_grp"], (layer,)), _Sel(w["pool_scale"], (layer,)),
                               _Sel(w["pool_w_out"], (layer,)), pos0)
        h = h3.reshape(b * s, d)
        pool_rows.append(rows[:, POOL_HALO - POOL_BUF:])
        h = _ffn(h, g, G_FFN2, *_ffn_w(w, layer, 1))
    return h, jnp.stack(pool_rows)


def _prompt_trunk(x, w):
    b, s, d = x.shape
    t = b * s
    depth = w["norm_g"].shape[0]
    n_a = depth // 2
    assert s % (R16 * BAND) == 0 and s % TOKEN_TILE == 0
    empty = jnp.zeros((n_a, b, POOL_HALO, d), F32)
    h, pool_rows = _pool_layers(x.reshape(t, d), b, s, empty, 0, w, n_a)
    k32, v32, kn, vn, k4, v4, k16, v16 = _kv_streams(h, w["kv_norm"], w["w_k"], w["w_v"], b, s)
    kn, vn = kn.reshape(b, s, d), vn.reshape(b, s, d)
    for layer in range(n_a, depth):
        g = _Sel(w["norm_g"], (layer,))
        j = layer - n_a
        if layer == n_a:
            h = _ffn(h, g, G_FFN1, *_ffn_w(w, layer, 0), order_in="natural", order_out="by16", seq=s)
        else:
            h = _ffn(h, g, G_FFN1, *_ffn_w(w, layer, 0))
            h = h.reshape(b, R16, s // R16, d)
        q0, q1, q2 = _q_streams(h, g, G_MIX, _Sel(w["attn_w_q"], (j,)))
        groups = [_band_attn_group(q0, kn, vn, 0), _band_attn_group(q1, k4, v4, 1),
                  _band_attn_group(q2, k16, v16, 2)]
        h = _merge_out_proj(groups, h, g, G_MIX + 1, _Sel(w["attn_w_o"], (j,)))
        if layer == depth - 1:
            h = _ffn(h, g, G_FFN2, *_ffn_w(w, layer, 1), order_in="by16", order_out="natural", seq=s)
        else:
            h = _ffn(h.reshape(t, d), g, G_FFN2, *_ffn_w(w, layer, 1))
    buf = min(max(win for win, _ in DIL_GROUPS), s)
    k32, v32 = k32[:, s - buf:], v32[:, s - buf:]
    return h.reshape(b, s, d), pool_rows, k32, v32


def _kv_new_body(x_ref, g_ref, wk_ref, wv_ref, k_ref, v_ref, *, t_new):
    per_block = LANES // t_new
    xn = _rms(x_ref[...], g_ref[...]).astype(BF16)
    first = lax.broadcasted_iota(jnp.int32, (D_MODEL, LANES), 1) < t_new
    for w_ref, o_ref in ((wk_ref, k_ref), (wv_ref, v_ref)):
        y = _dot_nt(w_ref[...], xn)
        for i in range(per_block):
            moved = y if i == 0 else pltpu.roll(y, LANES - i * t_new, 1)
            o_ref[i] = jnp.where(first, moved, 0.0)


def _kv_new(x, g, wk_t, wv_t):
    b, t_new, d = x.shape
    assert LANES % t_new == 0
    bb = LANES // t_new
    n_steps = -(-b // bb)
    rows = x.reshape(b * t_new, d)
    if n_steps * bb != b:
        rows = jnp.pad(rows, ((0, (n_steps * bb - b) * t_new), (0, 0)))
    out = jax.ShapeDtypeStruct((n_steps * bb, d, LANES), F32)
    out_spec = pl.BlockSpec((bb, d, LANES), lambda i: (i, 0, 0))
    k_t, v_t = pl.pallas_call(
        functools.partial(_kv_new_body, t_new=t_new),
        grid=(n_steps,),
        in_specs=[pl.BlockSpec((LANES, d), lambda i: (i, 0)), _resident(g.shape), _resident(wk_t.shape),
                  _resident(wv_t.shape)],
        out_specs=[out_spec, out_spec],
        out_shape=[out, out],
        compiler_params=_params(1),
        name="kv_new",
    )(rows, g, wk_t, wv_t)
    return k_t[:b], v_t[:b]


def _sample_trunk(x, pool_prev, cache_kt, cache_vt, pos0, w):
    b, s, d = x.shape
    t = b * s
    depth = w["norm_g"].shape[0]
    n_a = depth // 2
    h, pool_rows = _pool_layers(x.reshape(t, d), b, s, pool_prev, pos0, w, n_a)
    new_kt, new_vt = _kv_new(h.reshape(b, s, d), w["kv_norm"], w["w_k"].T, w["w_v"].T)
    win_k, win_v = cache_kt, cache_vt
    for layer in range(n_a, depth):
        g = _Sel(w["norm_g"], (layer,))
        j = layer - n_a
        h = _ffn(h, g, G_FFN1, *_ffn_w(w, layer, 0))
        (q,) = _norm_proj(h, g, G_MIX, [_Sel(w["attn_w_q"], (j,))], out_cfg=((0, BF16, ATTN_SCALE),))
        q = q.reshape(b, s, N_DIL * d)
        if j == 0:
            o, win_k, win_v = _step_attention(q, win_k, win_v, new_kt, new_vt, append=True)
        else:
            (o,) = _step_attention(q, win_k, win_v, cache_kt, cache_vt, append=False)
        h = _out_proj(o.reshape(t, d), h, g, G_MIX + 1, _Sel(w["attn_w_o"], (j,)))
        h = _ffn(h, g, G_FFN2, *_ffn_w(w, layer, 1))
    return h.reshape(b, s, d), pool_rows, win_k, win_v


def kernel(x_prompt, x_sample, state_pool, cache_k, cache_v, norm_g, ffn_w_gate, ffn_w_up, ffn_w_down,
           pool_w_in, pool_w_grp, pool_scale, pool_w_out, kv_norm, w_k, w_v, attn_w_q, attn_w_o):
    d = D_MODEL
    w = dict(
        norm_g=norm_g,
        ffn_w_gate=ffn_w_gate.astype(BF16), ffn_w_up=ffn_w_up.astype(BF16), ffn_w_down=ffn_w_down.astype(BF16),
        pool_w_in=pool_w_in.astype(BF16), pool_w_grp=pool_w_grp.astype(BF16),
        pool_scale=pool_scale.reshape(-1, 1, d), pool_w_out=pool_w_out.astype(BF16),
        kv_norm=kv_norm.reshape(1, d), w_k=w_k.astype(BF16), w_v=w_v.astype(BF16),
        attn_w_q=attn_w_q.astype(BF16), attn_w_o=attn_w_o.astype(BF16),
    )
    bs, n_past = cache_k.shape[:2]
    y_p, pool_p, k_p, v_p = _prompt_trunk(x_prompt, w)

    prev_s = jnp.pad(state_pool, ((0, 0), (0, 0), (POOL_HALO - POOL_BUF, 0), (0, 0)))
    def feature_major(c):
        return jnp.transpose(c.reshape(bs, n_past, d), (0, 2, 1))

    y_s, pool_s, k_s, v_s = _sample_trunk(x_sample, prev_s, feature_major(cache_k), feature_major(cache_v),
                                          PAST_LEN, w)
    k_s, v_s = jnp.transpose(k_s, (0, 2, 1)), jnp.transpose(v_s, (0, 2, 1))
    hd = (N_HEADS, HEAD_DIM)
    return (y_p, y_s, pool_p,
            k_p.reshape(k_p.shape[:2] + hd), v_p.reshape(v_p.shape[:2] + hd),
            pool_s,
            k_s.reshape(k_s.shape[:2] + hd), v_s.reshape(v_s.shape[:2] + hd))
```

```python
import functools
import math
from typing import NamedTuple

import jax
import jax.numpy as jnp
from jax import lax
from jax.experimental import pallas as pl
from jax.experimental.pallas import tpu as pltpu

D_MODEL = 1024
D_FF = 2816
HEAD_DIM = 64
N_HEADS = D_MODEL // HEAD_DIM
POOL_WINDOWS = (2, 4, 8, 16)
POOL_GROUP_DIM = D_MODEL // len(POOL_WINDOWS)
POOL_BUF = max(POOL_WINDOWS) - 1
POOL_HALO = 16
POOL_TOP = 32
DIL_GROUPS = ((128, 1), (512, 4), (2048, 16))
N_DIL = len(DIL_GROUPS)
BAND = 128
ATTN_SUBS = 4
PAST_LEN = 8192
RMS_EPS = 1e-6
NEG_INF = -1e30
ATTN_SCALE = HEAD_DIM ** -0.5
LOG2E = 1.4426950408889634
LN2 = 0.6931471805599453
LANES = 128
N_SLABS = D_MODEL // LANES
TOKEN_TILE = 512
FFN_TILE = 512
FFN_SPLIT_ROWS = 256
R16 = 16
ROWS16 = TOKEN_TILE // R16
VMEM_LIMIT_BYTES = 56 * 1024 * 1024

F32 = jnp.float32
BF16 = jnp.bfloat16


def _params(n_axes):
    return pltpu.CompilerParams(dimension_semantics=("arbitrary",) * n_axes,
                                vmem_limit_bytes=VMEM_LIMIT_BYTES)


def _resident(shape):
    zeros = (0,) * len(shape)
    return pl.BlockSpec(shape, lambda *_: zeros, pipeline_mode=pl.Buffered(1))


class _Sel(NamedTuple):
    arr: jax.Array
    idx: tuple = ()


def _member(sel):
    lead = len(sel.idx)
    shape = sel.arr.shape[lead:]
    index = tuple(sel.idx) + (0,) * len(shape)
    return pl.BlockSpec((None,) * lead + shape, lambda *_: index, pipeline_mode=pl.Buffered(1))


def _rms(x, g):
    return x * lax.rsqrt(jnp.mean(x * x, axis=-1, keepdims=True) + RMS_EPS) * g


def _dot(a, b):
    return jnp.dot(a, b, preferred_element_type=F32)


def _dot_nt(a, b):
    return lax.dot_general(a, b, (((1,), (1,)), ((), ())), preferred_element_type=F32)


def _token_tile(t, largest=TOKEN_TILE):
    for tm in (largest, TOKEN_TILE, 256, 128, 64, 32, 16, 8):
        if t % tm == 0:
            return tm
    raise ValueError(f"token count {t} is not a multiple of 8")


SLAB_SCRATCH = pltpu.VMEM((N_SLABS, TOKEN_TILE, LANES), F32)


def _slabs_put(scr, val):
    for j in range(val.shape[-1] // LANES):
        scr[j, 0:val.shape[0], :] = val[:, j * LANES:(j + 1) * LANES]


def _slabs_get(scr, n_slabs, n_rows=TOKEN_TILE):
    return jnp.concatenate([scr[j, 0:n_rows, :] for j in range(n_slabs)], axis=-1)


def _slabs_gather(scr, n_slabs, start, n, stride):
    return jnp.concatenate([scr[j, pl.ds(start, n, stride=stride), :] for j in range(n_slabs)], axis=-1)


def _slabs_scatter(scr, start, n, stride, val):
    for j in range(val.shape[-1] // LANES):
        scr[j, pl.ds(start, n, stride=stride), :] = val[:, j * LANES:(j + 1) * LANES]


def _natural_to_by4(scr, val):
    n_slabs, m = val.shape[-1] // LANES, val.shape[0] // 4
    _slabs_put(scr, val)
    return jnp.concatenate([_slabs_gather(scr, n_slabs, r4, m, 4) for r4 in range(4)], axis=0)


def _by4_to_natural(scr, val):
    m = val.shape[0] // 4
    for r4 in range(4):
        _slabs_scatter(scr, r4, m, 4, val[r4 * m:(r4 + 1) * m])
    return _slabs_get(scr, val.shape[-1] // LANES, val.shape[0])


def _by4_to_by16(scr, val):
    n_slabs, n = val.shape[-1] // LANES, val.shape[0] // R16
    _slabs_put(scr, val)
    parts = [None] * R16
    for r4 in range(4):
        for c in range(4):
            parts[4 * c + r4] = _slabs_gather(scr, n_slabs, r4 * 4 * n + c, n, 4)
    return jnp.concatenate(parts, axis=0)


def _by16_to_by4(scr, val):
    n = val.shape[0] // R16
    for r4 in range(4):
        for c in range(4):
            r16 = 4 * c + r4
            _slabs_scatter(scr, r4 * 4 * n + c, n, 4, val[r16 * n:(r16 + 1) * n])
    return _slabs_get(scr, val.shape[-1] // LANES, val.shape[0])


def _natural_to_by16(scr, val):
    n = val.shape[0] // R16
    by16 = _by4_to_by16(scr, _natural_to_by4(scr, val))
    return [by16[r * n:(r + 1) * n] for r in range(R16)]


def _by16_to_natural(scr, val):
    return _by4_to_natural(scr, _by16_to_by4(scr, val))


def _tile16_spec(b_tiles, c):
    return pl.BlockSpec((1, R16, ROWS16, c), lambda i: (i // b_tiles, 0, i % b_tiles, 0))


def _tile4_spec(b_tiles, c):
    return pl.BlockSpec((1, 4, TOKEN_TILE // 4, c), lambda i: (i // b_tiles, 0, i % b_tiles, 0))


def _ffn_body(x_ref, g_ref, wg_ref, wu_ref, wd_ref, o_ref, *scratch, g_row, order_in, order_out):
    d = D_MODEL

    def half_step(x):
        xn = _rms(x, g_ref[g_row:g_row + 1, :]).astype(BF16)
        hg = _dot(xn, wg_ref[...])
        hu = _dot(xn, wu_ref[...])
        a = (hg / (1.0 + jnp.exp(-hg)) * hu).astype(BF16)
        f = _dot(a, wd_ref[...])
        return x + 0.5 * _rms(f, g_ref[g_row + 1:g_row + 2, :])

    n = math.prod(x_ref.shape) // d
    rows = FFN_SPLIT_ROWS if n % FFN_SPLIT_ROWS == 0 else n
    per_res = rows // R16
    for k in range(n // rows):
        if order_in == "by16":
            x = x_ref[0, :, k * per_res:(k + 1) * per_res, :].reshape(rows, d)
        else:
            x = x_ref[k * rows:(k + 1) * rows, :]
        y = half_step(x)
        if order_out == "by16" and order_in == "by16":
            o_ref[0, :, k * per_res:(k + 1) * per_res, :] = y.reshape(R16, per_res, d)
        elif order_out == "by16":
            for r, part in enumerate(_natural_to_by16(scratch[0], y)):
                o_ref[0, r, k * per_res:(k + 1) * per_res, :] = part
        elif order_in == "by16":
            o_ref[k * rows:(k + 1) * rows, :] = _by16_to_natural(scratch[0], y)
        else:
            o_ref[k * rows:(k + 1) * rows, :] = y


def _ffn(x, g, g_row, wg, wu, wd, order_in="natural", order_out="natural", seq=None):
    d = D_MODEL
    regroup = order_in != order_out
    if regroup:
        b_tiles = seq // TOKEN_TILE
        t = x.size // d
        tm = TOKEN_TILE
        nat = pl.BlockSpec((tm, d), lambda i: (i, 0))
        p16 = _tile16_spec(b_tiles, d)
        in_spec, out_spec = (nat, p16) if order_out == "by16" else (p16, nat)
        out_shape = (t // seq, R16, seq // R16, d) if order_out == "by16" else (t, d)
        scratch = [SLAB_SCRATCH]
    else:
        t = x.shape[0]
        tm = _token_tile(t, FFN_TILE)
        in_spec = out_spec = pl.BlockSpec((tm, d), lambda i: (i, 0))
        out_shape = (t, d)
        scratch = []
    return pl.pallas_call(
        functools.partial(_ffn_body, g_row=g_row, order_in=order_in, order_out=order_out),
        grid=(t // tm,),
        in_specs=[in_spec, _member(g), _member(wg), _member(wu), _member(wd)],
        out_specs=out_spec,
        out_shape=jax.ShapeDtypeStruct(out_shape, F32),
        scratch_shapes=scratch,
        compiler_params=_params(1),
        name="ffn",
    )(x, g.arr, wg.arr, wu.arr, wd.arr)


def _proj_body(x_ref, g_ref, *refs, g_row, out_cfg):
    n_w = max(wi for wi, _, _ in out_cfg) + 1
    w_refs, o_refs = refs[:n_w], refs[n_w:]
    xn = _rms(x_ref[...], g_ref[g_row:g_row + 1, :]).astype(BF16)
    ys = [_dot(xn, w[...]) for w in w_refs]
    for o_ref, (wi, _, scale) in zip(o_refs, out_cfg):
        y = ys[wi] if scale == 1.0 else ys[wi] * scale
        o_ref[...] = y.astype(o_ref.dtype)


def _norm_proj(x, g, g_row, ws, out_cfg):
    t, d = x.shape
    tm = _token_tile(t)
    tile = pl.BlockSpec((tm, d), lambda i: (i, 0))
    n_out = [w.arr.shape[-1] for w in ws]
    return pl.pallas_call(
        functools.partial(_proj_body, g_row=g_row, out_cfg=out_cfg),
        grid=(t // tm,),
        in_specs=[tile, _member(g)] + [_member(w) for w in ws],
        out_specs=[pl.BlockSpec((tm, n_out[wi]), lambda i: (i, 0)) for wi, _, _ in out_cfg],
        out_shape=[jax.ShapeDtypeStruct((t, n_out[wi]), dt) for wi, dt, _ in out_cfg],
        compiler_params=_params(1),
        name="norm_proj",
    )(x, g.arr, *[w.arr for w in ws])


def _kv_streams_body(x_ref, g_ref, wk_ref, wv_ref, k32_ref, v32_ref, kn_ref, vn_ref, k4_ref, v4_ref,
                     k16_ref, v16_ref, scr):
    xn = _rms(x_ref[...], g_ref[...]).astype(BF16)
    per_r4 = TOKEN_TILE // 4
    for w_ref, o32, on, o4, o16 in ((wk_ref, k32_ref, kn_ref, k4_ref, k16_ref),
                                    (wv_ref, v32_ref, vn_ref, v4_ref, v16_ref)):
        y = _dot(xn, w_ref[...])
        o32[0] = y
        on[...] = y.astype(BF16)
        y4 = _natural_to_by4(scr, y)
        o4[0] = y4.astype(BF16).reshape(4, per_r4, D_MODEL)
        o16[0] = _by4_to_by16(scr, y4).astype(BF16).reshape(R16, ROWS16, D_MODEL)


def _kv_streams(x, g, wk, wv, b, s):
    t, d = x.shape
    b_tiles = s // TOKEN_TILE
    nat = pl.BlockSpec((TOKEN_TILE, d), lambda i: (i, 0))
    f32n = jax.ShapeDtypeStruct((b, s, d), F32)
    nat3 = pl.BlockSpec((1, TOKEN_TILE, d), lambda i: (i // b_tiles, i % b_tiles, 0))
    bfn = jax.ShapeDtypeStruct((t, d), BF16)
    bf4 = jax.ShapeDtypeStruct((b, 4, s // 4, d), BF16)
    bf16_ = jax.ShapeDtypeStruct((b, R16, s // R16, d), BF16)
    return pl.pallas_call(
        _kv_streams_body,
        grid=(t // TOKEN_TILE,),
        in_specs=[nat, _resident(g.shape), _resident(wk.shape), _resident(wv.shape)],
        out_specs=[nat3, nat3, nat, nat, _tile4_spec(b_tiles, d), _tile4_spec(b_tiles, d),
                   _tile16_spec(b_tiles, d), _tile16_spec(b_tiles, d)],
        out_shape=[f32n, f32n, bfn, bfn, bf4, bf4, bf16_, bf16_],
        scratch_shapes=[SLAB_SCRATCH],
        compiler_params=_params(1),
        name="kv_streams",
    )(x, g, wk, wv)


def _q_streams_body(x_ref, g_ref, w_ref, q0_ref, q1_ref, q2_ref, scr, *, g_row):
    d = D_MODEL
    xn = _rms(x_ref[0].reshape(TOKEN_TILE, d), g_ref[g_row:g_row + 1, :]).astype(BF16)
    scale = ATTN_SCALE * LOG2E
    y0 = _dot(xn, w_ref[:, 0:d]) * scale
    q0_ref[0] = _by16_to_natural(scr, y0).astype(BF16)
    y1 = _dot(xn, w_ref[:, d:2 * d]) * scale
    q1_ref[0] = _by16_to_by4(scr, y1).astype(BF16).reshape(4, TOKEN_TILE // 4, d)
    y2 = _dot(xn, w_ref[:, 2 * d:3 * d]) * scale
    q2_ref[0] = y2.astype(BF16).reshape(R16, ROWS16, d)


def _q_streams(h16, g, g_row, wq):
    b, _, n16, d = h16.shape
    s = n16 * R16
    b_tiles = s // TOKEN_TILE
    return pl.pallas_call(
        functools.partial(_q_streams_body, g_row=g_row),
        grid=(b * b_tiles,),
        in_specs=[_tile16_spec(b_tiles, d), _member(g), _member(wq)],
        out_specs=[pl.BlockSpec((1, TOKEN_TILE, d), lambda i: (i // b_tiles, i % b_tiles, 0)),
                   _tile4_spec(b_tiles, d), _tile16_spec(b_tiles, d)],
        out_shape=[jax.ShapeDtypeStruct((b, s, d), BF16), jax.ShapeDtypeStruct((b, 4, s // 4, d), BF16),
                   jax.ShapeDtypeStruct((b, R16, n16, d), BF16)],
        scratch_shapes=[SLAB_SCRATCH],
        compiler_params=_params(1),
        name="q_streams",
    )(h16, g.arr, wq.arr)


def _oproj_body(o_ref, h_ref, g_ref, w_ref, out_ref, *, g_row):
    y = _dot(o_ref[...], w_ref[...])
    out_ref[...] = h_ref[...] + _rms(y, g_ref[g_row:g_row + 1, :])


def _out_proj(o, h, g, g_row, w):
    t, d = h.shape
    tm = _token_tile(t)
    tile = pl.BlockSpec((tm, d), lambda i: (i, 0))
    return pl.pallas_call(
        functools.partial(_oproj_body, g_row=g_row),
        grid=(t // tm,),
        in_specs=[tile, tile, _member(g), _member(w)],
        out_specs=tile,
        out_shape=jax.ShapeDtypeStruct((t, d), F32),
        compiler_params=_params(1),
        name="out_proj",
    )(o, h, g.arr, w.arr)


def _head_stat_lane(h):
    return (h % 2) * HEAD_DIM + h // 2


def _stat_expander():
    col_head = jnp.arange(D_MODEL) // HEAD_DIM
    row = jnp.arange(LANES)
    row_head = jnp.where(row % HEAD_DIM < N_HEADS // 2, 2 * (row % HEAD_DIM) + row // HEAD_DIM, -1)
    e = (row_head[:, None] == col_head[None, :]).astype(BF16)
    return jnp.concatenate([e, e], axis=0)


def _merge_oproj_body(o0_ref, l0_ref, o1_ref, l1_ref, o2_ref, l2_ref, h_ref, g_ref, w_ref, e_ref, out_ref, scr, *,
                      g_row):
    d = D_MODEL
    o0 = jnp.concatenate(_natural_to_by16(scr, o0_ref[0]), axis=0)
    l0 = jnp.concatenate(_natural_to_by16(scr, l0_ref[0]), axis=0)
    o1 = _by4_to_by16(scr, o1_ref[0].reshape(TOKEN_TILE, d))
    l1 = _by4_to_by16(scr, l1_ref[0].reshape(TOKEN_TILE, LANES))
    o2 = o2_ref[0].reshape(TOKEN_TILE, d)
    l2 = l2_ref[0].reshape(TOKEN_TILE, LANES)
    mm = jnp.maximum(jnp.maximum(l0, l1), l2)
    es = [jnp.exp(l - mm) for l in (l0, l1, l2)]
    den = es[0] + es[1] + es[2]
    o = None
    for e, og in zip(es, (o0, o1, o2)):
        wgt = e / den
        hi = wgt.astype(BF16)
        lo = (wgt - hi.astype(F32)).astype(BF16)
        term = _dot(jnp.concatenate([hi, lo], axis=1), e_ref[...]) * og
        o = term if o is None else o + term
    y = _dot(o.astype(BF16), w_ref[...])
    out_ref[0] = (h_ref[0].reshape(TOKEN_TILE, d) + _rms(y, g_ref[g_row:g_row + 1, :])).reshape(R16, ROWS16, d)


def _merge_out_proj(groups, h16, g, g_row, w):
    b, _, n16, d = h16.shape
    s = n16 * R16
    b_tiles = s // TOKEN_TILE
    (o0, l0), (o1, l1), (o2, l2) = groups
    expander = _stat_expander()

    def specs(c):
        return (pl.BlockSpec((1, TOKEN_TILE, c), lambda i: (i // b_tiles, i % b_tiles, 0)),
                _tile4_spec(b_tiles, c), _tile16_spec(b_tiles, c))

    (nat, p4, p16), (nat_s, p4_s, p16_s) = specs(d), specs(LANES)
    return pl.pallas_call(
        functools.partial(_merge_oproj_body, g_row=g_row),
        grid=(b * b_tiles,),
        in_specs=[nat, nat_s, p4, p4_s, p16, p16_s, p16, _member(g), _member(w), _resident(expander.shape)],
        out_specs=p16,
        out_shape=jax.ShapeDtypeStruct(h16.shape, F32),
        scratch_shapes=[SLAB_SCRATCH],
        compiler_params=_params(1),
        name="merge_out_proj",
    )(o0, l0, o1, l1, o2, l2, h16, g.arr, w.arr, expander)


def _pool_body(h_ref, prev_ref, g_ref, win_ref, wgrp_ref, scale_ref, wout_ref,
               out_ref, rows_ref, ext_ref, sa_ref, sb_ref, *, g_row, pos0, bb, ts):
    d = D_MODEL
    j = pl.program_id(1)
    top, end = POOL_TOP, POOL_TOP + ts

    @pl.when(j == 0)
    def _():
        ext_ref[:, 0:top - POOL_HALO, :] = jnp.zeros((bb, top - POOL_HALO, d), F32)
        ext_ref[:, top - POOL_HALO:top, :] = prev_ref[...]

    @pl.when(j > 0)
    def _():
        ext_ref[:, 0:top, :] = ext_ref[:, ts:end, :]

    h = h_ref[...].reshape(bb * ts, d)
    hn = _rms(h, g_ref[g_row:g_row + 1, :]).astype(BF16)
    u = _dot(hn, win_ref[...])
    ext_ref[:, top:end, :] = u.reshape(bb, ts, d)
    rows_ref[...] = ext_ref[:, end - POOL_HALO:end, :]

    def doubled(ref, cs, lo, shift):
        return ref[:, lo:end, cs] + ref[:, lo - shift:end - shift, cs]

    every = slice(None)
    pos = pos0 + j * ts + lax.broadcasted_iota(jnp.int32, (1, ts, 1), 1)
    zs = []
    for gi, win in enumerate(POOL_WINDOWS):
        cs = slice(gi * POOL_GROUP_DIM, (gi + 1) * POOL_GROUP_DIM)
        n_stage = win.bit_length() - 1
        src, src_cs = ext_ref, cs
        stage_refs = (sa_ref, sb_ref)
        for k in range(n_stage - 1):
            lo = top - 8 * (n_stage - 1 - k)
            dst = stage_refs[k % 2]
            dst[:, lo:end, :] = doubled(src, src_cs, lo, 1 << k)
            src, src_cs = dst, every
        tot = doubled(src, src_cs, top, win // 2)
        cur = ext_ref[:, top:end, cs]
        cnt = jnp.minimum(pos + 1, win).astype(F32)
        z = (tot / cnt - cur).reshape(bb * ts, POOL_GROUP_DIM).astype(BF16)
        zs.append(_dot(z, wgrp_ref[gi]))
    zc = (jnp.concatenate(zs, axis=-1) * scale_ref[...]).astype(BF16)
    y = _dot(zc, wout_ref[...])
    out_ref[...] = (h + _rms(y, g_ref[g_row + 1:g_row + 2, :])).reshape(bb, ts, d)


def _pool_mixer(h, prev, g, g_row, w_in, w_grp, scale, w_out, pos0):
    b, s, d = h.shape
    if s >= TOKEN_TILE:
        bb, ts = 1, TOKEN_TILE
    else:
        bb, ts = b, s
    n_t = s // ts
    assert s % ts == 0 and ts % 8 == 0 and (n_t == 1 or ts >= POOL_TOP)
    tile = pl.BlockSpec((bb, ts, d), lambda i, j: (i, j, 0))
    halo = pl.BlockSpec((bb, POOL_HALO, d), lambda i, j: (i, 0, 0))
    stage = pltpu.VMEM((bb, POOL_TOP + ts, POOL_GROUP_DIM), F32)
    return pl.pallas_call(
        functools.partial(_pool_body, g_row=g_row, pos0=pos0, bb=bb, ts=ts),
        grid=(b // bb, n_t),
        in_specs=[tile, halo, _member(g), _member(w_in), _member(w_grp), _member(scale), _member(w_out)],
        out_specs=[tile, halo],
        out_shape=[jax.ShapeDtypeStruct((b, s, d), F32),
                   jax.ShapeDtypeStruct((b, POOL_HALO, d), F32)],
        scratch_shapes=[pltpu.VMEM((bb, POOL_TOP + ts, d), F32), stage, stage],
        compiler_params=_params(2),
        name="pool_mixer",
    )(h, prev, g.arr, w_in.arr, w_grp.arr, scale.arr, w_out.arr)


def _alibi_slopes(n):
    return 2.0 ** (-8.0 * jnp.arange(1, n + 1, dtype=F32) / n)


def _band_table(dil, n_steps):
    qi = jnp.arange(BAND)[:, None]
    ki = jnp.arange(2 * BAND)[None, :]
    step = qi + BAND - ki
    valid = (step >= 0) & (step <= n_steps)
    bias = -_alibi_slopes(N_HEADS)[:, None, None] * (step * dil).astype(F32)[None]
    return jnp.where(valid[None], bias, NEG_INF)


def _pair_tables(dil, n_steps):
    tab = _band_table(dil, n_steps) * LOG2E
    both = tab.reshape(N_HEADS // 2, 2, BAND, 2 * BAND)
    with_prev = jnp.concatenate([both[:, 0], both[:, 1]], axis=-1)
    cur_only = jnp.concatenate([both[:, 0, :, BAND:], both[:, 1, :, BAND:]], axis=-1)
    return with_prev, cur_only


def _step_key_positions(n_past, t_new, n_keys, old_window):
    j = jnp.arange(n_keys)
    if old_window:
        return jnp.where(j < n_past + t_new, j, -1)
    pos = jnp.where(j < n_past, j + t_new, -1)
    return jnp.where((j >= n_past) & (j < n_past + t_new), j - n_past, pos)


def _step_table(n_past, t_new, n_keys, old_window):
    pos = _step_key_positions(n_past, t_new, n_keys, old_window)
    dist = n_past + jnp.arange(t_new)[:, None] - pos[None, :]
    held = (pos >= 0)[None, :]
    bias = -_alibi_slopes(N_HEADS)[:, None, None] * dist.astype(F32)[None]
    tabs = []
    for win, dil in DIL_GROUPS:
        valid = (dist >= 0) & (dist % dil == 0) & (dist <= win) & held
        tabs.append(jnp.where(valid[None], bias, NEG_INF))
    return jnp.stack(tabs, axis=1).reshape(N_HEADS * N_DIL * t_new, n_keys)


def _band_attn_body(*refs, chained):
    if chained:
        q_ref, kc_ref, vc_ref, kp_ref, vp_ref, tab2_ref, tab1_ref, o_ref, l_ref, s_ref, p_ref = refs
    else:
        q_ref, kc_ref, vc_ref, tab1_ref, o_ref, l_ref, s_ref, p_ref = refs
        kp_ref = vp_ref = tab2_ref = None
    n_pairs = N_HEADS // 2
    lead = (0,) * (len(q_ref.shape) - (2 if chained else 3))

    def rows(ref, sub):
        view = ref.at[lead]
        return view.at[pl.ds(sub * BAND, BAND)] if chained else view.at[sub]

    def attend(sub, k_prev, v_prev):
        q_v, kc_v, vc_v, o_v, l_v = (rows(r, sub) for r in (q_ref, kc_ref, vc_ref, o_ref, l_ref))
        s_v, p_v = s_ref.at[sub], p_ref.at[sub]
        with_prev = k_prev is not None
        nk = 2 * BAND if with_prev else BAND
        low = lax.broadcasted_iota(jnp.int32, (nk, LANES), 1) < HEAD_DIM
        row_low = lax.broadcasted_iota(jnp.int32, (2 * nk, LANES), 0) < nk
        lane_low = lax.broadcasted_iota(jnp.int32, (2 * nk, LANES), 1) < HEAD_DIM
        ones_st = jnp.where(row_low == lane_low, 1.0, 0.0).astype(BF16)
        lane = lax.broadcasted_iota(jnp.int32, (BAND, LANES), 1)
        q_low = lane < HEAD_DIM
        tab_ref = tab2_ref if with_prev else tab1_ref

        def stacked(c_v, p_v_, cs):
            x = c_v[:, cs]
            if with_prev:
                x = jnp.concatenate([p_v_[:, cs], x], axis=0)
            z = jnp.zeros_like(x)
            return jnp.concatenate([jnp.where(low, x, z), jnp.where(low, z, x)], axis=0)

        for j in range(n_pairs):
            cs = slice(j * LANES, (j + 1) * LANES)
            s_v[:, 2 * nk * j:2 * nk * (j + 1)] = _dot_nt(q_v[:, cs], stacked(kc_v, k_prev, cs)) + tab_ref[j]
        ms = []
        for h in range(N_HEADS):
            sh = s_v[:, nk * h:nk * (h + 1)]
            m = jnp.max(sh, axis=-1, keepdims=True)
            p_v[:, nk * h:nk * (h + 1)] = jnp.exp2(sh - m).astype(BF16)
            ms.append(m)
        stats = jnp.zeros((BAND, LANES), F32)
        for j in range(n_pairs):
            cs = slice(j * LANES, (j + 1) * LANES)
            aug = jnp.concatenate([stacked(vc_v, v_prev, cs), ones_st], axis=1)
            r = _dot(p_v[:, 2 * nk * j:2 * nk * (j + 1)], aug)
            l = r[:, LANES:]
            o_v[:, cs] = r[:, :LANES] / l
            m2 = jnp.where(q_low, jnp.broadcast_to(ms[2 * j], (BAND, LANES)),
                           jnp.broadcast_to(ms[2 * j + 1], (BAND, LANES)))
            stats = jnp.where((lane & (HEAD_DIM - 1)) == j, (m2 + jnp.log2(l)) * LN2, stats)
        l_v[...] = stats

    if chained:
        first = pl.program_id(2) == 0
        kp_v, vp_v = kp_ref.at[lead], vp_ref.at[lead]
        pl.when(first)(lambda: attend(0, None, None))
        pl.when(jnp.logical_not(first))(lambda: attend(0, kp_v, vp_v))
        for sub in range(1, ATTN_SUBS):
            attend(sub, rows(kc_ref, sub - 1), rows(vc_ref, sub - 1))
    else:
        for sub in range(ATTN_SUBS):
            attend(sub, None, None)


def _band_attn_group(q, kb, vb, gi):
    win, dil = DIL_GROUPS[gi]
    d = D_MODEL
    n = q.shape[-2]
    b = q.shape[0]
    chained = n > BAND
    step_rows = ATTN_SUBS * BAND

    def spec(c):
        if dil == 1:
            return (pl.BlockSpec((1, step_rows, c), lambda i, r, j: (i, j, 0)),
                    pl.BlockSpec((1, BAND, c), lambda i, r, j: (i, jnp.maximum(ATTN_SUBS * j - 1, 0), 0)))
        if chained:
            return (pl.BlockSpec((1, 1, step_rows, c), lambda i, r, j: (i, r, j, 0)),
                    pl.BlockSpec((1, 1, BAND, c), lambda i, r, j: (i, r, jnp.maximum(ATTN_SUBS * j - 1, 0), 0)))
        return pl.BlockSpec((1, ATTN_SUBS, BAND, c), lambda i, r, j: (i, r, 0, 0)), None

    if chained:
        assert n % step_rows == 0
        grid = (b, dil, n // step_rows)
    else:
        assert n == BAND and dil % ATTN_SUBS == 0
        grid = (b, dil // ATTN_SUBS, 1)
    cur, prv = spec(d)
    stat, _ = spec(LANES)
    tab2, tab1 = _pair_tables(dil, win // dil)
    in_specs = [cur, cur, cur]
    args = [q, kb, vb]
    if chained:
        in_specs += [prv, prv, _resident(tab2.shape)]
        args += [kb, vb, tab2]
    in_specs.append(_resident(tab1.shape))
    args.append(tab1)
    width = N_HEADS * 2 * BAND
    return pl.pallas_call(
        functools.partial(_band_attn_body, chained=chained),
        grid=grid,
        in_specs=in_specs,
        out_specs=[cur, stat],
        out_shape=[jax.ShapeDtypeStruct(q.shape, F32), jax.ShapeDtypeStruct(q.shape[:-1] + (LANES,), F32)],
        scratch_shapes=[pltpu.VMEM((ATTN_SUBS, BAND, width), F32), pltpu.VMEM((ATTN_SUBS, BAND, width), BF16)],
        compiler_params=_params(3),
        name=f"band_attn_g{gi}",
    )(*args)


STEP_HEADS = 8
STEP_COLS = STEP_HEADS * HEAD_DIM


def _step_attn_body(*refs, n_past, t_new, append):
    q_refs = refs[:N_DIL]
    if append:
        wk_ref, wv_ref, xk_ref, xv_ref, tab_ref, o_ref, ok_ref, ov_ref, kb_ref, vb_ref = refs[N_DIL:]
        outs = (ok_ref, ov_ref)
    else:
        wk_ref, wv_ref, xk_ref, xv_ref, tab_ref, o_ref, kb_ref, vb_ref = refs[N_DIL:]
        outs = (None, None)
    rows_g = N_DIL * t_new
    rows = STEP_HEADS * rows_g

    for w_ref, x_ref, out_ref, b_ref in zip((wk_ref, wv_ref), (xk_ref, xv_ref), outs, (kb_ref, vb_ref)):
        win, extra = w_ref[0], x_ref[0]
        b_ref[:, 0:n_past] = win.astype(BF16)
        b_ref[:, n_past:n_past + LANES] = extra.astype(BF16)
        if append:
            out_ref[0] = jnp.concatenate([win, extra], axis=1)[:, t_new:t_new + n_past]

    q_rows = jnp.concatenate([r[0] for r in q_refs], axis=0)
    q_all = jnp.concatenate([q_rows] * STEP_HEADS, axis=0)
    row_head = lax.broadcasted_iota(jnp.int32, (rows, STEP_COLS), 0) // rows_g
    col_head = lax.broadcasted_iota(jnp.int32, (rows, STEP_COLS), 1) // HEAD_DIM
    own = row_head == col_head
    q_bd = jnp.where(own, q_all, jnp.zeros_like(q_all))
    s = _dot(q_bd, kb_ref[...]) + tab_ref[...]
    m = jnp.max(s, axis=-1, keepdims=True)
    p = jnp.exp(s - m)
    l = jnp.sum(p, axis=-1, keepdims=True)
    o_full = _dot_nt(p.astype(BF16), vb_ref[...]) / l
    lse = m + jnp.log(l)

    out = jnp.zeros((t_new, STEP_COLS), F32)
    col = lax.broadcasted_iota(jnp.int32, (t_new, STEP_COLS), 1) // HEAD_DIM
    for hh in range(STEP_HEADS):
        base = hh * rows_g
        lses = [lse[base + g * t_new:base + (g + 1) * t_new] for g in range(N_DIL)]
        mm = functools.reduce(jnp.maximum, lses)
        ws = [jnp.exp(x - mm) for x in lses]
        den = functools.reduce(lambda a, b: a + b, ws)
        acc = functools.reduce(
            lambda a, b: a + b,
            [ws[g] * o_full[base + g * t_new:base + (g + 1) * t_new] for g in range(N_DIL)])
        out = jnp.where(col == hh, acc / den, out)
    o_ref[0] = out.astype(o_ref.dtype)


def _step_attention(q, win_k, win_v, extra_k, extra_v, append):
    b, t_new, _ = q.shape
    n_past = win_k.shape[2]
    d = D_MODEL
    assert t_new % 8 == 0 and t_new <= LANES and n_past % LANES == 0
    n_keys = n_past + LANES
    rows = STEP_HEADS * N_DIL * t_new
    tab = _step_table(n_past, t_new, n_keys, append)
    n_hg = N_HEADS // STEP_HEADS

    def q_spec(g):
        return pl.BlockSpec((1, t_new, STEP_COLS), lambda j, i: (i, 0, g * n_hg + j))

    win_spec = pl.BlockSpec((1, STEP_COLS, n_past), lambda j, i: (i, j, 0))
    extra_spec = pl.BlockSpec((1, STEP_COLS, LANES), lambda j, i: (i, j, 0))
    out_specs = [pl.BlockSpec((1, t_new, STEP_COLS), lambda j, i: (i, 0, j))]
    out_shape = [jax.ShapeDtypeStruct((b, t_new, d), BF16)]
    if append:
        out_specs += [win_spec, win_spec]
        out_shape += [jax.ShapeDtypeStruct((b, d, n_past), F32)] * 2
    return pl.pallas_call(
        functools.partial(_step_attn_body, n_past=n_past, t_new=t_new, append=append),
        grid=(n_hg, b),
        in_specs=[q_spec(g) for g in range(N_DIL)] + [win_spec, win_spec, extra_spec, extra_spec,
                                                     pl.BlockSpec((rows, n_keys), lambda j, i: (j, 0))],
        out_specs=out_specs,
        out_shape=out_shape,
        scratch_shapes=[pltpu.VMEM((STEP_COLS, n_keys), BF16)] * 2,
        compiler_params=_params(2),
        name="step_attn",
    )(q, q, q, win_k, win_v, extra_k, extra_v, tab)


G_FFN1, G_MIX, G_FFN2 = 0, 2, 4


def _ffn_w(w, layer, half):
    idx = (layer, half)
    return _Sel(w["ffn_w_gate"], idx), _Sel(w["ffn_w_up"], idx), _Sel(w["ffn_w_down"], idx)


def _pool_layers(h, b, s, pool_prev, pos0, w, n_a):
    d = D_MODEL
    pool_rows = []
    for layer in range(n_a):
        g = _Sel(w["norm_g"], (layer,))
        h = _ffn(h, g, G_FFN1, *_ffn_w(w, layer, 0))
        h3, rows = _pool_mixer(h.reshape(b, s, d), pool_prev[layer], g, G_MIX, _Sel(w["pool_w_in"], (layer,)),
                               _Sel(w["pool_w_grp"], (layer,)), _Sel(w["pool_scale"], (layer,)),
                               _Sel(w["pool_w_out"], (layer,)), pos0)
        h = h3.reshape(b * s, d)
        pool_rows.append(rows[:, POOL_HALO - POOL_BUF:])
        h = _ffn(h, g, G_FFN2, *_ffn_w(w, layer, 1))
    return h, jnp.stack(pool_rows)


def _prompt_trunk(x, w):
    b, s, d = x.shape
    t = b * s
    depth = w["norm_g"].shape[0]
    n_a = depth // 2
    assert s % (R16 * BAND) == 0 and s % TOKEN_TILE == 0
    empty = jnp.zeros((n_a, b, POOL_HALO, d), F32)
    h, pool_rows = _pool_layers(x.reshape(t, d), b, s, empty, 0, w, n_a)
    k32, v32, kn, vn, k4, v4, k16, v16 = _kv_streams(h, w["kv_norm"], w["w_k"], w["w_v"], b, s)
    kn, vn = kn.reshape(b, s, d), vn.reshape(b, s, d)
    for layer in range(n_a, depth):
        g = _Sel(w["norm_g"], (layer,))
        j = layer - n_a
        if layer == n_a:
            h = _ffn(h, g, G_FFN1, *_ffn_w(w, layer, 0), order_in="natural", order_out="by16", seq=s)
        else:
            h = _ffn(h, g, G_FFN1, *_ffn_w(w, layer, 0))
            h = h.reshape(b, R16, s // R16, d)
        q0, q1, q2 = _q_streams(h, g, G_MIX, _Sel(w["attn_w_q"], (j,)))
        groups = [_band_attn_group(q0, kn, vn, 0), _band_attn_group(q1, k4, v4, 1),
                  _band_attn_group(q2, k16, v16, 2)]
        h = _merge_out_proj(groups, h, g, G_MIX + 1, _Sel(w["attn_w_o"], (j,)))
        if layer == depth - 1:
            h = _ffn(h, g, G_FFN2, *_ffn_w(w, layer, 1), order_in="by16", order_out="natural", seq=s)
        else:
            h = _ffn(h.reshape(t, d), g, G_FFN2, *_ffn_w(w, layer, 1))
    buf = min(max(win for win, _ in DIL_GROUPS), s)
    k32, v32 = k32[:, s - buf:], v32[:, s - buf:]
    return h.reshape(b, s, d), pool_rows, k32, v32


def _kv_new_body(x_ref, g_ref, wk_ref, wv_ref, k_ref, v_ref, *, t_new):
    per_block = LANES // t_new
    xn = _rms(x_ref[...], g_ref[...]).astype(BF16)
    first = lax.broadcasted_iota(jnp.int32, (D_MODEL, LANES), 1) < t_new
    for w_ref, o_ref in ((wk_ref, k_ref), (wv_ref, v_ref)):
        y = _dot_nt(w_ref[...], xn)
        for i in range(per_block):
            moved = y if i == 0 else pltpu.roll(y, LANES - i * t_new, 1)
            o_ref[i] = jnp.where(first, moved, 0.0)


def _kv_new(x, g, wk_t, wv_t):
    b, t_new, d = x.shape
    assert LANES % t_new == 0
    bb = LANES // t_new
    n_steps = -(-b // bb)
    rows = x.reshape(b * t_new, d)
    if n_steps * bb != b:
        rows = jnp.pad(rows, ((0, (n_steps * bb - b) * t_new), (0, 0)))
    out = jax.ShapeDtypeStruct((n_steps * bb, d, LANES), F32)
    out_spec = pl.BlockSpec((bb, d, LANES), lambda i: (i, 0, 0))
    k_t, v_t = pl.pallas_call(
        functools.partial(_kv_new_body, t_new=t_new),
        grid=(n_steps,),
        in_specs=[pl.BlockSpec((LANES, d), lambda i: (i, 0)), _resident(g.shape), _resident(wk_t.shape),
                  _resident(wv_t.shape)],
        out_specs=[out_spec, out_spec],
        out_shape=[out, out],
        compiler_params=_params(1),
        name="kv_new",
    )(rows, g, wk_t, wv_t)
    return k_t[:b], v_t[:b]


def _sample_trunk(x, pool_prev, cache_kt, cache_vt, pos0, w):
    b, s, d = x.shape
    t = b * s
    depth = w["norm_g"].shape[0]
    n_a = depth // 2
    h, pool_rows = _pool_layers(x.reshape(t, d), b, s, pool_prev, pos0, w, n_a)
    new_kt, new_vt = _kv_new(h.reshape(b, s, d), w["kv_norm"], w["w_k"].T, w["w_v"].T)
    win_k, win_v = cache_kt, cache_vt
    for layer in range(n_a, depth):
        g = _Sel(w["norm_g"], (layer,))
        j = layer - n_a
        h = _ffn(h, g, G_FFN1, *_ffn_w(w, layer, 0))
        (q,) = _norm_proj(h, g, G_MIX, [_Sel(w["attn_w_q"], (j,))], out_cfg=((0, BF16, ATTN_SCALE),))
        q = q.reshape(b, s, N_DIL * d)
        if j == 0:
            o, win_k, win_v = _step_attention(q, win_k, win_v, new_kt, new_vt, append=True)
        else:
            (o,) = _step_attention(q, win_k, win_v, cache_kt, cache_vt, append=False)
        h = _out_proj(o.reshape(t, d), h, g, G_MIX + 1, _Sel(w["attn_w_o"], (j,)))
        h = _ffn(h, g, G_FFN2, *_ffn_w(w, layer, 1))
    return h.reshape(b, s, d), pool_rows, win_k, win_v


def kernel(x_prompt, x_sample, state_pool, cache_k, cache_v, norm_g, ffn_w_gate, ffn_w_up, ffn_w_down,
           pool_w_in, pool_w_grp, pool_scale, pool_w_out, kv_norm, w_k, w_v, attn_w_q, attn_w_o):
    d = D_MODEL
    w = dict(
        norm_g=norm_g,
        ffn_w_gate=ffn_w_gate.astype(BF16), ffn_w_up=ffn_w_up.astype(BF16), ffn_w_down=ffn_w_down.astype(BF16),
        pool_w_in=pool_w_in.astype(BF16), pool_w_grp=pool_w_grp.astype(BF16),
        pool_scale=pool_scale.reshape(-1, 1, d), pool_w_out=pool_w_out.astype(BF16),
        kv_norm=kv_norm.reshape(1, d), w_k=w_k.astype(BF16), w_v=w_v.astype(BF16),
        attn_w_q=attn_w_q.astype(BF16), attn_w_o=attn_w_o.astype(BF16),
    )
    bs, n_past = cache_k.shape[:2]
    y_p, pool_p, k_p, v_p = _prompt_trunk(x_prompt, w)

    prev_s = jnp.pad(state_pool, ((0, 0), (0, 0), (POOL_HALO - POOL_BUF, 0), (0, 0)))
    def feature_major(c):
        return jnp.transpose(c.reshape(bs, n_past, d), (0, 2, 1))

    y_s, pool_s, k_s, v_s = _sample_trunk(x_sample, prev_s, feature_major(cache_k), feature_major(cache_v),
                                          PAST_LEN, w)
    k_s, v_s = jnp.transpose(k_s, (0, 2, 1)), jnp.transpose(v_s, (0, 2, 1))
    hd = (N_HEADS, HEAD_DIM)
    return (y_p, y_s, pool_p,
            k_p.reshape(k_p.shape[:2] + hd), v_p.reshape(v_p.shape[:2] + hd),
            pool_s,
            k_s.reshape(k_s.shape[:2] + hd), v_s.reshape(v_s.shape[:2] + hd))
```
